```python
import jax, jax.numpy as jnp
from jax import lax
import numpy as np

D_MODEL = 1024
BATCH = 16
SEQ = 256
DEPTH = 2
DEC_BATCH = 4
DEC_SEQ = 2048
PAST_LEN = 256

F32 = jnp.float32
NEG = -1e30
GRID_W = 64
HEAD_DIM = 64
SCALE = HEAD_DIM ** -0.5
ROPE_BASE = 10000.0
EPS = 1e-6
HG_HEADS = 4
HG_DK = 64
HG_DV = 64
HG_W = HG_HEADS * HG_DK
HG_CHUNK = 16
WIN_HEADS = 6
WIN_KV = 2
WIN = 128
POOL_GROUPS = 4
POOL_GC = 64
POOL_W = POOL_GROUPS * POOL_GC
POOL_SIZES = (2, 4, 8, 16)
NA_HEADS = 4
NA_KH = 8
NA_KW = 16
NA_QW = 16
NA_SPAN = NA_QW + NA_KW
QBLK = 128
N_BRANCH = 4
BRANCH_W = (HG_W, WIN_HEADS * HEAD_DIM, POOL_W, NA_HEADS * HEAD_DIM)
MIX_W = sum(BRANCH_W)
IN_SIZES = (HG_W, HG_W, HG_W, HG_W, HG_W,
            WIN_HEADS * HEAD_DIM, WIN_KV * HEAD_DIM, WIN_KV * HEAD_DIM,
            POOL_W,
            NA_HEADS * HEAD_DIM, NA_HEADS * HEAD_DIM, NA_HEADS * HEAD_DIM)
IN_COLS = sum(IN_SIZES)
N_EXPERTS = 32
TOP_K = 4
D_EXPERT = 256
D_SHARED = 256
ROUTED_SCALE = 2.5
MOE_BLK = 128

kernel_name = 'hybrid_diffusion_ctx_prefix_step'


def _split(x, sizes):
    return jnp.split(x, np.cumsum(sizes)[:-1].tolist(), axis=-1)


def rms_norm(x, g):
    xf = x.astype(F32)
    y = xf * lax.rsqrt(jnp.mean(xf * xf, axis=-1, keepdims=True) + EPS)
    return (y * g.astype(F32)).astype(x.dtype)


def modulation(cvec, w_mod, b_mod):
    m = jax.nn.silu(cvec) @ w_mod + b_mod
    return jnp.split(m[:, None, :], 6, axis=-1)


def axial_rope(x):
    n_ = x.shape[1]
    t = np.arange(n_)
    half = HEAD_DIM // 2
    quarter = half // 2
    inv = (ROPE_BASE ** (-np.arange(quarter) / quarter)).astype(np.float32)
    xf = x.astype(F32)

    def rot(xp, pos):
        ang = pos.astype(np.float32)[:, None] * inv[None]
        cos = jnp.asarray(np.cos(ang), F32)[None, :, None, :]
        sin = jnp.asarray(np.sin(ang), F32)[None, :, None, :]
        x1, x2 = xp[..., :quarter], xp[..., quarter:]
        return jnp.concatenate([x1 * cos - x2 * sin, x2 * cos + x1 * sin], axis=-1)

    out = jnp.concatenate([rot(xf[..., :half], t // GRID_W), rot(xf[..., half:], t % GRID_W)], axis=-1)
    return out.astype(x.dtype)


def hgrn_scan(q, logf, k, v, s0):
    b_, t_, h_, dk = q.shape
    dv = v.shape[-1]
    n = t_ // HG_CHUNK
    q, logf, k = (a.reshape(b_, n, HG_CHUNK, h_, dk) for a in (q, logf, k))
    v = v.reshape(b_, n, HG_CHUNK, h_, dv)
    cum = jnp.cumsum(logf, axis=2)
    last = cum[:, :, -1]
    causal = np.tril(np.ones((HG_CHUNK, HG_CHUNK), dtype=bool))
    diff = cum[:, :, :, None] - cum[:, :, None, :]
    decay = jnp.exp(jnp.where(causal[:, :, None, None], diff, NEG))
    att = jnp.einsum('bntshk,bnshk->bnhts', decay * q[:, :, :, None], k)
    o_intra = jnp.einsum('bnhts,bnshv->bnthv', att, v)
    ds = jnp.einsum('bnshk,bnshv->bnhkv', k * jnp.exp(last[:, :, None] - cum), v)

    def step(s, inp):
        dec, d = inp
        return dec[..., None] * s + d, s

    s_fin, s_start = lax.scan(step, s0, (jnp.moveaxis(jnp.exp(last), 1, 0), jnp.moveaxis(ds, 1, 0)))
    o_inter = jnp.einsum('bnthk,nbhkv->bnthv', q * jnp.exp(cum), s_start)
    return (o_intra + o_inter).reshape(b_, t_, h_, dv), s_fin


def hgrn_mixer(zq, zf_fwd, zf_bwd, zi, zg, lb, onorm, s0):
    b_, t_, _ = zq.shape
    hd = lambda a: a.astype(F32).reshape(b_, t_, HG_HEADS, -1)
    q, i = hd(zq), hd(zi)

    def gate(z, lbd):
        lbd = lbd.astype(F32).reshape(HG_HEADS, HG_DK)
        f = lbd + (1.0 - lbd) * jax.nn.sigmoid(hd(z))
        return jnp.log(f), 1.0 - f

    lf_f, k_f = gate(zf_fwd, lb[0])
    lf_b, k_b = gate(zf_bwd, lb[1])
    o_f, s_f = hgrn_scan(q, lf_f, k_f, i, s0[:, 0])
    rev = lambda a: jnp.flip(a, axis=1)
    o_b, s_b = hgrn_scan(rev(q), rev(lf_b), rev(k_b), rev(i), s0[:, 1])
    o = o_f + rev(o_b)
    o = rms_norm(o, onorm) * jax.nn.silu(hd(zg))
    return o.reshape(b_, t_, HG_W).astype(zq.dtype), jnp.stack([s_f, s_b], axis=1)


def pool_mixer(u, w_pool, scale):
    b_, t_, _ = u.shape
    uf = u.astype(F32).reshape(b_, t_, POOL_GROUPS, POOL_GC)
    csum = jnp.concatenate([jnp.zeros((b_, 1, POOL_GROUPS, POOL_GC), F32), jnp.cumsum(uf, axis=1)], axis=1)
    pos = np.arange(t_)[:, None]
    half = np.array(POOL_SIZES)[None] // 2
    lo = np.clip(pos - half, 0, t_)
    hi = np.clip(pos + half, 0, t_)
    grp = np.arange(POOL_GROUPS)[None]
    win_sum = csum[:, hi, grp] - csum[:, lo, grp]
    mean = win_sum / (hi - lo).astype(np.float32)[None, :, :, None]
    y = jnp.einsum('btgc,gcd->btgd', mean - uf, w_pool.astype(F32))
    y = y * scale.astype(F32).reshape(POOL_GROUPS, POOL_GC)
    return y.reshape(b_, t_, POOL_W).astype(u.dtype)


def full_attn(q, k, v, sink):
    b_, l_, hq, d = q.shape
    hkv = k.shape[2]
    g = hq // hkv
    qb = jnp.moveaxis(q.reshape(b_, l_ // QBLK, QBLK, hkv, g, d), 1, 0)

    def one(qi):
        s = jnp.einsum('bqhgd,bkhd->bhgqk', qi, k, preferred_element_type=F32) * SCALE
        if sink is not None:
            sk = jnp.broadcast_to(sink.astype(F32).reshape(1, hkv, g, 1, 1), s.shape[:-1] + (1,))
            s = jnp.concatenate([s, sk], axis=-1)
        p = jax.nn.softmax(s, axis=-1)[..., :l_].astype(v.dtype)
        return jnp.einsum('bhgqk,bkhd->bqhgd', p, v)

    o = lax.map(one, qb)
    return jnp.moveaxis(o, 0, 1).reshape(b_, l_, hq * d)


def window_attn(q, k, v, kc, vc, sink):
    b_, n_, hq, d = q.shape
    hkv = k.shape[2]
    g = hq // hkv
    nb = n_ // WIN
    qb = q.reshape(b_, nb, WIN, hkv, g, d)

    def band(a):
        ap = jnp.pad(a, ((0, 0), (WIN, WIN), (0, 0), (0, 0))).reshape(b_, nb + 2, WIN, hkv, d)
        return jnp.concatenate([ap[:, :-2], ap[:, 1:-1], ap[:, 2:]], axis=2)

    kb, vb = band(k), band(v)
    qpos = np.arange(nb)[:, None] * WIN + np.arange(WIN)[None]
    kpos = (np.arange(nb)[:, None] - 1) * WIN + np.arange(3 * WIN)[None]
    kp = kpos[:, None, :]
    valid = (np.abs(kp - qpos[:, :, None]) <= WIN) & (kp >= 0) & (kp < n_)
    s_loc = jnp.einsum('bnqhgd,bnkhd->bnhgqk', qb, kb, preferred_element_type=F32) * SCALE
    s_loc = jnp.where(valid[None, :, None, None], s_loc, NEG)
    s_ctx = jnp.einsum('bnqhgd,blhd->bnhgql', qb, kc, preferred_element_type=F32) * SCALE
    s_sink = jnp.broadcast_to(sink.astype(F32).reshape(1, 1, hkv, g, 1, 1), s_loc.shape[:-1] + (1,))
    p = jax.nn.softmax(jnp.concatenate([s_loc, s_ctx, s_sink], axis=-1), axis=-1).astype(v.dtype)
    nk = 3 * WIN
    o = (jnp.einsum('bnhgqk,bnkhd->bnqhgd', p[..., :nk], vb)
         + jnp.einsum('bnhgql,blhd->bnqhgd', p[..., nk:nk + kc.shape[1]], vc))
    return o.reshape(b_, n_, hq * d)


def natten(q, k, v, kc, vc, rpb):
    b_, n_, h_, d = q.shape
    rows = n_ // GRID_W
    kh = min(NA_KH, rows)
    ncb = GRID_W // NA_QW
    r = np.arange(rows)
    krow = np.clip(r - kh // 2, 0, rows - kh)[:, None] + np.arange(kh)[None]
    cb = np.arange(ncb)
    qcol = cb[:, None] * NA_QW + np.arange(NA_QW)[None]
    kcol = np.clip(cb * NA_QW - NA_KW // 2, 0, GRID_W - NA_SPAN)[:, None] + np.arange(NA_SPAN)[None]
    cstart = np.clip(qcol - NA_KW // 2, 0, GRID_W - NA_KW)
    in_win = (kcol[:, None, :] >= cstart[:, :, None]) & (kcol[:, None, :] < cstart[:, :, None] + NA_KW)
    dr = krow - r[:, None] + NA_KH - 1
    dc = np.clip(kcol[:, None, :] - qcol[:, :, None] + NA_KW - 1, 0, 2 * NA_KW - 2)
    nk = kh * NA_SPAN
    bias = rpb.astype(F32)[:, dr[:, None, None, :, None], dc[None, :, :, None, :]]
    bias = jnp.transpose(bias, (1, 2, 0, 3, 4, 5)).reshape(rows, ncb, h_, NA_QW, nk)
    mask = np.broadcast_to(in_win[None, :, None, :, None, :],
                           (rows, ncb, 1, NA_QW, kh, NA_SPAN)).reshape(rows, ncb, 1, NA_QW, nk)

    def gather(a):
        ag = a.reshape(b_, rows, GRID_W, h_, d)[:, krow[:, None, :, None], kcol[None, :, None, :]]
        return ag.reshape(b_, rows, ncb, nk, h_, d)

    kblk, vblk = gather(k), gather(v)
    qb = q.reshape(b_, rows, ncb, NA_QW, h_, d)
    s_loc = jnp.einsum('brcqhd,brckhd->brchqk', qb, kblk, preferred_element_type=F32) * SCALE + bias[None]
    s_loc = jnp.where(mask[None], s_loc, NEG)
    s_ctx = jnp.einsum('brcqhd,blhd->brchql', qb, kc, preferred_element_type=F32) * SCALE
    p = jax.nn.softmax(jnp.concatenate([s_loc, s_ctx], axis=-1), axis=-1).astype(v.dtype)
    o = (jnp.einsum('brchqk,brckhd->brcqhd', p[..., :nk], vblk)
         + jnp.einsum('brchql,blhd->brcqhd', p[..., nk:], vc))
    return o.reshape(b_, n_, h_ * d)


def moe(h, w_router, b_router, w_eg, w_eu, w_ed, w_sg, w_su, w_sd):
    b_, t_, d = h.shape
    tok = h.reshape(-1, d)
    scores = jax.nn.sigmoid(jnp.einsum('nd,de->ne', tok, w_router, preferred_element_type=F32))
    _, idx = lax.top_k(scores + b_router.astype(F32), TOP_K)
    sel = jnp.take_along_axis(scores, idx, axis=-1)
    wts = ROUTED_SCALE * sel / jnp.sum(sel, axis=-1, keepdims=True)
    gate = jnp.einsum('nk,nke->ne', wts, jax.nn.one_hot(idx, N_EXPERTS, dtype=F32)).astype(h.dtype)

    def block(args):
        tb, gb = args
        a = jax.nn.silu(jnp.einsum('nd,edf->nef', tb, w_eg)) * jnp.einsum('nd,edf->nef', tb, w_eu)
        return jnp.einsum('nef,efd->nd', a * gb[..., None], w_ed)

    nblk = tok.shape[0] // MOE_BLK
    routed = lax.map(block, (tok.reshape(nblk, MOE_BLK, d), gate.reshape(nblk, MOE_BLK, N_EXPERTS)))
    shared = (jax.nn.silu(tok @ w_sg) * (tok @ w_su)) @ w_sd
    return (routed.reshape(-1, d) + shared).reshape(b_, t_, d)


def trunk_layer(x, cvec, lw, lb, ctx):
    bsz, t_, _ = x.shape
    sh1, sc1, g1, sh2, sc2, g2 = modulation(cvec, lw['w_mod'], lw['b_mod'])
    h = rms_norm(x, lw['norm1_g']) * (1 + sc1) + sh1
    (aq, af_f, af_b, ai, ag, bq, bk, bv, cu, dq, dk, dv) = _split(h @ lw['w_in'], IN_SIZES)
    heads = lambda a, n: a.reshape(bsz, t_, n, HEAD_DIM)
    latent = ctx is not None
    s0 = ctx[4].astype(F32) if latent else jnp.zeros((bsz, 2, HG_HEADS, HG_DK, HG_DV), F32)
    y_a, s_fin = hgrn_mixer(aq, af_f, af_b, ai, ag, lb, lw['hg_onorm'], s0)
    q_b = rms_norm(heads(bq, WIN_HEADS), lw['win_qn'])
    k_b = rms_norm(heads(bk, WIN_KV), lw['win_kn'])
    v_b = heads(bv, WIN_KV)
    q_d = rms_norm(heads(dq, NA_HEADS), lw['na_qn'])
    k_d = rms_norm(heads(dk, NA_HEADS), lw['na_kn'])
    v_d = heads(dv, NA_HEADS)
    y_c = pool_mixer(cu, lw['pool_w'], lw['pool_scale'])
    if latent:
        y_b = window_attn(axial_rope(q_b), axial_rope(k_b), v_b, ctx[0], ctx[1], lw['win_sink'])
        y_d = natten(q_d, k_d, v_d, ctx[2], ctx[3], lw['na_rpb'])
    else:
        y_b = full_attn(q_b, k_b, v_b, lw['win_sink'])
        y_d = full_attn(q_d, k_d, v_d, None)
    g_a, g_b, g_c, g_d = jnp.split(jax.nn.sigmoid(h @ lw['w_mgate'] + lw['b_mgate']), N_BRANCH, axis=-1)
    offs = np.concatenate([[0], np.cumsum(BRANCH_W)])
    wb = lw['w_branch']
    merged = (g_a * (y_a @ wb[offs[0]:offs[1]]) + g_b * (y_b @ wb[offs[1]:offs[2]])
              + g_c * (y_c @ wb[offs[2]:offs[3]]) + g_d * (y_d @ wb[offs[3]:offs[4]]))
    x = x + g1 * (merged @ lw['w_out'])
    h2 = rms_norm(x, lw['norm2_g']) * (1 + sc2) + sh2
    x = x + g2 * moe(h2, lw['w_router'], lw['b_router'], lw['w_eg'], lw['w_eu'], lw['w_ed'],
                     lw['w_sg'], lw['w_su'], lw['w_sd'])
    return x, (k_b, v_b, k_d, v_d, s_fin)


def setup_inputs(seed: int = 0) -> dict:
    key = jax.random.key(seed)
    ks = iter(jax.random.split(key, 40))
    nrm = lambda shape, s: jax.random.normal(next(ks), shape, F32) * s
    gain = lambda shape: 1.0 + nrm(shape, 0.1)
    D = D_MODEL
    return {
        'x_prompt': nrm((BATCH, SEQ, D), 1.0),
        'x_sample': nrm((DEC_BATCH, DEC_SEQ, D), 1.0),
        'cache_win_k': nrm((DEC_BATCH, DEPTH, PAST_LEN, WIN_KV, HEAD_DIM), 1.0),
        'cache_win_v': nrm((DEC_BATCH, DEPTH, PAST_LEN, WIN_KV, HEAD_DIM), 1.0),
        'cache_na_k': nrm((DEC_BATCH, DEPTH, PAST_LEN, NA_HEADS, HEAD_DIM), 1.0),
        'cache_na_v': nrm((DEC_BATCH, DEPTH, PAST_LEN, NA_HEADS, HEAD_DIM), 1.0),
        'state_hgrn': nrm((DEC_BATCH, DEPTH, 2, HG_HEADS, HG_DK, HG_DV), 0.3),
        'c': nrm((DEC_BATCH, D), 1.0),
        'c_ctx': nrm((D,), 1.0),
        'w_mod': nrm((DEPTH, D, 6 * D), 0.5 * D ** -0.5),
        'b_mod': nrm((DEPTH, 6 * D), 0.02),
        'norm1_g': gain((DEPTH, D)),
        'norm2_g': gain((DEPTH, D)),
        'w_in': nrm((DEPTH, D, IN_COLS), D ** -0.5),
        'w_mgate': nrm((DEPTH, D, N_BRANCH * D), D ** -0.5),
        'b_mgate': nrm((DEPTH, N_BRANCH * D), 0.02),
        'hg_lb': nrm((DEPTH, 2, HG_W), 1.0),
        'hg_onorm': gain((DEPTH, HG_DV)),
        'win_qn': gain((DEPTH, HEAD_DIM)),
        'win_kn': gain((DEPTH, HEAD_DIM)),
        'win_sink': nrm((DEPTH, WIN_HEADS), 1.0),
        'pool_w': nrm((DEPTH, POOL_GROUPS, POOL_GC, POOL_GC), POOL_GC ** -0.5),
        'pool_scale': gain((DEPTH, POOL_W)),
        'na_qn': gain((DEPTH, HEAD_DIM)),
        'na_kn': gain((DEPTH, HEAD_DIM)),
        'na_rpb': nrm((DEPTH, NA_HEADS, 2 * NA_KH - 1, 2 * NA_KW - 1), 0.5),
        'w_branch': nrm((DEPTH, MIX_W, D), (MIX_W // N_BRANCH) ** -0.5),
        'w_out': nrm((DEPTH, D, D), D ** -0.5),
        'w_router': nrm((DEPTH, D, N_EXPERTS), D ** -0.5),
        'b_router': nrm((DEPTH, N_EXPERTS), 0.01),
        'w_eg': nrm((DEPTH, N_EXPERTS, D, D_EXPERT), D ** -0.5),
        'w_eu': nrm((DEPTH, N_EXPERTS, D, D_EXPERT), D ** -0.5),
        'w_ed': nrm((DEPTH, N_EXPERTS, D_EXPERT, D), D_EXPERT ** -0.5),
        'w_sg': nrm((DEPTH, D, D_SHARED), D ** -0.5),
        'w_su': nrm((DEPTH, D, D_SHARED), D ** -0.5),
        'w_sd': nrm((DEPTH, D_SHARED, D), D_SHARED ** -0.5),
    }


def reference(x_prompt, x_sample, cache_win_k, cache_win_v, cache_na_k, cache_na_v, state_hgrn,
              c, c_ctx, w_mod, b_mod, norm1_g, norm2_g, w_in, w_mgate, b_mgate, hg_lb, hg_onorm,
              win_qn, win_kn, win_sink, pool_w, pool_scale, na_qn, na_kn, na_rpb, w_branch, w_out,
              w_router, b_router, w_eg, w_eu, w_ed, w_sg, w_su, w_sd):
    lbp = jax.nn.softmax(hg_lb.astype(F32), axis=0)
    lbs = jnp.cumsum(lbp, axis=0) - lbp[0:1]
    y_p, y_s = x_prompt, x_sample
    ctx_out = []
    for l in range(DEPTH):
        lw = {
            'w_mod': w_mod[l], 'b_mod': b_mod[l], 'norm1_g': norm1_g[l], 'norm2_g': norm2_g[l],
            'w_in': w_in[l], 'w_mgate': w_mgate[l], 'b_mgate': b_mgate[l], 'hg_onorm': hg_onorm[l],
            'win_qn': win_qn[l], 'win_kn': win_kn[l], 'win_sink': win_sink[l],
            'pool_w': pool_w[l], 'pool_scale': pool_scale[l],
            'na_qn': na_qn[l], 'na_kn': na_kn[l], 'na_rpb': na_rpb[l],
            'w_branch': w_branch[l], 'w_out': w_out[l],
            'w_router': w_router[l], 'b_router': b_router[l],
            'w_eg': w_eg[l], 'w_eu': w_eu[l], 'w_ed': w_ed[l],
            'w_sg': w_sg[l], 'w_su': w_su[l], 'w_sd': w_sd[l],
        }
        y_p, outs = trunk_layer(y_p, c_ctx[None], lw, lbs[l], None)
        ctx_out.append(outs)
        y_s, _ = trunk_layer(y_s, c, lw, lbs[l],
                             (cache_win_k[:, l], cache_win_v[:, l], cache_na_k[:, l], cache_na_v[:, l],
                              state_hgrn[:, l]))
    new_win_k = jnp.stack([o[0] for o in ctx_out], axis=1)
    new_win_v = jnp.stack([o[1] for o in ctx_out], axis=1)
    new_na_k = jnp.stack([o[2] for o in ctx_out], axis=1)
    new_na_v = jnp.stack([o[3] for o in ctx_out], axis=1)
    new_hgrn = jnp.stack([o[4] for o in ctx_out], axis=1).astype(x_prompt.dtype)
    return (y_p, y_s, new_win_k, new_win_v, new_na_k, new_na_v, new_hgrn)
```

```python
import functools

import numpy as np
import jax
import jax.numpy as jnp
from jax import lax
from jax.experimental import pallas as pl
from jax.experimental.pallas import tpu as pltpu

F32 = jnp.float32
BF16 = jnp.bfloat16

D_MODEL = 1024
BATCH = 16
SEQ = 256
DEPTH = 2
DEC_BATCH = 4
DEC_SEQ = 2048
PAST_LEN = 256
NEG = -1e30
GRID_W = 64
HEAD_DIM = 64
SCALE = HEAD_DIM ** -0.5
ROPE_BASE = 10000.0
EPS = 1e-6
HG_HEADS = 4
HG_W = 256
WIN_HEADS = 6
WIN_KV = 2
WIN = 128
POOL_SIZES = (2, 4, 8, 16)
POOL_W = 256
NA_HEADS = 4
NA_KH = 8
NA_KW = 16
N_EXPERTS = 32
TOP_K = 4
D_EXPERT = 256
ROUTED_SCALE = 2.5

N_CTX = BATCH * SEQ
N_LAT = DEC_BATCH * DEC_SEQ
NTOK = N_CTX + N_LAT
WQ_B = WIN_HEADS * HEAD_DIM
WK_B = WIN_KV * HEAD_DIM
WD = NA_HEADS * HEAD_DIM

Z_W = 3072
COL_AQ, COL_AFF, COL_AFB, COL_AI, COL_AG, COL_CU, COL_DQ, COL_DK, COL_DV = (
    0, 256, 512, 768, 1024, 1280, 1536, 1792, 2048)
COL_BQ, COL_BK, COL_BV = 2304, 2688, 2816
GATE_W = 4 * D_MODEL

TM = 512
TN = 512
HB = 128
VMEM_LIMIT = 56 * 1024 * 1024


def _params(sem, vmem=VMEM_LIMIT):
    return pltpu.CompilerParams(dimension_semantics=sem, vmem_limit_bytes=vmem)


def _seg_of_tile(i, tile):
    nct = N_CTX // tile
    per = DEC_SEQ // tile
    return jnp.where(i < nct, 0, 1 + (i - nct) // per)


def _bd_ones(w):
    idx = np.arange(w) // HEAD_DIM
    return (idx[:, None] == idx[None, :]).astype(np.float32)


def _dot(a, b):
    return jnp.dot(a, b, preferred_element_type=F32)


def _dot_nt(a, b):
    return lax.dot_general(a, b, (((1,), (1,)), ((), ())), preferred_element_type=F32)


def _split_dot(x, w_bf16):
    hi = x.astype(BF16)
    lo = (x - hi.astype(F32)).astype(BF16)
    return _dot(hi, w_bf16) + _dot(lo, w_bf16)


def _head_rms(x, bd, gain):
    ms = _split_dot(x * x, bd) * (1.0 / HEAD_DIM)
    return x * lax.rsqrt(ms + EPS) * gain


def _silu(x):
    return x * jax.nn.sigmoid(x)


def _mod_kernel(c_ref, w_ref, b_ref, o_ref):
    c = c_ref[...]
    a = _silu(c).astype(BF16)
    o_ref[...] = _dot(a, w_ref[...].astype(BF16)) + b_ref[...]


def _modulation(cvec8, w_mod, b_mod):
    n = 6 * D_MODEL
    tn = 1536
    return pl.pallas_call(
        _mod_kernel,
        grid=(DEPTH, n // tn),
        in_specs=[
            pl.BlockSpec((8, D_MODEL), lambda l, j: (0, 0)),
            pl.BlockSpec((None, D_MODEL, tn), lambda l, j: (l, 0, j)),
            pl.BlockSpec((None, 1, tn), lambda l, j: (l, 0, j)),
        ],
        out_specs=pl.BlockSpec((None, 8, tn), lambda l, j: (l, 0, j)),
        out_shape=jax.ShapeDtypeStruct((DEPTH, 8, n), F32),
        compiler_params=_params(("arbitrary", "arbitrary")),
        name="modulation",
    )(cvec8, w_mod, b_mod.reshape(DEPTH, 1, n))


NZ_TILES = Z_W // TN
NG_TILES = GATE_W // TN


def _proj_kernel(x_ref, m_ref, g_ref, win_ref, wg_ref, bg_ref, z_ref, gate_ref, h_scr):
    j = pl.program_id(1)

    @pl.when(j == 0)
    def _():
        x = x_ref[...]
        ms = jnp.mean(x * x, axis=-1, keepdims=True)
        y = x * lax.rsqrt(ms + EPS) * g_ref[...]
        h = y * (1.0 + m_ref[1:2, :]) + m_ref[0:1, :]
        h_scr[...] = h.astype(BF16)

    @pl.when(j < NZ_TILES)
    def _():
        z_ref[...] = _dot(h_scr[...], win_ref[...])

    @pl.when(j >= NZ_TILES)
    def _():
        g = _dot(h_scr[...], wg_ref[...]) + bg_ref[...]
        gate_ref[...] = jax.nn.sigmoid(g).astype(BF16)


def _projection(x, mods, norm_g, w_in_p, w_gate, b_gate):
    nt = NTOK // TM
    return pl.pallas_call(
        _proj_kernel,
        grid=(nt, NZ_TILES + NG_TILES),
        in_specs=[
            pl.BlockSpec((TM, D_MODEL), lambda i, j: (i, 0)),
            pl.BlockSpec((None, 6, D_MODEL), lambda i, j: (_seg_of_tile(i, TM), 0, 0)),
            pl.BlockSpec((1, D_MODEL), lambda i, j: (0, 0)),
            pl.BlockSpec((D_MODEL, TN), lambda i, j: (0, jnp.minimum(j, NZ_TILES - 1))),
            pl.BlockSpec((D_MODEL, TN), lambda i, j: (0, jnp.maximum(j - NZ_TILES, 0))),
            pl.BlockSpec((1, TN), lambda i, j: (0, jnp.maximum(j - NZ_TILES, 0))),
        ],
        out_specs=[
            pl.BlockSpec((TM, TN), lambda i, j: (i, jnp.minimum(j, NZ_TILES - 1))),
            pl.BlockSpec((TM, TN), lambda i, j: (i, jnp.maximum(j - NZ_TILES, 0))),
        ],
        out_shape=[
            jax.ShapeDtypeStruct((NTOK, Z_W), F32),
            jax.ShapeDtypeStruct((NTOK, GATE_W), BF16),
        ],
        scratch_shapes=[pltpu.VMEM((TM, D_MODEL), BF16)],
        compiler_params=_params(("arbitrary", "arbitrary")),
        name="projection",
    )(x, mods, norm_g, w_in_p, w_gate, b_gate)


def _hgrn_kernel(q_ref, f_ref, v_ref, lb_ref, s0_ref, bd_ref, o_ref, sfin_ref, s_scr, *, rev):
    i = pl.program_id(0)
    blk = (pl.num_programs(0) - 1 - i) if rev else i
    nct = N_CTX // HB
    per_c = SEQ // HB
    per_l = DEC_SEQ // HB
    is_ctx = blk < nct
    pos = jnp.where(is_ctx, blk % per_c, (blk - nct) % per_l)
    last = jnp.where(is_ctx, per_c - 1, per_l - 1)
    first_pos = last if rev else 0
    final_pos = 0 if rev else last

    @pl.when(pos == first_pos)
    def _():
        s_scr[...] = s0_ref[...]

    q = q_ref[...]
    v = v_ref[...]
    lb = lb_ref[...]
    f = lb + (1.0 - lb) * jax.nn.sigmoid(f_ref[...])
    lf = jnp.log(f)
    kk = 1.0 - f

    row = lax.broadcasted_iota(jnp.int32, (HB, HG_W), 0)
    tq = lax.broadcasted_iota(jnp.int32, (HB, HB), 0)
    tk = lax.broadcasted_iota(jnp.int32, (HB, HB), 1)

    def before(x, m):
        return pltpu.roll(x, (HB - m) if rev else m, 0)

    def after(x, m):
        return pltpu.roll(x, m if rev else (HB - m), 0)

    q16 = q.astype(BF16)
    k16 = kk.astype(BF16)
    att = [jnp.where(tq == tk, _dot_nt(_hs(q16, h), _hs(k16, h)), 0.0) for h in range(HG_HEADS)]

    tot = lf
    pin = lf
    sex = jnp.zeros_like(lf)
    m = 1
    while m < HB:
        late = ((row & (2 * m - 1)) < m) if rev else ((row & (2 * m - 1)) >= m)
        qm = jnp.where(late, q * jnp.exp(pin), 0.0).astype(BF16)
        km = jnp.where(late, 0.0, kk * jnp.exp(sex)).astype(BF16)
        shift = (2 * m).bit_length() - 1
        same = (tq >> shift) == (tk >> shift)
        for h in range(HG_HEADS):
            sc = _dot_nt(qm[:, 64 * h:64 * h + 64], km[:, 64 * h:64 * h + 64])
            att[h] = att[h] + (sc if 2 * m == HB else jnp.where(same, sc, 0.0))
        tb = before(tot, m)
        ta = after(tot, m)
        pin = pin + jnp.where(late, tb, 0.0)
        sex = sex + jnp.where(late, 0.0, ta)
        tot = tot + jnp.where(late, tb, ta)
        m *= 2

    lane_head = lax.broadcasted_iota(jnp.int32, (HB, HG_W), 1) // HEAD_DIM
    o = jnp.zeros((HB, HG_W), F32)
    for h in range(HG_HEADS):
        vh = jnp.where(lane_head == h, v, 0.0).astype(BF16)
        o = o + _dot(att[h].astype(BF16), vh)

    s_t = s_scr[...]
    qt = (q * jnp.exp(pin)).astype(BF16)
    o = o + _dot_nt(qt, s_t.astype(BF16))
    o_ref[...] = o

    kt = (kk * jnp.exp(sex)).astype(BF16)
    dec = jnp.exp(tot[0:1, :])
    ds_t = _dot(v.T.astype(BF16), kt)
    s_new = s_t * dec + ds_t * bd_ref[...]
    s_scr[...] = s_new

    @pl.when(pos == final_pos)
    def _():
        sfin_ref[...] = s_new


def _hgrn(z, lbs_l, s0t, bd, rev):
    nb = NTOK // HB
    nct = N_CTX // HB
    nseq = BATCH + DEC_BATCH
    d = 1 if rev else 0

    def blk_of(i):
        return (nb - 1 - i) if rev else i

    def seq_of(i):
        b = blk_of(i)
        return jnp.where(b < nct, b // (SEQ // HB), BATCH + (b - nct) // (DEC_SEQ // HB))

    return pl.pallas_call(
        functools.partial(_hgrn_kernel, rev=rev),
        grid=(nb,),
        in_specs=[
            pl.BlockSpec((HB, HG_W), lambda i: (blk_of(i), COL_AQ // HG_W)),
            pl.BlockSpec((HB, HG_W), lambda i: (blk_of(i), (COL_AFB if rev else COL_AFF) // HG_W)),
            pl.BlockSpec((HB, HG_W), lambda i: (blk_of(i), COL_AI // HG_W)),
            pl.BlockSpec((None, 1, HG_W), lambda i: (d, 0, 0)),
            pl.BlockSpec((None, None, HG_W, HG_W), lambda i: (seq_of(i), d, 0, 0)),
            pl.BlockSpec((HG_W, HG_W), lambda i: (0, 0)),
        ],
        out_specs=[
            pl.BlockSpec((HB, HG_W), lambda i: (blk_of(i), 0)),
            pl.BlockSpec((None, HG_W, HG_W), lambda i: (seq_of(i), 0, 0)),
        ],
        out_shape=[
            jax.ShapeDtypeStruct((NTOK, HG_W), F32),
            jax.ShapeDtypeStruct((nseq, HG_W, HG_W), F32),
        ],
        scratch_shapes=[pltpu.VMEM((HG_W, HG_W), F32)],
        compiler_params=_params(("arbitrary",)),
        name="hgrn_bwd" if rev else "hgrn_fwd",
    )(z, z, z, lbs_l.reshape(2, 1, HG_W), s0t, bd)


def _pool_kernel(u_ref, w_ref, sc_ref, o_ref, *, t_len):
    u = u_ref[...]
    row = lax.broadcasted_iota(jnp.int32, (t_len, POOL_W), 0)
    grp = lax.broadcasted_iota(jnp.int32, (t_len, POOL_W), 1) // HEAD_DIM
    half = jnp.left_shift(1, grp)
    acc = jnp.zeros_like(u)
    for j in range(-8, 8):
        src = row + j
        ok = (j >= -half) & (j < half) & (src >= 0) & (src < t_len)
        shifted = u if j == 0 else pltpu.roll(u, (-j) % t_len, 0)
        acc = acc + jnp.where(ok, shifted, 0.0)
    cnt = (jnp.minimum(row + half, t_len) - jnp.maximum(row - half, 0)).astype(F32)
    y = _dot((acc / cnt - u).astype(BF16), w_ref[...]) * sc_ref[...]
    o_ref[...] = y.astype(BF16)


def _pool(z, row0, nseq, t_len, w_bd, scale):
    return pl.pallas_call(
        functools.partial(_pool_kernel, t_len=t_len),
        grid=(nseq,),
        in_specs=[
            pl.BlockSpec((t_len, POOL_W), lambda b: (row0 // t_len + b, COL_CU // POOL_W)),
            pl.BlockSpec((POOL_W, POOL_W), lambda b: (0, 0)),
            pl.BlockSpec((1, POOL_W), lambda b: (0, 0)),
        ],
        out_specs=pl.BlockSpec((t_len, POOL_W), lambda b: (b, 0)),
        out_shape=jax.ShapeDtypeStruct((nseq * t_len, POOL_W), BF16),
        compiler_params=_params(("arbitrary",)),
        name="pool",
    )(z, w_bd, scale)


def _rope(x, cos, sin):
    w = x.shape[-1]
    lane = lax.broadcasted_iota(jnp.int32, x.shape, 1)
    up = pltpu.roll(x, w - 16, 1)
    dn = pltpu.roll(x, 16, 1)
    return x * cos + jnp.where((lane & 31) < 16, up, dn) * sin


def _prep_kernel(*refs, rope):
    if rope:
        (bq, bk, bv, dq, dk, dv, gq, gk, gdq, gdk, bd, cq, sq, ck, sk,
         oq, ok_, ov, odq, odk, odv) = refs
    else:
        (bq, bk, bv, dq, dk, dv, gq, gk, gdq, gdk, bd,
         oq, ok_, ov, odq, odk, odv, ok32, odk32) = refs
    bdm = bd[...]
    q = _head_rms(bq[...], bdm, gq[...])
    k = _head_rms(bk[...], bdm[:WK_B, :WK_B], gk[...])
    qd = _head_rms(dq[...], bdm[:WD, :WD], gdq[...])
    kd = _head_rms(dk[...], bdm[:WD, :WD], gdk[...])
    if rope:
        q = _rope(q, cq[...], sq[...])
        k = _rope(k, ck[...], sk[...])
    else:
        ok32[...] = k
        odk32[...] = kd
    oq[...] = q.astype(BF16)
    ok_[...] = k.astype(BF16)
    ov[...] = bv[...].astype(BF16)
    odq[...] = qd.astype(BF16)
    odk[...] = kd.astype(BF16)
    odv[...] = dv[...].astype(BF16)


def _prep(z, row0, nrows, gains, bd, rope_tabs):
    tm = 512
    nt = nrows // tm
    r0 = row0 // tm
    rope = rope_tabs is not None
    col = lambda c, w: (lambda i: (r0 + i, c // w))
    in_specs = [
        pl.BlockSpec((tm, WQ_B), col(COL_BQ, WQ_B)),
        pl.BlockSpec((tm, WK_B), col(COL_BK, WK_B)),
        pl.BlockSpec((tm, WK_B), col(COL_BV, WK_B)),
        pl.BlockSpec((tm, WD), col(COL_DQ, WD)),
        pl.BlockSpec((tm, WD), col(COL_DK, WD)),
        pl.BlockSpec((tm, WD), col(COL_DV, WD)),
        pl.BlockSpec((1, WQ_B), lambda i: (0, 0)),
        pl.BlockSpec((1, WK_B), lambda i: (0, 0)),
        pl.BlockSpec((1, WD), lambda i: (0, 0)),
        pl.BlockSpec((1, WD), lambda i: (0, 0)),
        pl.BlockSpec((WQ_B, WQ_B), lambda i: (0, 0)),
    ]
    args = [z, z, z, z, z, z, *gains, bd]
    per = DEC_SEQ // tm
    if rope:
        in_specs += [
            pl.BlockSpec((tm, WQ_B), lambda i: (i % per, 0)),
            pl.BlockSpec((tm, WQ_B), lambda i: (i % per, 0)),
            pl.BlockSpec((tm, WK_B), lambda i: (i % per, 0)),
            pl.BlockSpec((tm, WK_B), lambda i: (i % per, 0)),
        ]
        args += list(rope_tabs)
    widths = [WQ_B, WK_B, WK_B, WD, WD, WD]
    out_specs = [pl.BlockSpec((tm, w), lambda i: (i, 0)) for w in widths]
    out_shape = [jax.ShapeDtypeStruct((nrows, w), BF16) for w in widths]
    if not rope:
        out_specs += [pl.BlockSpec((tm, WK_B), lambda i: (i, 0)), pl.BlockSpec((tm, WD), lambda i: (i, 0))]
        out_shape += [jax.ShapeDtypeStruct((nrows, WK_B), F32), jax.ShapeDtypeStruct((nrows, WD), F32)]
    return pl.pallas_call(
        functools.partial(_prep_kernel, rope=rope),
        grid=(nt,),
        in_specs=in_specs,
        out_specs=out_specs,
        out_shape=out_shape,
        compiler_params=_params(("arbitrary",)),
        name="prep_lat" if rope else "prep_ctx",
    )(*args)


def _softmax_pv(scores, values, sink):
    m = scores[0].max(axis=-1, keepdims=True)
    for s in scores[1:]:
        m = jnp.maximum(m, s.max(axis=-1, keepdims=True))
    if sink is not None:
        m = jnp.maximum(m, sink)
    den = jnp.zeros_like(m) if sink is None else jnp.exp(sink - m)
    acc = None
    for s, v in zip(scores, values):
        p = jnp.exp(s - m)
        den = den + p.sum(axis=-1, keepdims=True)
        pv = _dot(p.astype(BF16), v)
        acc = pv if acc is None else acc + pv
    return acc / den


def _hs(x, h):
    return x[:, HEAD_DIM * h:HEAD_DIM * (h + 1)]


def _ctx_attn_kernel(sink_ref, q_ref, k_ref, v_ref, qd_ref, kd_ref, vd_ref, ob_ref, od_ref):
    q, k, v = q_ref[...], k_ref[...], v_ref[...]
    g = WIN_HEADS // WIN_KV
    for h in range(WIN_HEADS):
        s = _dot_nt(_hs(q, h), _hs(k, h // g)) * SCALE
        o = _softmax_pv([s], [_hs(v, h // g)], sink_ref[0, h])
        ob_ref[:, HEAD_DIM * h:HEAD_DIM * (h + 1)] = o.astype(BF16)
    qd, kd, vd = qd_ref[...], kd_ref[...], vd_ref[...]
    for h in range(NA_HEADS):
        s = _dot_nt(_hs(qd, h), _hs(kd, h)) * SCALE
        o = _softmax_pv([s], [_hs(vd, h)], None)
        od_ref[:, HEAD_DIM * h:HEAD_DIM * (h + 1)] = o.astype(BF16)


def _ctx_attn(sink, q, k, v, qd, kd, vd):
    blk = lambda w: pl.BlockSpec((SEQ, w), lambda b: (b, 0))
    return pl.pallas_call(
        _ctx_attn_kernel,
        grid=(BATCH,),
        in_specs=[pl.BlockSpec(memory_space=pltpu.SMEM),
                  blk(WQ_B), blk(WK_B), blk(WK_B), blk(WD), blk(WD), blk(WD)],
        out_specs=[blk(WQ_B), blk(WD)],
        out_shape=[jax.ShapeDtypeStruct((N_CTX, WQ_B), BF16), jax.ShapeDtypeStruct((N_CTX, WD), BF16)],
        compiler_params=_params(("arbitrary",)),
        name="ctx_attn",
    )(sink, q, k, v, qd, kd, vd)


WIN_SPAN = 3 * WIN


def _win_attn_kernel(sink_ref, q_ref, k_ref, v_ref, kc_ref, vc_ref, o_ref):
    qi = pl.program_id(1)
    start = pl.multiple_of(jnp.clip(qi * WIN - WIN, 0, DEC_SEQ - WIN_SPAN), WIN)
    q = q_ref[...]
    kw = k_ref[pl.ds(start, WIN_SPAN), :]
    vw = v_ref[pl.ds(start, WIN_SPAN), :]
    kc, vc = kc_ref[...], vc_ref[...]
    qpos = qi * WIN + lax.broadcasted_iota(jnp.int32, (WIN, WIN_SPAN), 0)
    kpos = start + lax.broadcasted_iota(jnp.int32, (WIN, WIN_SPAN), 1)
    valid = jnp.abs(qpos - kpos) <= WIN
    g = WIN_HEADS // WIN_KV
    for h in range(WIN_HEADS):
        qh = _hs(q, h)
        s_loc = jnp.where(valid, _dot_nt(qh, _hs(kw, h // g)) * SCALE, NEG)
        s_ctx = _dot_nt(qh, _hs(kc, h // g)) * SCALE
        o = _softmax_pv([s_loc, s_ctx], [_hs(vw, h // g), _hs(vc, h // g)], sink_ref[0, h])
        o_ref[:, HEAD_DIM * h:HEAD_DIM * (h + 1)] = o.astype(BF16)


def _win_attn(sink, q, k, v, kc, vc):
    nq = DEC_SEQ // WIN
    return pl.pallas_call(
        _win_attn_kernel,
        grid=(DEC_BATCH, nq),
        in_specs=[
            pl.BlockSpec(memory_space=pltpu.SMEM),
            pl.BlockSpec((WIN, WQ_B), lambda b, i: (b * nq + i, 0)),
            pl.BlockSpec((None, DEC_SEQ, WK_B), lambda b, i: (b, 0, 0)),
            pl.BlockSpec((None, DEC_SEQ, WK_B), lambda b, i: (b, 0, 0)),
            pl.BlockSpec((None, PAST_LEN, WK_B), lambda b, i: (b, 0, 0)),
            pl.BlockSpec((None, PAST_LEN, WK_B), lambda b, i: (b, 0, 0)),
        ],
        out_specs=pl.BlockSpec((WIN, WQ_B), lambda b, i: (b * nq + i, 0)),
        out_shape=jax.ShapeDtypeStruct((N_LAT, WQ_B), BF16),
        compiler_params=_params(("arbitrary", "arbitrary")),
        name="win_attn",
    )(sink, q, k.reshape(DEC_BATCH, DEC_SEQ, WK_B), v.reshape(DEC_BATCH, DEC_SEQ, WK_B), kc, vc)


NA_ROWS = DEC_SEQ // GRID_W
NA_KEYS = NA_KH * GRID_W


def _na_row_start(r):
    return jnp.clip(r - NA_KH // 2, 0, NA_ROWS - NA_KH)


def _natten_kernel(q_ref, k_ref, v_ref, kc_ref, vc_ref, bias_ref, o_ref):
    r = pl.program_id(1)
    start = pl.multiple_of(_na_row_start(r) * GRID_W, GRID_W)
    q = q_ref[...]
    kw = k_ref[pl.ds(start, NA_KEYS), :]
    vw = v_ref[pl.ds(start, NA_KEYS), :]
    kc, vc = kc_ref[...], vc_ref[...]
    for h in range(NA_HEADS):
        qh = _hs(q, h)
        s_loc = _dot_nt(qh, _hs(kw, h)) * SCALE + bias_ref[h]
        s_ctx = _dot_nt(qh, _hs(kc, h)) * SCALE
        o = _softmax_pv([s_loc, s_ctx], [_hs(vw, h), _hs(vc, h)], None)
        o_ref[:, HEAD_DIM * h:HEAD_DIM * (h + 1)] = o.astype(BF16)


def _natten(q, k, v, kc, vc, bias):
    def delta(r):
        return _na_row_start(r) - r + NA_KH - 1

    return pl.pallas_call(
        _natten_kernel,
        grid=(DEC_BATCH, NA_ROWS),
        in_specs=[
            pl.BlockSpec((GRID_W, WD), lambda b, r: (b * NA_ROWS + r, 0)),
            pl.BlockSpec((None, DEC_SEQ, WD), lambda b, r: (b, 0, 0)),
            pl.BlockSpec((None, DEC_SEQ, WD), lambda b, r: (b, 0, 0)),
            pl.BlockSpec((None, PAST_LEN, WD), lambda b, r: (b, 0, 0)),
            pl.BlockSpec((None, PAST_LEN, WD), lambda b, r: (b, 0, 0)),
            pl.BlockSpec((None, NA_HEADS, GRID_W, NA_KEYS), lambda b, r: (delta(r), 0, 0, 0)),
        ],
        out_specs=pl.BlockSpec((GRID_W, WD), lambda b, r: (b * NA_ROWS + r, 0)),
        out_shape=jax.ShapeDtypeStruct((N_LAT, WD), BF16),
        compiler_params=_params(("arbitrary", "arbitrary")),
        name="natten",
    )(q, k.reshape(DEC_BATCH, DEC_SEQ, WD), v.reshape(DEC_BATCH, DEC_SEQ, WD), kc, vc, bias)


def _natten_bias(rpb):
    c = np.arange(GRID_W)
    cstart = np.clip(c - NA_KW // 2, 0, GRID_W - NA_KW)
    inwin = (c[None, :] >= cstart[:, None]) & (c[None, :] < cstart[:, None] + NA_KW)
    dc = np.clip(c[None, :] - c[:, None] + NA_KW - 1, 0, 2 * NA_KW - 2)
    toep = rpb.astype(F32)[:, :, dc]
    toep = jnp.where(inwin[None, None], toep, NEG)
    out = []
    for d0 in range(NA_KH):
        out.append(jnp.concatenate([toep[:, d0 + i] for i in range(NA_KH)], axis=-1))
    return jnp.stack(out, axis=0)


def _merge_kernel(x_ref, m_ref, of_ref, ob_ref, ag_ref, on_ref, bd_ref,
                  ybc_ref, ybl_ref, ycc_ref, ycl_ref, ydc_ref, ydl_ref,
                  gate_ref, wa_ref, wb_ref, wc_ref, wd_ref, wo_ref, o_ref):
    i = pl.program_id(0)
    is_ctx = i < N_CTX // TM
    o = of_ref[...] + ob_ref[...]
    ya = _head_rms(o, bd_ref[...], on_ref[...]) * _silu(ag_ref[...])
    yb = jnp.where(is_ctx, ybc_ref[...], ybl_ref[...])
    yc = jnp.where(is_ctx, ycc_ref[...], ycl_ref[...])
    yd = jnp.where(is_ctx, ydc_ref[...], ydl_ref[...])
    d = D_MODEL
    merged = (gate_ref[:, 0:d].astype(F32) * _dot(ya.astype(BF16), wa_ref[...])
              + gate_ref[:, d:2 * d].astype(F32) * _dot(yb, wb_ref[...])
              + gate_ref[:, 2 * d:3 * d].astype(F32) * _dot(yc, wc_ref[...])
              + gate_ref[:, 3 * d:4 * d].astype(F32) * _dot(yd, wd_ref[...]))
    o_ref[...] = x_ref[...] + m_ref[2:3, :] * _dot(merged.astype(BF16), wo_ref[...])


def _merge(x, mods, o_f, o_b, z, onorm, bd, yb_c, yb_l, yc_c, yc_l, yd_c, yd_l, gates, wa, wb, wc, wd, wo):
    nt = NTOK // TM
    nct = N_CTX // TM
    cspec = lambda w: pl.BlockSpec((TM, w), lambda i: (jnp.minimum(i, nct - 1), 0))
    lspec = lambda w: pl.BlockSpec((TM, w), lambda i: (jnp.maximum(i - nct, 0), 0))
    full = lambda a: pl.BlockSpec(a.shape, lambda i: (0, 0))
    return pl.pallas_call(
        _merge_kernel,
        grid=(nt,),
        in_specs=[
            pl.BlockSpec((TM, D_MODEL), lambda i: (i, 0)),
            pl.BlockSpec((None, 6, D_MODEL), lambda i: (_seg_of_tile(i, TM), 0, 0)),
            pl.BlockSpec((TM, HG_W), lambda i: (i, 0)),
            pl.BlockSpec((TM, HG_W), lambda i: (i, 0)),
            pl.BlockSpec((TM, HG_W), lambda i: (i, COL_AG // HG_W)),
            full(onorm), full(bd),
            cspec(WQ_B), lspec(WQ_B), cspec(POOL_W), lspec(POOL_W), cspec(WD), lspec(WD),
            pl.BlockSpec((TM, GATE_W), lambda i: (i, 0)),
            full(wa), full(wb), full(wc), full(wd), full(wo),
        ],
        out_specs=pl.BlockSpec((TM, D_MODEL), lambda i: (i, 0)),
        out_shape=jax.ShapeDtypeStruct((NTOK, D_MODEL), F32),
        compiler_params=_params(("arbitrary",)),
        name="merge",
    )(x, mods, o_f, o_b, z, onorm, bd, yb_c, yb_l, yc_c, yc_l, yd_c, yd_l, gates, wa, wb, wc, wd, wo)


E_PAD = 128
N_SLOTS = N_EXPERTS + 1


def _moe_kernel(x_ref, m_ref, g_ref, wr_ref, br_ref, wg_ref, wu_ref, wd_ref, o_ref,
                h_scr, gate_scr, acc_scr):
    e = pl.program_id(1)
    lane = lax.broadcasted_iota(jnp.int32, (TM, E_PAD), 1)

    @pl.when(e == 0)
    def _():
        x = x_ref[...]
        ms = jnp.mean(x * x, axis=-1, keepdims=True)
        y = x * lax.rsqrt(ms + EPS) * g_ref[...]
        h = y * (1.0 + m_ref[4:5, :]) + m_ref[3:4, :]
        h_scr[...] = h.astype(BF16)
        h_hi = h.astype(BF16)
        h_lo = (h - h_hi.astype(F32)).astype(BF16)
        w = wr_ref[...]
        w_hi = w.astype(BF16)
        w_lo = (w - w_hi.astype(F32)).astype(BF16)
        logits = _dot(h_hi, w_hi) + _dot(h_hi, w_lo) + _dot(h_lo, w_hi)
        scores = jax.nn.sigmoid(logits)
        sel = jnp.where(lane < N_EXPERTS, scores + br_ref[...], -jnp.inf)
        picked = jnp.zeros((TM, E_PAD), F32)
        for _ in range(TOP_K):
            mx = sel.max(axis=-1, keepdims=True)
            idx = jnp.min(jnp.where(sel == mx, lane, E_PAD), axis=-1, keepdims=True)
            hit = lane == idx
            picked = jnp.where(hit, scores, picked)
            sel = jnp.where(hit, -jnp.inf, sel)
        wts = ROUTED_SCALE * picked / picked.sum(axis=-1, keepdims=True)
        gate_scr[...] = jnp.where(lane == N_EXPERTS, 1.0, wts)
        acc_scr[...] = jnp.zeros_like(acc_scr)

    h = h_scr[...]
    a = _silu(_dot(h, wg_ref[...])) * _dot(h, wu_ref[...])
    ge = jnp.sum(jnp.where(lane == e, gate_scr[...], 0.0), axis=-1, keepdims=True)
    acc_scr[...] += _dot((a * ge).astype(BF16), wd_ref[...])

    @pl.when(e == N_SLOTS - 1)
    def _():
        o_ref[...] = x_ref[...] + m_ref[5:6, :] * acc_scr[...]


def _moe(x, mods, norm_g, w_router, b_router, wg, wu, wd):
    nt = NTOK // TM
    return pl.pallas_call(
        _moe_kernel,
        grid=(nt, N_SLOTS),
        in_specs=[
            pl.BlockSpec((TM, D_MODEL), lambda i, e: (i, 0)),
            pl.BlockSpec((None, 6, D_MODEL), lambda i, e: (_seg_of_tile(i, TM), 0, 0)),
            pl.BlockSpec((1, D_MODEL), lambda i, e: (0, 0)),
            pl.BlockSpec((D_MODEL, E_PAD), lambda i, e: (0, 0)),
            pl.BlockSpec((1, E_PAD), lambda i, e: (0, 0)),
            pl.BlockSpec((None, D_MODEL, D_EXPERT), lambda i, e: (e, 0, 0)),
            pl.BlockSpec((None, D_MODEL, D_EXPERT), lambda i, e: (e, 0, 0)),
            pl.BlockSpec((None, D_EXPERT, D_MODEL), lambda i, e: (e, 0, 0)),
        ],
        out_specs=pl.BlockSpec((TM, D_MODEL), lambda i, e: (i, 0)),
        out_shape=jax.ShapeDtypeStruct((NTOK, D_MODEL), F32),
        scratch_shapes=[pltpu.VMEM((TM, D_MODEL), BF16), pltpu.VMEM((TM, E_PAD), F32),
                        pltpu.VMEM((TM, D_MODEL), F32)],
        compiler_params=_params(("arbitrary", "arbitrary")),
        name="moe",
    )(x, mods, norm_g, w_router, b_router, wg, wu, wd)


def _rope_tables(width):
    t = np.arange(DEC_SEQ)
    quarter = HEAD_DIM // 4
    inv = (ROPE_BASE ** (-np.arange(quarter) / quarter)).astype(np.float32)
    ang_r = (t // GRID_W).astype(np.float32)[:, None] * inv[None]
    ang_c = (t % GRID_W).astype(np.float32)[:, None] * inv[None]
    cos = np.concatenate([np.cos(ang_r), np.cos(ang_r), np.cos(ang_c), np.cos(ang_c)], axis=1)
    sin = np.concatenate([-np.sin(ang_r), np.sin(ang_r), -np.sin(ang_c), np.sin(ang_c)], axis=1)
    reps = width // HEAD_DIM
    return (jnp.asarray(np.tile(cos, (1, reps)), F32), jnp.asarray(np.tile(sin, (1, reps)), F32))


def _permute_w_in(w):
    a = w[:, 0:1280]
    bq, bk, bv = w[:, 1280:1664], w[:, 1664:1792], w[:, 1792:1920]
    cu = w[:, 1920:2176]
    d = w[:, 2176:2944]
    pad = jnp.zeros((w.shape[0], Z_W - 2944), w.dtype)
    return jnp.concatenate([a, cu, d, bq, bk, bv, pad], axis=1)


def _block_diag(blocks):
    g = blocks.shape[0]
    eye = jnp.eye(g, dtype=blocks.dtype)
    return jnp.einsum('gh,gij->gihj', eye, blocks).reshape(g * HEAD_DIM, g * HEAD_DIM)


def kernel(x_prompt, x_sample, cache_win_k, cache_win_v, cache_na_k, cache_na_v, state_hgrn, c, c_ctx, w_mod, b_mod, norm1_g, norm2_g, w_in, w_mgate, b_mgate, hg_lb, hg_onorm, win_qn, win_kn, win_sink, pool_w, pool_scale, na_qn, na_kn, na_rpb, w_branch, w_out, w_router, b_router, w_eg, w_eu, w_ed, w_sg, w_su, w_sd):
    lbp = jax.nn.softmax(hg_lb.astype(F32), axis=0)
    lbs = jnp.cumsum(lbp, axis=0) - lbp[0:1]

    cvec8 = jnp.concatenate([c_ctx[None], c, jnp.zeros((3, D_MODEL), F32)], axis=0)
    mods_all = _modulation(cvec8, w_mod, b_mod).reshape(DEPTH, 8, 6, D_MODEL)

    bd384 = jnp.asarray(_bd_ones(WQ_B), BF16)
    bd256 = bd384[:HG_W, :HG_W]
    bd256_f32 = jnp.asarray(_bd_ones(HG_W), F32)
    rope_q = _rope_tables(WQ_B)
    rope_k = _rope_tables(WK_B)
    tile = lambda g, reps: jnp.tile(g, reps)[None, :]

    x = jnp.concatenate([x_prompt.reshape(N_CTX, D_MODEL), x_sample.reshape(N_LAT, D_MODEL)], axis=0)
    new_k, new_v, new_kd, new_vd, new_s = [], [], [], [], []
    for l in range(DEPTH):
        mods = mods_all[l]
        z, gates = _projection(x, mods, norm1_g[l][None], _permute_w_in(w_in[l]).astype(BF16),
                               w_mgate[l].astype(BF16), b_mgate[l][None])

        st_lat = _block_diag_states(state_hgrn[:, l])
        s0t = jnp.concatenate([jnp.zeros((BATCH, 2, HG_W, HG_W), F32), st_lat], axis=0)
        o_f, sfin_f = _hgrn(z, lbs[l], s0t, bd256_f32, rev=False)
        o_b, sfin_b = _hgrn(z, lbs[l], s0t, bd256_f32, rev=True)
        new_s.append(jnp.stack([_unblock_states(sfin_f[:BATCH]), _unblock_states(sfin_b[:BATCH])], axis=1))

        w_pool = _block_diag(pool_w[l]).astype(BF16)
        yc_c = _pool(z, 0, BATCH, SEQ, w_pool, pool_scale[l][None])
        yc_l = _pool(z, N_CTX, DEC_BATCH, DEC_SEQ, w_pool, pool_scale[l][None])

        gains = (tile(win_qn[l], WIN_HEADS), tile(win_kn[l], WIN_KV), tile(na_qn[l], NA_HEADS), tile(na_kn[l], NA_HEADS))
        qb_c, kb_c, vb_c, qd_c, kd_c, vd_c, kb32, kd32 = _prep(z, 0, N_CTX, gains, bd384, None)
        qb_l, kb_l, vb_l, qd_l, kd_l, vd_l = _prep(z, N_CTX, N_LAT, gains, bd384, rope_q + rope_k)
        sink = win_sink[l][None]
        yb_c, yd_c = _ctx_attn(sink, qb_c, kb_c, vb_c, qd_c, kd_c, vd_c)
        kc = cache_win_k[:, l].reshape(DEC_BATCH, PAST_LEN, WK_B).astype(BF16)
        vc = cache_win_v[:, l].reshape(DEC_BATCH, PAST_LEN, WK_B).astype(BF16)
        yb_l = _win_attn(sink, qb_l, kb_l, vb_l, kc, vc)
        kcd = cache_na_k[:, l].reshape(DEC_BATCH, PAST_LEN, WD).astype(BF16)
        vcd = cache_na_v[:, l].reshape(DEC_BATCH, PAST_LEN, WD).astype(BF16)
        yd_l = _natten(qd_l, kd_l, vd_l, kcd, vcd, _natten_bias(na_rpb[l]))

        new_k.append(kb32.reshape(BATCH, SEQ, WIN_KV, HEAD_DIM))
        new_v.append(z[:N_CTX, COL_BV:COL_BV + WK_B].reshape(BATCH, SEQ, WIN_KV, HEAD_DIM))
        new_kd.append(kd32.reshape(BATCH, SEQ, NA_HEADS, HEAD_DIM))
        new_vd.append(z[:N_CTX, COL_DV:COL_DV + WD].reshape(BATCH, SEQ, NA_HEADS, HEAD_DIM))

        wbr = w_branch[l].astype(BF16)
        x = _merge(x, mods, o_f, o_b, z, tile(hg_onorm[l], HG_HEADS), bd256,
                   yb_c, yb_l, yc_c, yc_l, yd_c, yd_l, gates,
                   wbr[0:256], wbr[256:640], wbr[640:896], wbr[896:1152], w_out[l].astype(BF16))

        wr = jnp.pad(w_router[l], ((0, 0), (0, E_PAD - N_EXPERTS)))
        br = jnp.pad(b_router[l], (0, E_PAD - N_EXPERTS))[None]
        wg = jnp.concatenate([w_eg[l], w_sg[l][None]], axis=0).astype(BF16)
        wu = jnp.concatenate([w_eu[l], w_su[l][None]], axis=0).astype(BF16)
        wd = jnp.concatenate([w_ed[l], w_sd[l][None]], axis=0).astype(BF16)
        x = _moe(x, mods, norm2_g[l][None], wr, br, wg, wu, wd)

    y_p = x[:N_CTX].reshape(BATCH, SEQ, D_MODEL)
    y_s = x[N_CTX:].reshape(DEC_BATCH, DEC_SEQ, D_MODEL)
    return (y_p, y_s, jnp.stack(new_k, axis=1), jnp.stack(new_v, axis=1), jnp.stack(new_kd, axis=1),
            jnp.stack(new_vd, axis=1), jnp.stack(new_s, axis=1))


def _block_diag_states(s):
    b = s.shape[0]
    st = jnp.swapaxes(s.astype(F32), -1, -2)
    eye = jnp.eye(HG_HEADS, dtype=F32)
    return jnp.einsum('gh,bdgvk->bdgvhk', eye, st).reshape(b, 2, HG_W, HG_W)


def _unblock_states(st):
    b = st.shape[0]
    s5 = st.reshape(b, HG_HEADS, HEAD_DIM, HG_HEADS, HEAD_DIM)
    diag = jnp.stack([s5[:, h, :, h, :] for h in range(HG_HEADS)], axis=1)
    return jnp.swapaxes(diag, -1, -2)
```

```python
import functools

import numpy as np
import jax
import jax.numpy as jnp
from jax import lax
from jax.experimental import pallas as pl
from jax.experimental.pallas import tpu as pltpu

F32 = jnp.float32
BF16 = jnp.bfloat16

D_MODEL = 1024
BATCH = 16
SEQ = 256
DEPTH = 2
DEC_BATCH = 4
DEC_SEQ = 2048
PAST_LEN = 256
NEG = -1e30
GRID_W = 64
HEAD_DIM = 64
SCALE = HEAD_DIM ** -0.5
ROPE_BASE = 10000.0
EPS = 1e-6
HG_HEADS = 4
HG_W = 256
WIN_HEADS = 6
WIN_KV = 2
WIN = 128
POOL_SIZES = (2, 4, 8, 16)
POOL_W = 256
NA_HEADS = 4
NA_KH = 8
NA_KW = 16
N_EXPERTS = 32
TOP_K = 4
D_EXPERT = 256
ROUTED_SCALE = 2.5

N_CTX = BATCH * SEQ
N_LAT = DEC_BATCH * DEC_SEQ
NTOK = N_CTX + N_LAT
WQ_B = WIN_HEADS * HEAD_DIM
WK_B = WIN_KV * HEAD_DIM
WD = NA_HEADS * HEAD_DIM

Z_W = 3072
COL_AQ, COL_AFF, COL_AFB, COL_AI, COL_AG, COL_CU, COL_DQ, COL_DK, COL_DV = (
    0, 256, 512, 768, 1024, 1280, 1536, 1792, 2048)
COL_BQ, COL_BK, COL_BV = 2304, 2688, 2816
GATE_W = 4 * D_MODEL

TM = 512
TMP = 2048
TN = 512
HB = 128
VMEM_LIMIT = 56 * 1024 * 1024


def _params(sem, vmem=VMEM_LIMIT):
    return pltpu.CompilerParams(dimension_semantics=sem, vmem_limit_bytes=vmem)


def _seg_of_tile(i, tile):
    nct = N_CTX // tile
    per = DEC_SEQ // tile
    return jnp.where(i < nct, 0, 1 + (i - nct) // per)


def _bd_ones(w):
    idx = np.arange(w) // HEAD_DIM
    return (idx[:, None] == idx[None, :]).astype(np.float32)


def _dot(a, b):
    return jnp.dot(a, b, preferred_element_type=F32)


def _dot_nt(a, b):
    return lax.dot_general(a, b, (((1,), (1,)), ((), ())), preferred_element_type=F32)


def _split_dot(x, w_bf16):
    hi = x.astype(BF16)
    lo = (x - hi.astype(F32)).astype(BF16)
    return _dot(hi, w_bf16) + _dot(lo, w_bf16)


def _head_rms(x, bd, gain):
    ms = _split_dot(x * x, bd) * (1.0 / HEAD_DIM)
    return x * lax.rsqrt(ms + EPS) * gain


def _silu(x):
    return x * jax.nn.sigmoid(x)


def _mod_kernel(c_ref, w_ref, b_ref, o_ref):
    c = c_ref[...]
    a = _silu(c).astype(BF16)
    o_ref[...] = _dot(a, w_ref[...].astype(BF16)) + b_ref[...]


def _modulation(cvec8, w_mod, b_mod):
    n = 6 * D_MODEL
    tn = 1536
    return pl.pallas_call(
        _mod_kernel,
        grid=(DEPTH, n // tn),
        in_specs=[
            pl.BlockSpec((8, D_MODEL), lambda l, j: (0, 0)),
            pl.BlockSpec((None, D_MODEL, tn), lambda l, j: (l, 0, j)),
            pl.BlockSpec((None, 1, tn), lambda l, j: (l, 0, j)),
        ],
        out_specs=pl.BlockSpec((None, 8, tn), lambda l, j: (l, 0, j)),
        out_shape=jax.ShapeDtypeStruct((DEPTH, 8, n), F32),
        compiler_params=_params(("arbitrary", "arbitrary")),
        name="modulation",
    )(cvec8, w_mod, b_mod.reshape(DEPTH, 1, n))


NZ_TILES = Z_W // TN
NG_TILES = GATE_W // TN


def _proj_kernel(x_ref, m_ref, g_ref, win_ref, wg_ref, bg_ref, z_ref, gate_ref, h_scr):
    j = pl.program_id(1)

    @pl.when(j == 0)
    def _():
        x = x_ref[...]
        ms = jnp.mean(x * x, axis=-1, keepdims=True)
        y = x * lax.rsqrt(ms + EPS) * g_ref[...]
        h = y * (1.0 + m_ref[1:2, :]) + m_ref[0:1, :]
        h_scr[...] = h.astype(BF16)

    @pl.when(j < NZ_TILES)
    def _():
        z_ref[...] = _dot(h_scr[...], win_ref[...])

    @pl.when(j >= NZ_TILES)
    def _():
        g = _dot(h_scr[...], wg_ref[...]) + bg_ref[...]
        gate_ref[...] = jax.nn.sigmoid(g).astype(BF16)


def _projection(x, mods, norm_g, w_in_p, w_gate, b_gate):
    nt = NTOK // TMP
    return pl.pallas_call(
        _proj_kernel,
        grid=(nt, NZ_TILES + NG_TILES),
        in_specs=[
            pl.BlockSpec((TMP, D_MODEL), lambda i, j: (i, 0)),
            pl.BlockSpec((None, 6, D_MODEL), lambda i, j: (_seg_of_tile(i, TMP), 0, 0)),
            pl.BlockSpec((1, D_MODEL), lambda i, j: (0, 0)),
            pl.BlockSpec((D_MODEL, TN), lambda i, j: (0, jnp.minimum(j, NZ_TILES - 1))),
            pl.BlockSpec((D_MODEL, TN), lambda i, j: (0, jnp.maximum(j - NZ_TILES, 0))),
            pl.BlockSpec((1, TN), lambda i, j: (0, jnp.maximum(j - NZ_TILES, 0))),
        ],
        out_specs=[
            pl.BlockSpec((TMP, TN), lambda i, j: (i, jnp.minimum(j, NZ_TILES - 1))),
            pl.BlockSpec((TMP, TN), lambda i, j: (i, jnp.maximum(j - NZ_TILES, 0))),
        ],
        out_shape=[
            jax.ShapeDtypeStruct((NTOK, Z_W), F32),
            jax.ShapeDtypeStruct((NTOK, GATE_W), BF16),
        ],
        scratch_shapes=[pltpu.VMEM((TMP, D_MODEL), BF16)],
        compiler_params=_params(("arbitrary", "arbitrary")),
        name="projection",
    )(x, mods, norm_g, w_in_p, w_gate, b_gate)


def _hgrn_kernel(q_ref, f_ref, v_ref, lb_ref, s0_ref, bd_ref, o_ref, sfin_ref, s_scr, *, rev):
    i = pl.program_id(0)
    blk = (pl.num_programs(0) - 1 - i) if rev else i
    nct = N_CTX // HB
    per_c = SEQ // HB
    per_l = DEC_SEQ // HB
    is_ctx = blk < nct
    pos = jnp.where(is_ctx, blk % per_c, (blk - nct) % per_l)
    last = jnp.where(is_ctx, per_c - 1, per_l - 1)
    first_pos = last if rev else 0
    final_pos = 0 if rev else last

    @pl.when(pos == first_pos)
    def _():
        s_scr[...] = s0_ref[...]

    q = q_ref[...]
    v = v_ref[...]
    lb = lb_ref[...]
    f = lb + (1.0 - lb) * jax.nn.sigmoid(f_ref[...])
    lf = jnp.log(f)
    kk = 1.0 - f

    row = lax.broadcasted_iota(jnp.int32, (HB, HG_W), 0)
    tq = lax.broadcasted_iota(jnp.int32, (HB, HB), 0)
    tk = lax.broadcasted_iota(jnp.int32, (HB, HB), 1)

    def before(x, m):
        return pltpu.roll(x, (HB - m) if rev else m, 0)

    def after(x, m):
        return pltpu.roll(x, m if rev else (HB - m), 0)

    q16 = q.astype(BF16)
    k16 = kk.astype(BF16)
    att = [jnp.where(tq == tk, _dot_nt(_hs(q16, h), _hs(k16, h)), 0.0) for h in range(HG_HEADS)]

    tot = lf
    pin = lf
    sex = jnp.zeros_like(lf)
    m = 1
    while m < HB:
        late = ((row & (2 * m - 1)) < m) if rev else ((row & (2 * m - 1)) >= m)
        qm = jnp.where(late, q * jnp.exp(pin), 0.0).astype(BF16)
        km = jnp.where(late, 0.0, kk * jnp.exp(sex)).astype(BF16)
        shift = (2 * m).bit_length() - 1
        same = (tq >> shift) == (tk >> shift)
        for h in range(HG_HEADS):
            sc = _dot_nt(qm[:, 64 * h:64 * h + 64], km[:, 64 * h:64 * h + 64])
            att[h] = att[h] + (sc if 2 * m == HB else jnp.where(same, sc, 0.0))
        tb = before(tot, m)
        ta = after(tot, m)
        pin = pin + jnp.where(late, tb, 0.0)
        sex = sex + jnp.where(late, 0.0, ta)
        tot = tot + jnp.where(late, tb, ta)
        m *= 2

    lane_head = lax.broadcasted_iota(jnp.int32, (HB, HG_W), 1) // HEAD_DIM
    o = jnp.zeros((HB, HG_W), F32)
    for h in range(HG_HEADS):
        vh = jnp.where(lane_head == h, v, 0.0).astype(BF16)
        o = o + _dot(att[h].astype(BF16), vh)

    s_t = s_scr[...]
    qt = (q * jnp.exp(pin)).astype(BF16)
    o = o + _dot_nt(qt, s_t.astype(BF16))
    o_ref[...] = o

    kt = (kk * jnp.exp(sex)).astype(BF16)
    dec = jnp.exp(tot[0:1, :])
    ds_t = _dot(v.T.astype(BF16), kt)
    s_new = s_t * dec + ds_t * bd_ref[...]
    s_scr[...] = s_new

    @pl.when(pos == final_pos)
    def _():
        sfin_ref[...] = s_new


def _hgrn(z, lbs_l, s0t, bd, rev):
    nb = NTOK // HB
    nct = N_CTX // HB
    nseq = BATCH + DEC_BATCH
    d = 1 if rev else 0

    def blk_of(i):
        return (nb - 1 - i) if rev else i

    def seq_of(i):
        b = blk_of(i)
        return jnp.where(b < nct, b // (SEQ // HB), BATCH + (b - nct) // (DEC_SEQ // HB))

    return pl.pallas_call(
        functools.partial(_hgrn_kernel, rev=rev),
        grid=(nb,),
        in_specs=[
            pl.BlockSpec((HB, HG_W), lambda i: (blk_of(i), COL_AQ // HG_W)),
            pl.BlockSpec((HB, HG_W), lambda i: (blk_of(i), (COL_AFB if rev else COL_AFF) // HG_W)),
            pl.BlockSpec((HB, HG_W), lambda i: (blk_of(i), COL_AI // HG_W)),
            pl.BlockSpec((None, 1, HG_W), lambda i: (d, 0, 0)),
            pl.BlockSpec((None, None, HG_W, HG_W), lambda i: (seq_of(i), d, 0, 0)),
            pl.BlockSpec((HG_W, HG_W), lambda i: (0, 0)),
        ],
        out_specs=[
            pl.BlockSpec((HB, HG_W), lambda i: (blk_of(i), 0)),
            pl.BlockSpec((None, HG_W, HG_W), lambda i: (seq_of(i), 0, 0)),
        ],
        out_shape=[
            jax.ShapeDtypeStruct((NTOK, HG_W), F32),
            jax.ShapeDtypeStruct((nseq, HG_W, HG_W), F32),
        ],
        scratch_shapes=[pltpu.VMEM((HG_W, HG_W), F32)],
        compiler_params=_params(("arbitrary",)),
        name="hgrn_bwd" if rev else "hgrn_fwd",
    )(z, z, z, lbs_l.reshape(2, 1, HG_W), s0t, bd)


def _pool_kernel(u_ref, w_ref, sc_ref, o_ref, *, t_len):
    u = u_ref[...]
    row = lax.broadcasted_iota(jnp.int32, (t_len, POOL_W), 0)
    grp = lax.broadcasted_iota(jnp.int32, (t_len, POOL_W), 1) // HEAD_DIM
    half = jnp.left_shift(1, grp)
    acc = jnp.zeros_like(u)
    for j in range(-8, 8):
        src = row + j
        ok = (j >= -half) & (j < half) & (src >= 0) & (src < t_len)
        shifted = u if j == 0 else pltpu.roll(u, (-j) % t_len, 0)
        acc = acc + jnp.where(ok, shifted, 0.0)
    cnt = (jnp.minimum(row + half, t_len) - jnp.maximum(row - half, 0)).astype(F32)
    y = _dot((acc / cnt - u).astype(BF16), w_ref[...]) * sc_ref[...]
    o_ref[...] = y.astype(BF16)


def _pool(z, row0, nseq, t_len, w_bd, scale):
    return pl.pallas_call(
        functools.partial(_pool_kernel, t_len=t_len),
        grid=(nseq,),
        in_specs=[
            pl.BlockSpec((t_len, POOL_W), lambda b: (row0 // t_len + b, COL_CU // POOL_W)),
            pl.BlockSpec((POOL_W, POOL_W), lambda b: (0, 0)),
            pl.BlockSpec((1, POOL_W), lambda b: (0, 0)),
        ],
        out_specs=pl.BlockSpec((t_len, POOL_W), lambda b: (b, 0)),
        out_shape=jax.ShapeDtypeStruct((nseq * t_len, POOL_W), BF16),
        compiler_params=_params(("arbitrary",)),
        name="pool",
    )(z, w_bd, scale)


def _rope(x, cos, sin):
    w = x.shape[-1]
    lane = lax.broadcasted_iota(jnp.int32, x.shape, 1)
    up = pltpu.roll(x, w - 16, 1)
    dn = pltpu.roll(x, 16, 1)
    return x * cos + jnp.where((lane & 31) < 16, up, dn) * sin


def _prep_kernel(*refs, rope):
    if rope:
        (bq, bk, bv, dq, dk, dv, gq, gk, gdq, gdk, bd, cq, sq, ck, sk,
         oq, ok_, ov, odq, odk, odv) = refs
    else:
        (bq, bk, bv, dq, dk, dv, gq, gk, gdq, gdk, bd,
         oq, ok_, ov, odq, odk, odv, ok32, odk32, ov32, odv32) = refs
    bdm = bd[...]
    q = _head_rms(bq[...], bdm, gq[...])
    k = _head_rms(bk[...], bdm[:WK_B, :WK_B], gk[...])
    qd = _head_rms(dq[...], bdm[:WD, :WD], gdq[...])
    kd = _head_rms(dk[...], bdm[:WD, :WD], gdk[...])
    if rope:
        q = _rope(q, cq[...], sq[...])
        k = _rope(k, ck[...], sk[...])
    else:
        ok32[...] = k
        odk32[...] = kd
        ov32[...] = bv[...]
        odv32[...] = dv[...]
    oq[...] = q.astype(BF16)
    ok_[...] = k.astype(BF16)
    ov[...] = bv[...].astype(BF16)
    odq[...] = qd.astype(BF16)
    odk[...] = kd.astype(BF16)
    odv[...] = dv[...].astype(BF16)


def _prep(z, row0, nrows, gains, bd, rope_tabs):
    tm = 512
    nt = nrows // tm
    r0 = row0 // tm
    rope = rope_tabs is not None
    col = lambda c, w: (lambda i: (r0 + i, c // w))
    in_specs = [
        pl.BlockSpec((tm, WQ_B), col(COL_BQ, WQ_B)),
        pl.BlockSpec((tm, WK_B), col(COL_BK, WK_B)),
        pl.BlockSpec((tm, WK_B), col(COL_BV, WK_B)),
        pl.BlockSpec((tm, WD), col(COL_DQ, WD)),
        pl.BlockSpec((tm, WD), col(COL_DK, WD)),
        pl.BlockSpec((tm, WD), col(COL_DV, WD)),
        pl.BlockSpec((1, WQ_B), lambda i: (0, 0)),
        pl.BlockSpec((1, WK_B), lambda i: (0, 0)),
        pl.BlockSpec((1, WD), lambda i: (0, 0)),
        pl.BlockSpec((1, WD), lambda i: (0, 0)),
        pl.BlockSpec((WQ_B, WQ_B), lambda i: (0, 0)),
    ]
    args = [z, z, z, z, z, z, *gains, bd]
    per = DEC_SEQ // tm
    if rope:
        in_specs += [
            pl.BlockSpec((tm, WQ_B), lambda i: (i % per, 0)),
            pl.BlockSpec((tm, WQ_B), lambda i: (i % per, 0)),
            pl.BlockSpec((tm, WK_B), lambda i: (i % per, 0)),
            pl.BlockSpec((tm, WK_B), lambda i: (i % per, 0)),
        ]
        args += list(rope_tabs)
    widths = [WQ_B, WK_B, WK_B, WD, WD, WD]
    out_specs = [pl.BlockSpec((tm, w), lambda i: (i, 0)) for w in widths]
    out_shape = [jax.ShapeDtypeStruct((nrows, w), BF16) for w in widths]
    if not rope:
        out_specs += [pl.BlockSpec((tm, w), lambda i: (i, 0)) for w in (WK_B, WD, WK_B, WD)]
        out_shape += [jax.ShapeDtypeStruct((nrows, w), F32) for w in (WK_B, WD, WK_B, WD)]
    return pl.pallas_call(
        functools.partial(_prep_kernel, rope=rope),
        grid=(nt,),
        in_specs=in_specs,
        out_specs=out_specs,
        out_shape=out_shape,
        compiler_params=_params(("arbitrary",)),
        name="prep_lat" if rope else "prep_ctx",
    )(*args)


def _softmax_pv(scores, values, sink):
    m = scores[0].max(axis=-1, keepdims=True)
    for s in scores[1:]:
        m = jnp.maximum(m, s.max(axis=-1, keepdims=True))
    if sink is not None:
        m = jnp.maximum(m, sink)
    den = jnp.zeros_like(m) if sink is None else jnp.exp(sink - m)
    acc = None
    for s, v in zip(scores, values):
        p = jnp.exp(s - m)
        den = den + p.sum(axis=-1, keepdims=True)
        pv = _dot(p.astype(BF16), v)
        acc = pv if acc is None else acc + pv
    return acc / den


def _hs(x, h):
    return x[:, HEAD_DIM * h:HEAD_DIM * (h + 1)]


def _ctx_attn_kernel(sink_ref, q_ref, k_ref, v_ref, qd_ref, kd_ref, vd_ref, ob_ref, od_ref):
    q, k, v = q_ref[...], k_ref[...], v_ref[...]
    g = WIN_HEADS // WIN_KV
    for h in range(WIN_HEADS):
        s = _dot_nt(_hs(q, h), _hs(k, h // g)) * SCALE
        o = _softmax_pv([s], [_hs(v, h // g)], sink_ref[0, h])
        ob_ref[:, HEAD_DIM * h:HEAD_DIM * (h + 1)] = o.astype(BF16)
    qd, kd, vd = qd_ref[...], kd_ref[...], vd_ref[...]
    for h in range(NA_HEADS):
        s = _dot_nt(_hs(qd, h), _hs(kd, h)) * SCALE
        o = _softmax_pv([s], [_hs(vd, h)], None)
        od_ref[:, HEAD_DIM * h:HEAD_DIM * (h + 1)] = o.astype(BF16)


def _ctx_attn(sink, q, k, v, qd, kd, vd):
    blk = lambda w: pl.BlockSpec((SEQ, w), lambda b: (b, 0))
    return pl.pallas_call(
        _ctx_attn_kernel,
        grid=(BATCH,),
        in_specs=[pl.BlockSpec(memory_space=pltpu.SMEM),
                  blk(WQ_B), blk(WK_B), blk(WK_B), blk(WD), blk(WD), blk(WD)],
        out_specs=[blk(WQ_B), blk(WD)],
        out_shape=[jax.ShapeDtypeStruct((N_CTX, WQ_B), BF16), jax.ShapeDtypeStruct((N_CTX, WD), BF16)],
        compiler_params=_params(("arbitrary",)),
        name="ctx_attn",
    )(sink, q, k, v, qd, kd, vd)


WIN_SPAN = 3 * WIN


def _win_attn_kernel(sink_ref, q_ref, k_ref, v_ref, kc_ref, vc_ref, o_ref):
    qi = pl.program_id(1)
    start = pl.multiple_of(jnp.clip(qi * WIN - WIN, 0, DEC_SEQ - WIN_SPAN), WIN)
    q = q_ref[...]
    kw = k_ref[pl.ds(start, WIN_SPAN), :]
    vw = v_ref[pl.ds(start, WIN_SPAN), :]
    kc, vc = kc_ref[...], vc_ref[...]
    qpos = qi * WIN + lax.broadcasted_iota(jnp.int32, (WIN, WIN_SPAN), 0)
    kpos = start + lax.broadcasted_iota(jnp.int32, (WIN, WIN_SPAN), 1)
    valid = jnp.abs(qpos - kpos) <= WIN
    g = WIN_HEADS // WIN_KV
    for h in range(WIN_HEADS):
        qh = _hs(q, h)
        s_loc = jnp.where(valid, _dot_nt(qh, _hs(kw, h // g)) * SCALE, NEG)
        s_ctx = _dot_nt(qh, _hs(kc, h // g)) * SCALE
        o = _softmax_pv([s_loc, s_ctx], [_hs(vw, h // g), _hs(vc, h // g)], sink_ref[0, h])
        o_ref[:, HEAD_DIM * h:HEAD_DIM * (h + 1)] = o.astype(BF16)


def _win_attn(sink, q, k, v, kc, vc):
    nq = DEC_SEQ // WIN
    return pl.pallas_call(
        _win_attn_kernel,
        grid=(DEC_BATCH, nq),
        in_specs=[
            pl.BlockSpec(memory_space=pltpu.SMEM),
            pl.BlockSpec((WIN, WQ_B), lambda b, i: (b * nq + i, 0)),
            pl.BlockSpec((None, DEC_SEQ, WK_B), lambda b, i: (b, 0, 0)),
            pl.BlockSpec((None, DEC_SEQ, WK_B), lambda b, i: (b, 0, 0)),
            pl.BlockSpec((None, PAST_LEN, WK_B), lambda b, i: (b, 0, 0)),
            pl.BlockSpec((None, PAST_LEN, WK_B), lambda b, i: (b, 0, 0)),
        ],
        out_specs=pl.BlockSpec((WIN, WQ_B), lambda b, i: (b * nq + i, 0)),
        out_shape=jax.ShapeDtypeStruct((N_LAT, WQ_B), BF16),
        compiler_params=_params(("arbitrary", "arbitrary")),
        name="win_attn",
    )(sink, q, k.reshape(DEC_BATCH, DEC_SEQ, WK_B), v.reshape(DEC_BATCH, DEC_SEQ, WK_B), kc, vc)


NA_ROWS = DEC_SEQ // GRID_W
NA_KEYS = NA_KH * GRID_W


def _na_row_start(r):
    return jnp.clip(r - NA_KH // 2, 0, NA_ROWS - NA_KH)


NA_RPS = 4


def _natten_kernel(q_ref, k_ref, v_ref, kc_ref, vc_ref, bias_ref, o_ref):
    kc, vc = kc_ref[...], vc_ref[...]
    for j in range(NA_RPS):
        r = pl.program_id(1) * NA_RPS + j
        row0 = _na_row_start(r)
        start = pl.multiple_of(row0 * GRID_W, GRID_W)
        delta = row0 - r + NA_KH - 1
        q = q_ref[GRID_W * j:GRID_W * (j + 1), :]
        kw = k_ref[pl.ds(start, NA_KEYS), :]
        vw = v_ref[pl.ds(start, NA_KEYS), :]
        for h in range(NA_HEADS):
            qh = _hs(q, h)
            s_loc = _dot_nt(qh, _hs(kw, h)) * SCALE + bias_ref[delta, h]
            s_ctx = _dot_nt(qh, _hs(kc, h)) * SCALE
            o = _softmax_pv([s_loc, s_ctx], [_hs(vw, h), _hs(vc, h)], None)
            o_ref[GRID_W * j:GRID_W * (j + 1), HEAD_DIM * h:HEAD_DIM * (h + 1)] = o.astype(BF16)


def _natten(q, k, v, kc, vc, bias):
    nsteps = NA_ROWS // NA_RPS
    rows = NA_RPS * GRID_W
    return pl.pallas_call(
        _natten_kernel,
        grid=(DEC_BATCH, nsteps),
        in_specs=[
            pl.BlockSpec((rows, WD), lambda b, r: (b * nsteps + r, 0)),
            pl.BlockSpec((None, DEC_SEQ, WD), lambda b, r: (b, 0, 0)),
            pl.BlockSpec((None, DEC_SEQ, WD), lambda b, r: (b, 0, 0)),
            pl.BlockSpec((None, PAST_LEN, WD), lambda b, r: (b, 0, 0)),
            pl.BlockSpec((None, PAST_LEN, WD), lambda b, r: (b, 0, 0)),
            pl.BlockSpec(bias.shape, lambda b, r: (0, 0, 0, 0)),
        ],
        out_specs=pl.BlockSpec((rows, WD), lambda b, r: (b * nsteps + r, 0)),
        out_shape=jax.ShapeDtypeStruct((N_LAT, WD), BF16),
        compiler_params=_params(("arbitrary", "arbitrary")),
        name="natten",
    )(q, k.reshape(DEC_BATCH, DEC_SEQ, WD), v.reshape(DEC_BATCH, DEC_SEQ, WD), kc, vc, bias)


def _natten_bias(rpb):
    c = np.arange(GRID_W)
    cstart = np.clip(c - NA_KW // 2, 0, GRID_W - NA_KW)
    inwin = (c[None, :] >= cstart[:, None]) & (c[None, :] < cstart[:, None] + NA_KW)
    dc = np.clip(c[None, :] - c[:, None] + NA_KW - 1, 0, 2 * NA_KW - 2)
    pick = (dc[None] == np.arange(2 * NA_KW - 1)[:, None, None]).astype(np.float32)
    toep = jnp.einsum('hdj,jck->hdck', rpb.astype(F32), jnp.asarray(pick),
                      precision=lax.Precision.HIGHEST)
    toep = jnp.where(inwin[None, None], toep, NEG)
    out = []
    for d0 in range(NA_KH):
        out.append(jnp.concatenate([toep[:, d0 + i] for i in range(NA_KH)], axis=-1))
    return jnp.stack(out, axis=0)


def _merge_kernel(x_ref, m_ref, of_ref, ob_ref, ag_ref, on_ref, bd_ref,
                  ybc_ref, ybl_ref, ycc_ref, ycl_ref, ydc_ref, ydl_ref,
                  gate_ref, wa_ref, wb_ref, wc_ref, wd_ref, wo_ref, o_ref):
    i = pl.program_id(0)
    is_ctx = i < N_CTX // TM
    o = of_ref[...] + ob_ref[...]
    ya = _head_rms(o, bd_ref[...], on_ref[...]) * _silu(ag_ref[...])
    yb = jnp.where(is_ctx, ybc_ref[...], ybl_ref[...])
    yc = jnp.where(is_ctx, ycc_ref[...], ycl_ref[...])
    yd = jnp.where(is_ctx, ydc_ref[...], ydl_ref[...])
    d = D_MODEL
    merged = (gate_ref[:, 0:d].astype(F32) * _dot(ya.astype(BF16), wa_ref[...])
              + gate_ref[:, d:2 * d].astype(F32) * _dot(yb, wb_ref[...])
              + gate_ref[:, 2 * d:3 * d].astype(F32) * _dot(yc, wc_ref[...])
              + gate_ref[:, 3 * d:4 * d].astype(F32) * _dot(yd, wd_ref[...]))
    o_ref[...] = x_ref[...] + m_ref[2:3, :] * _dot(merged.astype(BF16), wo_ref[...])


def _merge(x, mods, o_f, o_b, z, onorm, bd, yb_c, yb_l, yc_c, yc_l, yd_c, yd_l, gates, wa, wb, wc, wd, wo):
    nt = NTOK // TM
    nct = N_CTX // TM
    cspec = lambda w: pl.BlockSpec((TM, w), lambda i: (jnp.minimum(i, nct - 1), 0))
    lspec = lambda w: pl.BlockSpec((TM, w), lambda i: (jnp.maximum(i - nct, 0), 0))
    full = lambda a: pl.BlockSpec(a.shape, lambda i: (0, 0))
    return pl.pallas_call(
        _merge_kernel,
        grid=(nt,),
        in_specs=[
            pl.BlockSpec((TM, D_MODEL), lambda i: (i, 0)),
            pl.BlockSpec((None, 6, D_MODEL), lambda i: (_seg_of_tile(i, TM), 0, 0)),
            pl.BlockSpec((TM, HG_W), lambda i: (i, 0)),
            pl.BlockSpec((TM, HG_W), lambda i: (i, 0)),
            pl.BlockSpec((TM, HG_W), lambda i: (i, COL_AG // HG_W)),
            full(onorm), full(bd),
            cspec(WQ_B), lspec(WQ_B), cspec(POOL_W), lspec(POOL_W), cspec(WD), lspec(WD),
            pl.BlockSpec((TM, GATE_W), lambda i: (i, 0)),
            full(wa), full(wb), full(wc), full(wd), full(wo),
        ],
        out_specs=pl.BlockSpec((TM, D_MODEL), lambda i: (i, 0)),
        out_shape=jax.ShapeDtypeStruct((NTOK, D_MODEL), F32),
        compiler_params=_params(("arbitrary",)),
        name="merge",
    )(x, mods, o_f, o_b, z, onorm, bd, yb_c, yb_l, yc_c, yc_l, yd_c, yd_l, gates, wa, wb, wc, wd, wo)


E_PAD = 128
TR = 256
RT = 256
SLAB = 16
NT_R = NTOK // TR
S_LOC = 1536
S_MAX = NTOK * TOP_K + NT_R * N_EXPERTS * (SLAB - 1) + N_EXPERTS * (RT - SLAB)
N_XT = S_MAX // RT
XW = D_MODEL + 2 * E_PAD
assert S_LOC >= TR * TOP_K + N_EXPERTS * (SLAB - 1) and S_LOC % 128 == 0 and S_MAX % RT == 0


def _router_kernel(x_ref, m_ref, g_ref, wr_ref, br_ref, wsg_ref, wsu_ref, wsd_ref, tri_ref, ut_ref,
                   xa_ref, sl_ref, slt_ref, cnt_ref, xs_ref):
    lane = lax.broadcasted_iota(jnp.int32, (TR, E_PAD), 1)
    x = x_ref[...]
    ms = jnp.mean(x * x, axis=-1, keepdims=True)
    y = x * lax.rsqrt(ms + EPS) * g_ref[...]
    h = y * (1.0 + m_ref[4:5, :]) + m_ref[3:4, :]
    h_hi = h.astype(BF16)
    h_lo = (h - h_hi.astype(F32)).astype(BF16)
    w = wr_ref[...]
    w_hi = w.astype(BF16)
    w_lo = (w - w_hi.astype(F32)).astype(BF16)
    logits = _dot(h_hi, w_hi) + _dot(h_hi, w_lo) + _dot(h_lo, w_hi)
    scores = jax.nn.sigmoid(logits)
    sel = jnp.where(lane < N_EXPERTS, scores + br_ref[...], -jnp.inf)
    picked = jnp.zeros((TR, E_PAD), F32)
    hot = jnp.zeros((TR, E_PAD), F32)
    idxs = []
    for _ in range(TOP_K):
        mx = sel.max(axis=-1, keepdims=True)
        idx = jnp.min(jnp.where(sel == mx, lane, E_PAD), axis=-1, keepdims=True)
        hit = lane == idx
        picked = jnp.where(hit, scores, picked)
        hot = jnp.where(hit, 1.0, hot)
        sel = jnp.where(hit, -jnp.inf, sel)
        idxs.append(idx)
    wts = ROUTED_SCALE * picked / picked.sum(axis=-1, keepdims=True)

    cnt = hot.sum(axis=0, keepdims=True)
    pad = jnp.floor((cnt + (SLAB - 1.0)) * (1.0 / SLAB)) * SLAB
    loc = _dot(jnp.broadcast_to(pad, (8, E_PAD)).astype(BF16), ut_ref[...])[0:1, :]
    rank = _dot(tri_ref[...], hot.astype(BF16))
    slotmat = loc + rank
    sl = jnp.zeros((TR, E_PAD), F32)
    for k in range(TOP_K):
        s_k = jnp.sum(jnp.where(lane == idxs[k], slotmat, 0.0), axis=-1, keepdims=True)
        sl = jnp.where(lane == k, s_k, sl)
    sl_ref[...] = sl
    slt_ref[...] = sl.T[0:8, :]
    cnt_ref[...] = pad

    w16 = wts.astype(BF16)
    xa_ref[:, 0:D_MODEL] = h_hi
    xa_ref[:, D_MODEL:D_MODEL + E_PAD] = w16
    xa_ref[:, D_MODEL + E_PAD:XW] = (wts - w16.astype(F32)).astype(BF16)

    a = _silu(_dot(h_hi, wsg_ref[...])) * _dot(h_hi, wsu_ref[...])
    xs_ref[...] = x + m_ref[5:6, :] * _dot(a.astype(BF16), wsd_ref[...])


def _router(x, mods, norm_g, w_router, b_router, wsg, wsu, wsd, tri, ut):
    full = lambda a: pl.BlockSpec(a.shape, lambda i: (0,) * a.ndim)
    return pl.pallas_call(
        _router_kernel,
        grid=(NT_R,),
        in_specs=[
            pl.BlockSpec((TR, D_MODEL), lambda i: (i, 0)),
            pl.BlockSpec((None, 6, D_MODEL), lambda i: (_seg_of_tile(i, TR), 0, 0)),
            full(norm_g), full(w_router), full(b_router), full(wsg), full(wsu), full(wsd), full(tri), full(ut),
        ],
        out_specs=[
            pl.BlockSpec((TR, XW), lambda i: (i, 0)),
            pl.BlockSpec((TR, E_PAD), lambda i: (i, 0)),
            pl.BlockSpec((None, 8, TR), lambda i: (i, 0, 0)),
            pl.BlockSpec((None, 1, E_PAD), lambda i: (i, 0, 0)),
            pl.BlockSpec((TR, D_MODEL), lambda i: (i, 0)),
        ],
        out_shape=[
            jax.ShapeDtypeStruct((NTOK, XW), BF16),
            jax.ShapeDtypeStruct((NTOK, E_PAD), F32),
            jax.ShapeDtypeStruct((NT_R, 8, TR), F32),
            jax.ShapeDtypeStruct((NT_R, 1, E_PAD), F32),
            jax.ShapeDtypeStruct((NTOK, D_MODEL), F32),
        ],
        compiler_params=_params(("arbitrary",)),
        name="router",
    )(x, mods, norm_g, w_router, b_router, wsg, wsu, wsd, tri, ut)


def _slab_copy(src, src_row, dst, dst_row, sem):
    hint = lambda r: r if isinstance(r, int) else pl.multiple_of(r, SLAB)
    return pltpu.make_async_copy(src.at[pl.ds(hint(src_row), SLAB), :],
                                 dst.at[pl.ds(hint(dst_row), SLAB), :], sem)


def _for_each_slab(loc_ref, dst_ref, pc_ref, t, fn):
    def per_expert(e, carry):
        lo = loc_ref[t * N_EXPERTS + e]
        do = dst_ref[t * N_EXPERTS + e]

        def per_slab(s, c):
            fn(lo + s * SLAB, do + s * SLAB)
            return c

        lax.fori_loop(0, pc_ref[t * N_EXPERTS + e] // SLAB, per_slab, 0)
        return carry

    lax.fori_loop(0, N_EXPERTS, per_expert, 0)


def _compact_kernel(loc_ref, dst_ref, pc_ref, nd_ref, toff_ref, tn_ref, nu_ref,
                    xa_ref, slt_ref, xs_hbm, xc_scr, zero_scr, sem):
    t = pl.program_id(0)
    slot = t % 2

    def wait_n(n, s):
        def body(_, c):
            _slab_copy(xc_scr.at[s], 0, xs_hbm, 0, sem.at[s]).wait()
            return c
        lax.fori_loop(0, n, body, 0)

    def unused_tile_copy(j):
        row = pl.multiple_of((nu_ref[0] + j) * RT, RT)
        return pltpu.make_async_copy(zero_scr, xs_hbm.at[pl.ds(row, RT), :], sem.at[2])

    @pl.when(t == 0)
    def _():
        zero_scr[...] = jnp.zeros_like(zero_scr)

        def body(j, c):
            unused_tile_copy(j).start()
            return c
        lax.fori_loop(0, N_XT - nu_ref[0], body, 0)

    @pl.when(t >= 2)
    def _():
        wait_n(nd_ref[t - 2], slot)

    row = lax.broadcasted_iota(jnp.int32, (S_LOC, TR), 0)
    slt = slt_ref[...].astype(jnp.int32)
    hit = row == slt[0:1, :]
    for k in range(1, TOP_K):
        hit = hit | (row == slt[k:k + 1, :])
    onehot = jnp.where(hit, 1.0, 0.0).astype(BF16)
    xc_scr[slot] = _dot(onehot, xa_ref[...]).astype(BF16)

    _for_each_slab(loc_ref, dst_ref, pc_ref, t,
                   lambda lr, gr: _slab_copy(xc_scr.at[slot], lr, xs_hbm, gr, sem.at[slot]).start())

    @pl.when(t < N_EXPERTS)
    def _():
        def body(s, c):
            _slab_copy(zero_scr, 0, xs_hbm, toff_ref[t] + s * SLAB, sem.at[slot]).start()
            return c
        lax.fori_loop(0, tn_ref[t], body, 0)

    @pl.when(t == NT_R - 1)
    def _():
        wait_n(nd_ref[t], slot)
        wait_n(nd_ref[t - 1], 1 - slot)

        def body(j, c):
            unused_tile_copy(j).wait()
            return c
        lax.fori_loop(0, N_XT - nu_ref[0], body, 0)


def _compact(meta, xa, slt):
    grid_spec = pltpu.PrefetchScalarGridSpec(
        num_scalar_prefetch=7,
        grid=(NT_R,),
        in_specs=[
            pl.BlockSpec((TR, XW), lambda i, *_: (i, 0)),
            pl.BlockSpec((None, 8, TR), lambda i, *_: (i, 0, 0)),
        ],
        out_specs=pl.BlockSpec(memory_space=pl.ANY),
        scratch_shapes=[pltpu.VMEM((2, S_LOC, XW), BF16), pltpu.VMEM((RT, XW), BF16),
                        pltpu.SemaphoreType.DMA((3,))],
    )
    return pl.pallas_call(
        _compact_kernel,
        grid_spec=grid_spec,
        out_shape=jax.ShapeDtypeStruct((S_MAX, XW), BF16),
        compiler_params=_params(("arbitrary",)),
        name="compact",
    )(meta['loc'], meta['dst'], meta['pc'], meta['nd'], meta['toff'], meta['tn'], meta['nu'], xa, slt)


def _expert_kernel(te_ref, ti_ref, nu_ref, xs_ref, wg_ref, wu_ref, wd_ref, y_ref, wg_s, wu_s, wd_s):
    i = pl.program_id(0)
    e = te_ref[i]

    @pl.when((i == 0) | (e != te_ref[jnp.maximum(i - 1, 0)]))
    def _():
        wg_s[...] = wg_ref[...].astype(BF16)
        wu_s[...] = wu_ref[...].astype(BF16)
        wd_s[...] = wd_ref[...].astype(BF16)

    @pl.when(i < nu_ref[0])
    def _():
        x = xs_ref[:, 0:D_MODEL]
        gw = xs_ref[:, D_MODEL:D_MODEL + E_PAD].astype(F32) + xs_ref[:, D_MODEL + E_PAD:XW].astype(F32)
        lane = lax.broadcasted_iota(jnp.int32, (RT, E_PAD), 1)
        ge = jnp.sum(jnp.where(lane == e, gw, 0.0), axis=-1, keepdims=True)
        a = _silu(_dot(x, wg_s[...])) * _dot(x, wu_s[...])
        y_ref[...] = _dot((a * ge).astype(BF16), wd_s[...]).astype(BF16)

    @pl.when(i >= nu_ref[0])
    def _():
        y_ref[...] = jnp.zeros_like(y_ref)


def _experts(meta, layer, xs, w_eg, w_eu, w_ed):
    wspec = lambda r, c: pl.BlockSpec((None, None, r, c), lambda i, te, ti, nu: (layer, te[i], 0, 0))
    grid_spec = pltpu.PrefetchScalarGridSpec(
        num_scalar_prefetch=3,
        grid=(N_XT,),
        in_specs=[
            pl.BlockSpec((RT, XW), lambda i, te, ti, nu: (ti[i], 0)),
            wspec(D_MODEL, D_EXPERT), wspec(D_MODEL, D_EXPERT), wspec(D_EXPERT, D_MODEL),
        ],
        out_specs=pl.BlockSpec((RT, D_MODEL), lambda i, te, ti, nu: (i, 0)),
        scratch_shapes=[pltpu.VMEM((D_MODEL, D_EXPERT), BF16), pltpu.VMEM((D_MODEL, D_EXPERT), BF16),
                        pltpu.VMEM((D_EXPERT, D_MODEL), BF16)],
    )
    return pl.pallas_call(
        _expert_kernel,
        grid_spec=grid_spec,
        out_shape=jax.ShapeDtypeStruct((S_MAX, D_MODEL), BF16),
        compiler_params=_params(("arbitrary",)),
        name="experts",
    )(meta['te'], meta['ti'], meta['nu'], xs, w_eg, w_eu, w_ed)


def _combine_kernel(loc_ref, dst_ref, pc_ref, ns_ref, xs_ref, m_ref, sl_ref, y_hbm, *rest, split):
    if split:
        oc_ref, ol_ref, yc_scr, sem = rest
    else:
        o_ref, yc_scr, sem = rest
    t = pl.program_id(0)
    slot = t % 2

    def issue(tt, s):
        _for_each_slab(loc_ref, dst_ref, pc_ref, tt,
                       lambda lr, gr: _slab_copy(y_hbm, gr, yc_scr.at[s], lr, sem.at[s]).start())

    @pl.when(t == 0)
    def _():
        yc_scr[...] = jnp.zeros_like(yc_scr)
        issue(0, 0)

    @pl.when(t + 1 < NT_R)
    def _():
        issue(t + 1, 1 - slot)

    def wait_body(_, c):
        _slab_copy(y_hbm, 0, yc_scr.at[slot], 0, sem.at[slot]).wait()
        return c
    lax.fori_loop(0, ns_ref[t], wait_body, 0)

    col = lax.broadcasted_iota(jnp.int32, (TR, S_LOC), 1)
    sl = sl_ref[...].astype(jnp.int32)
    hit = col == sl[:, 0:1]
    for k in range(1, TOP_K):
        hit = hit | (col == sl[:, k:k + 1])
    onehot = jnp.where(hit, 1.0, 0.0).astype(BF16)
    out = xs_ref[...] + m_ref[5:6, :] * _dot(onehot, yc_scr[slot])
    if split:
        @pl.when(t < N_CTX // TR)
        def _():
            oc_ref[...] = out

        @pl.when(t >= N_CTX // TR)
        def _():
            ol_ref[...] = out
    else:
        o_ref[...] = out


def _combine(meta, xsh, mods, sl, y, split):
    nct = N_CTX // TR
    if split:
        out_specs = [pl.BlockSpec((TR, D_MODEL), lambda i, *_: (jnp.minimum(i, nct - 1), 0)),
                     pl.BlockSpec((TR, D_MODEL), lambda i, *_: (jnp.maximum(i - nct, 0), 0))]
        out_shape = [jax.ShapeDtypeStruct((N_CTX, D_MODEL), F32), jax.ShapeDtypeStruct((N_LAT, D_MODEL), F32)]
    else:
        out_specs = pl.BlockSpec((TR, D_MODEL), lambda i, *_: (i, 0))
        out_shape = jax.ShapeDtypeStruct((NTOK, D_MODEL), F32)
    grid_spec = pltpu.PrefetchScalarGridSpec(
        num_scalar_prefetch=4,
        grid=(NT_R,),
        in_specs=[
            pl.BlockSpec((TR, D_MODEL), lambda i, *_: (i, 0)),
            pl.BlockSpec((None, 6, D_MODEL), lambda i, *_: (_seg_of_tile(i, TR), 0, 0)),
            pl.BlockSpec((TR, E_PAD), lambda i, *_: (i, 0)),
            pl.BlockSpec(memory_space=pl.ANY),
        ],
        out_specs=out_specs,
        scratch_shapes=[pltpu.VMEM((2, S_LOC, D_MODEL), BF16), pltpu.SemaphoreType.DMA((2,))],
    )
    return pl.pallas_call(
        functools.partial(_combine_kernel, split=split),
        grid_spec=grid_spec,
        out_shape=out_shape,
        compiler_params=_params(("arbitrary",)),
        name="combine_split" if split else "combine",
    )(meta['loc'], meta['dst'], meta['pc'], meta['ns'], xsh, mods, sl, y)


def _route_meta(cnt):
    pc = cnt[:, 0, :N_EXPERTS].astype(jnp.int32)
    tot = pc.sum(axis=0)
    tot_pad = ((tot + RT - 1) // RT) * RT
    ends = jnp.cumsum(tot_pad)
    base = ends - tot_pad
    dst = base[None, :] + jnp.cumsum(pc, axis=0) - pc
    loc = jnp.cumsum(pc, axis=1) - pc
    ns = pc.sum(axis=1) // SLAB
    tn = (tot_pad - tot) // SLAB
    nd = ns + jnp.pad(tn, (0, NT_R - N_EXPERTS))
    n_used = ends[-1] // RT
    ti = jnp.minimum(jnp.arange(N_XT, dtype=jnp.int32), n_used - 1)
    te = jnp.minimum(jnp.sum(ends[None, :] <= (ti * RT)[:, None], axis=1), N_EXPERTS - 1)
    i32 = lambda a: a.astype(jnp.int32)
    return dict(loc=i32(loc.reshape(-1)), dst=i32(dst.reshape(-1)), pc=i32(pc.reshape(-1)), ns=i32(ns),
                nd=i32(nd), toff=i32(base + tot), tn=i32(tn), te=i32(te), ti=i32(ti),
                nu=i32(n_used.reshape(1)))


def _moe(layer, x, mods, norm_g, w_router, b_router, w_eg, w_eu, w_ed, wsg, wsu, wsd, tri, ut):
    xa, sl, slt, cnt, xsh = _router(x, mods, norm_g, w_router, b_router, wsg, wsu, wsd, tri, ut)
    meta = _route_meta(cnt)
    xs = _compact(meta, xa, slt)
    y = _experts(meta, layer, xs, w_eg, w_eu, w_ed)
    return _combine(meta, xsh, mods, sl, y, split=(layer == DEPTH - 1))


def _rope_tables(width):
    t = np.arange(DEC_SEQ)
    quarter = HEAD_DIM // 4
    inv = (ROPE_BASE ** (-np.arange(quarter) / quarter)).astype(np.float32)
    ang_r = (t // GRID_W).astype(np.float32)[:, None] * inv[None]
    ang_c = (t % GRID_W).astype(np.float32)[:, None] * inv[None]
    cos = np.concatenate([np.cos(ang_r), np.cos(ang_r), np.cos(ang_c), np.cos(ang_c)], axis=1)
    sin = np.concatenate([-np.sin(ang_r), np.sin(ang_r), -np.sin(ang_c), np.sin(ang_c)], axis=1)
    reps = width // HEAD_DIM
    return (jnp.asarray(np.tile(cos, (1, reps)), F32), jnp.asarray(np.tile(sin, (1, reps)), F32))


def _permute_w_in(w):
    a = w[:, 0:1280]
    bq, bk, bv = w[:, 1280:1664], w[:, 1664:1792], w[:, 1792:1920]
    cu = w[:, 1920:2176]
    d = w[:, 2176:2944]
    pad = jnp.zeros((w.shape[0], Z_W - 2944), w.dtype)
    return jnp.concatenate([a, cu, d, bq, bk, bv, pad], axis=1)


def _block_diag(blocks):
    g = blocks.shape[0]
    eye = jnp.eye(g, dtype=blocks.dtype)
    return jnp.einsum('gh,gij->gihj', eye, blocks).reshape(g * HEAD_DIM, g * HEAD_DIM)


def kernel(x_prompt, x_sample, cache_win_k, cache_win_v, cache_na_k, cache_na_v, state_hgrn, c, c_ctx, w_mod, b_mod, norm1_g, norm2_g, w_in, w_mgate, b_mgate, hg_lb, hg_onorm, win_qn, win_kn, win_sink, pool_w, pool_scale, na_qn, na_kn, na_rpb, w_branch, w_out, w_router, b_router, w_eg, w_eu, w_ed, w_sg, w_su, w_sd):
    lbp = jax.nn.softmax(hg_lb.astype(F32), axis=0)
    lbs = jnp.cumsum(lbp, axis=0) - lbp[0:1]

    cvec8 = jnp.concatenate([c_ctx[None], c, jnp.zeros((3, D_MODEL), F32)], axis=0)
    mods_all = _modulation(cvec8, w_mod, b_mod).reshape(DEPTH, 8, 6, D_MODEL)

    bd384 = jnp.asarray(_bd_ones(WQ_B), BF16)
    bd256 = bd384[:HG_W, :HG_W]
    bd256_f32 = jnp.asarray(_bd_ones(HG_W), F32)
    rope_q = _rope_tables(WQ_B)
    rope_k = _rope_tables(WK_B)
    tile = lambda g, reps: jnp.tile(g, reps)[None, :]
    tri = jnp.asarray(np.tril(np.ones((TR, TR), np.float32), -1), BF16)
    ut = jnp.asarray(np.triu(np.ones((E_PAD, E_PAD), np.float32), 1), BF16)

    x = jnp.concatenate([x_prompt.reshape(N_CTX, D_MODEL), x_sample.reshape(N_LAT, D_MODEL)], axis=0)
    new_k, new_v, new_kd, new_vd, new_s = [], [], [], [], []
    for l in range(DEPTH):
        mods = mods_all[l]
        z, gates = _projection(x, mods, norm1_g[l][None], _permute_w_in(w_in[l]).astype(BF16),
                               w_mgate[l].astype(BF16), b_mgate[l][None])

        st_lat = _block_diag_states(state_hgrn[:, l])
        s0t = jnp.concatenate([jnp.zeros((BATCH, 2, HG_W, HG_W), F32), st_lat], axis=0)
        o_f, sfin_f = _hgrn(z, lbs[l], s0t, bd256_f32, rev=False)
        o_b, sfin_b = _hgrn(z, lbs[l], s0t, bd256_f32, rev=True)
        new_s.append(jnp.stack([_unblock_states(sfin_f[:BATCH]), _unblock_states(sfin_b[:BATCH])], axis=1))

        w_pool = _block_diag(pool_w[l]).astype(BF16)
        yc_c = _pool(z, 0, BATCH, SEQ, w_pool, pool_scale[l][None])
        yc_l = _pool(z, N_CTX, DEC_BATCH, DEC_SEQ, w_pool, pool_scale[l][None])

        gains = (tile(win_qn[l], WIN_HEADS), tile(win_kn[l], WIN_KV), tile(na_qn[l], NA_HEADS), tile(na_kn[l], NA_HEADS))
        qb_c, kb_c, vb_c, qd_c, kd_c, vd_c, kb32, kd32, vb32, vd32 = _prep(z, 0, N_CTX, gains, bd384, None)
        qb_l, kb_l, vb_l, qd_l, kd_l, vd_l = _prep(z, N_CTX, N_LAT, gains, bd384, rope_q + rope_k)
        sink = win_sink[l][None]
        yb_c, yd_c = _ctx_attn(sink, qb_c, kb_c, vb_c, qd_c, kd_c, vd_c)
        kc = cache_win_k[:, l].reshape(DEC_BATCH, PAST_LEN, WK_B).astype(BF16)
        vc = cache_win_v[:, l].reshape(DEC_BATCH, PAST_LEN, WK_B).astype(BF16)
        yb_l = _win_attn(sink, qb_l, kb_l, vb_l, kc, vc)
        kcd = cache_na_k[:, l].reshape(DEC_BATCH, PAST_LEN, WD).astype(BF16)
        vcd = cache_na_v[:, l].reshape(DEC_BATCH, PAST_LEN, WD).astype(BF16)
        yd_l = _natten(qd_l, kd_l, vd_l, kcd, vcd, _natten_bias(na_rpb[l]))

        new_k.append(kb32.reshape(BATCH, SEQ, WIN_KV, HEAD_DIM))
        new_v.append(vb32.reshape(BATCH, SEQ, WIN_KV, HEAD_DIM))
        new_kd.append(kd32.reshape(BATCH, SEQ, NA_HEADS, HEAD_DIM))
        new_vd.append(vd32.reshape(BATCH, SEQ, NA_HEADS, HEAD_DIM))

        wbr = w_branch[l].astype(BF16)
        x = _merge(x, mods, o_f, o_b, z, tile(hg_onorm[l], HG_HEADS), bd256,
                   yb_c, yb_l, yc_c, yc_l, yd_c, yd_l, gates,
                   wbr[0:256], wbr[256:640], wbr[640:896], wbr[896:1152], w_out[l].astype(BF16))

        wr = jnp.pad(w_router[l], ((0, 0), (0, E_PAD - N_EXPERTS)))
        br = jnp.pad(b_router[l], (0, E_PAD - N_EXPERTS))[None]
        x = _moe(l, x, mods, norm2_g[l][None], wr, br, w_eg, w_eu, w_ed,
                 w_sg[l].astype(BF16), w_su[l].astype(BF16), w_sd[l].astype(BF16), tri, ut)

    y_p = x[0].reshape(BATCH, SEQ, D_MODEL)
    y_s = x[1].reshape(DEC_BATCH, DEC_SEQ, D_MODEL)
    return (y_p, y_s, jnp.stack(new_k, axis=1), jnp.stack(new_v, axis=1), jnp.stack(new_kd, axis=1),
            jnp.stack(new_vd, axis=1), jnp.stack(new_s, axis=1))


def _block_diag_states(s):
    b = s.shape[0]
    st = jnp.swapaxes(s.astype(F32), -1, -2)
    eye = jnp.eye(HG_HEADS, dtype=F32)
    return jnp.einsum('gh,bdgvk->bdgvhk', eye, st).reshape(b, 2, HG_W, HG_W)


def _unblock_states(st):
    b = st.shape[0]
    s5 = st.reshape(b, HG_HEADS, HEAD_DIM, HG_HEADS, HEAD_DIM)
    diag = jnp.stack([s5[:, h, :, h, :] for h in range(HG_HEADS)], axis=1)
    return jnp.swapaxes(diag, -1, -2)
```

```python
import functools

import numpy as np
import jax
import jax.numpy as jnp
from jax import lax
from jax.experimental import pallas as pl
from jax.experimental.pallas import tpu as pltpu

F32 = jnp.float32
BF16 = jnp.bfloat16

D_MODEL = 1024
BATCH = 16
SEQ = 256
DEPTH = 2
DEC_BATCH = 4
DEC_SEQ = 2048
PAST_LEN = 256
NEG = -1e30
GRID_W = 64
HEAD_DIM = 64
SCALE = HEAD_DIM ** -0.5
ROPE_BASE = 10000.0
EPS = 1e-6
HG_HEADS = 4
HG_W = 256
WIN_HEADS = 6
WIN_KV = 2
WIN = 128
POOL_SIZES = (2, 4, 8, 16)
POOL_W = 256
NA_HEADS = 4
NA_KH = 8
NA_KW = 16
N_EXPERTS = 32
TOP_K = 4
D_EXPERT = 256
ROUTED_SCALE = 2.5

N_CTX = BATCH * SEQ
N_LAT = DEC_BATCH * DEC_SEQ
NTOK = N_CTX + N_LAT
WQ_B = WIN_HEADS * HEAD_DIM
WK_B = WIN_KV * HEAD_DIM
WD = NA_HEADS * HEAD_DIM

Z_W = 3072
COL_AQ, COL_AFF, COL_AFB, COL_AI, COL_AG, COL_CU, COL_DQ, COL_DK, COL_DV = (
    0, 256, 512, 768, 1024, 1280, 1536, 1792, 2048)
COL_BQ, COL_BK, COL_BV = 2304, 2688, 2816

TM = 512
TMP = 2048
TN = 512
HB = 128
VMEM_LIMIT = 56 * 1024 * 1024


def _params(sem, vmem=VMEM_LIMIT):
    return pltpu.CompilerParams(dimension_semantics=sem, vmem_limit_bytes=vmem)


def _seg_of_tile(i, tile):
    nct = N_CTX // tile
    per = DEC_SEQ // tile
    return jnp.where(i < nct, 0, 1 + (i - nct) // per)


def _bd_ones(w):
    idx = np.arange(w) // HEAD_DIM
    return (idx[:, None] == idx[None, :]).astype(np.float32)


def _dot(a, b):
    return jnp.dot(a, b, preferred_element_type=F32)


def _dot_nt(a, b):
    return lax.dot_general(a, b, (((1,), (1,)), ((), ())), preferred_element_type=F32)


def _split_dot(x, w_bf16):
    hi = x.astype(BF16)
    lo = (x - hi.astype(F32)).astype(BF16)
    return _dot(hi, w_bf16) + _dot(lo, w_bf16)


def _head_rms(x, bd, gain):
    ms = _split_dot(x * x, bd) * (1.0 / HEAD_DIM)
    return x * lax.rsqrt(ms + EPS) * gain


def _silu(x):
    return x * jax.nn.sigmoid(x)


def _mod_kernel(c_ref, w_ref, b_ref, o_ref):
    c = c_ref[...]
    a = _silu(c).astype(BF16)
    o_ref[...] = _dot(a, w_ref[...].astype(BF16)) + b_ref[...]


def _modulation(cvec8, w_mod, b_mod):
    n = 6 * D_MODEL
    tn = 1536
    return pl.pallas_call(
        _mod_kernel,
        grid=(DEPTH, n // tn),
        in_specs=[
            pl.BlockSpec((8, D_MODEL), lambda l, j: (0, 0)),
            pl.BlockSpec((None, D_MODEL, tn), lambda l, j: (l, 0, j)),
            pl.BlockSpec((None, 1, tn), lambda l, j: (l, 0, j)),
        ],
        out_specs=pl.BlockSpec((None, 8, tn), lambda l, j: (l, 0, j)),
        out_shape=jax.ShapeDtypeStruct((DEPTH, 8, n), F32),
        compiler_params=_params(("arbitrary", "arbitrary")),
        name="modulation",
    )(cvec8, w_mod, b_mod.reshape(DEPTH, 1, n))


def _prenorm(x, gain, shift, scale):
    ms = jnp.mean(x * x, axis=-1, keepdims=True)
    return x * lax.rsqrt(ms + EPS) * gain * (1.0 + scale) + shift


def _proj_kernel(x_ref, m_ref, g_ref, win_ref, z_ref, h_scr):
    @pl.when(pl.program_id(1) == 0)
    def _():
        h_scr[...] = _prenorm(x_ref[...], g_ref[...], m_ref[0:1, :], m_ref[1:2, :]).astype(BF16)

    z_ref[...] = _dot(h_scr[...], win_ref[...])


def _projection(x, mods, norm_g, w_in_p):
    return pl.pallas_call(
        _proj_kernel,
        grid=(NTOK // TMP, Z_W // TN),
        in_specs=[
            pl.BlockSpec((TMP, D_MODEL), lambda i, j: (i, 0)),
            pl.BlockSpec((None, 6, D_MODEL), lambda i, j: (_seg_of_tile(i, TMP), 0, 0)),
            pl.BlockSpec((1, D_MODEL), lambda i, j: (0, 0)),
            pl.BlockSpec((D_MODEL, TN), lambda i, j: (0, j)),
        ],
        out_specs=pl.BlockSpec((TMP, TN), lambda i, j: (i, j)),
        out_shape=jax.ShapeDtypeStruct((NTOK, Z_W), F32),
        scratch_shapes=[pltpu.VMEM((TMP, D_MODEL), BF16)],
        compiler_params=_params(("arbitrary", "arbitrary")),
        name="projection",
    )(x, mods, norm_g, w_in_p)


def _hgrn_kernel(q_ref, f_ref, v_ref, lb_ref, s0_ref, bd_ref, o_ref, sfin_ref, s_scr, *, rev):
    i = pl.program_id(0)
    blk = (pl.num_programs(0) - 1 - i) if rev else i
    nct = N_CTX // HB
    per_c = SEQ // HB
    per_l = DEC_SEQ // HB
    is_ctx = blk < nct
    pos = jnp.where(is_ctx, blk % per_c, (blk - nct) % per_l)
    last = jnp.where(is_ctx, per_c - 1, per_l - 1)
    first_pos = last if rev else 0
    final_pos = 0 if rev else last

    @pl.when(pos == first_pos)
    def _():
        s_scr[...] = s0_ref[...]

    q = q_ref[...]
    v = v_ref[...]
    lb = lb_ref[...]
    f = lb + (1.0 - lb) * jax.nn.sigmoid(f_ref[...])
    lf = jnp.log(f)
    kk = 1.0 - f

    row = lax.broadcasted_iota(jnp.int32, (HB, HG_W), 0)
    tq = lax.broadcasted_iota(jnp.int32, (HB, HB), 0)
    tk = lax.broadcasted_iota(jnp.int32, (HB, HB), 1)

    def before(x, m):
        return pltpu.roll(x, (HB - m) if rev else m, 0)

    def after(x, m):
        return pltpu.roll(x, m if rev else (HB - m), 0)

    q16 = q.astype(BF16)
    k16 = kk.astype(BF16)
    att = [jnp.where(tq == tk, _dot_nt(_hs(q16, h), _hs(k16, h)), 0.0) for h in range(HG_HEADS)]

    tot = lf
    pin = lf
    sex = jnp.zeros_like(lf)
    m = 1
    while m < HB:
        late = ((row & (2 * m - 1)) < m) if rev else ((row & (2 * m - 1)) >= m)
        qm = jnp.where(late, q * jnp.exp(pin), 0.0).astype(BF16)
        km = jnp.where(late, 0.0, kk * jnp.exp(sex)).astype(BF16)
        shift = (2 * m).bit_length() - 1
        same = (tq >> shift) == (tk >> shift)
        for h in range(HG_HEADS):
            sc = _dot_nt(qm[:, 64 * h:64 * h + 64], km[:, 64 * h:64 * h + 64])
            att[h] = att[h] + (sc if 2 * m == HB else jnp.where(same, sc, 0.0))
        tb = before(tot, m)
        ta = after(tot, m)
        pin = pin + jnp.where(late, tb, 0.0)
        sex = sex + jnp.where(late, 0.0, ta)
        tot = tot + jnp.where(late, tb, ta)
        m *= 2

    lane_head = lax.broadcasted_iota(jnp.int32, (HB, HG_W), 1) // HEAD_DIM
    o = jnp.zeros((HB, HG_W), F32)
    for h in range(HG_HEADS):
        vh = jnp.where(lane_head == h, v, 0.0).astype(BF16)
        o = o + _dot(att[h].astype(BF16), vh)

    s_t = s_scr[...]
    qt = (q * jnp.exp(pin)).astype(BF16)
    o = o + _dot_nt(qt, s_t.astype(BF16))
    o_ref[...] = o

    kt = (kk * jnp.exp(sex)).astype(BF16)
    dec = jnp.exp(tot[0:1, :])
    ds_t = _dot(v.T.astype(BF16), kt)
    s_new = s_t * dec + ds_t * bd_ref[...]
    s_scr[...] = s_new

    @pl.when(pos == final_pos)
    def _():
        sfin_ref[...] = s_new


def _hgrn(z, lbs_l, s0t, bd, rev):
    nb = NTOK // HB
    nct = N_CTX // HB
    nseq = BATCH + DEC_BATCH
    d = 1 if rev else 0

    def blk_of(i):
        return (nb - 1 - i) if rev else i

    def seq_of(i):
        b = blk_of(i)
        return jnp.where(b < nct, b // (SEQ // HB), BATCH + (b - nct) // (DEC_SEQ // HB))

    return pl.pallas_call(
        functools.partial(_hgrn_kernel, rev=rev),
        grid=(nb,),
        in_specs=[
            pl.BlockSpec((HB, HG_W), lambda i: (blk_of(i), COL_AQ // HG_W)),
            pl.BlockSpec((HB, HG_W), lambda i: (blk_of(i), (COL_AFB if rev else COL_AFF) // HG_W)),
            pl.BlockSpec((HB, HG_W), lambda i: (blk_of(i), COL_AI // HG_W)),
            pl.BlockSpec((None, 1, HG_W), lambda i: (d, 0, 0)),
            pl.BlockSpec((None, None, HG_W, HG_W), lambda i: (seq_of(i), d, 0, 0)),
            pl.BlockSpec((HG_W, HG_W), lambda i: (0, 0)),
        ],
        out_specs=[
            pl.BlockSpec((HB, HG_W), lambda i: (blk_of(i), 0)),
            pl.BlockSpec((None, HG_W, HG_W), lambda i: (seq_of(i), 0, 0)),
        ],
        out_shape=[
            jax.ShapeDtypeStruct((NTOK, HG_W), F32),
            jax.ShapeDtypeStruct((nseq, HG_W, HG_W), F32),
        ],
        scratch_shapes=[pltpu.VMEM((HG_W, HG_W), F32)],
        compiler_params=_params(("arbitrary",)),
        name="hgrn_bwd" if rev else "hgrn_fwd",
    )(z, z, z, lbs_l.reshape(2, 1, HG_W), s0t, bd)


def _pool_kernel(u_ref, w_ref, sc_ref, o_ref, *, t_len):
    u = u_ref[...]
    row = lax.broadcasted_iota(jnp.int32, (t_len, POOL_W), 0)
    grp = lax.broadcasted_iota(jnp.int32, (t_len, POOL_W), 1) // HEAD_DIM
    half = jnp.left_shift(1, grp)
    acc = jnp.zeros_like(u)
    for j in range(-8, 8):
        src = row + j
        ok = (j >= -half) & (j < half) & (src >= 0) & (src < t_len)
        shifted = u if j == 0 else pltpu.roll(u, (-j) % t_len, 0)
        acc = acc + jnp.where(ok, shifted, 0.0)
    cnt = (jnp.minimum(row + half, t_len) - jnp.maximum(row - half, 0)).astype(F32)
    y = _dot((acc / cnt - u).astype(BF16), w_ref[...]) * sc_ref[...]
    o_ref[...] = y.astype(BF16)


def _pool(z, row0, nseq, t_len, w_bd, scale):
    return pl.pallas_call(
        functools.partial(_pool_kernel, t_len=t_len),
        grid=(nseq,),
        in_specs=[
            pl.BlockSpec((t_len, POOL_W), lambda b: (row0 // t_len + b, COL_CU // POOL_W)),
            pl.BlockSpec((POOL_W, POOL_W), lambda b: (0, 0)),
            pl.BlockSpec((1, POOL_W), lambda b: (0, 0)),
        ],
        out_specs=pl.BlockSpec((t_len, POOL_W), lambda b: (b, 0)),
        out_shape=jax.ShapeDtypeStruct((nseq * t_len, POOL_W), BF16),
        compiler_params=_params(("arbitrary",)),
        name="pool",
    )(z, w_bd, scale)


def _rope(x, cos, sin):
    w = x.shape[-1]
    lane = lax.broadcasted_iota(jnp.int32, x.shape, 1)
    up = pltpu.roll(x, w - 16, 1)
    dn = pltpu.roll(x, 16, 1)
    return x * cos + jnp.where((lane & 31) < 16, up, dn) * sin


def _prep_kernel(*refs, rope):
    if rope:
        (bq, bk, bv, dq, dk, dv, gq, gk, gdq, gdk, bd, cq, sq, ck, sk,
         oq, ok_, ov, odq, odk, odv) = refs
    else:
        (bq, bk, bv, dq, dk, dv, gq, gk, gdq, gdk, bd,
         oq, ok_, ov, odq, odk, odv, ok32, odk32, ov32, odv32) = refs
    bdm = bd[...]
    q = _head_rms(bq[...], bdm, gq[...])
    k = _head_rms(bk[...], bdm[:WK_B, :WK_B], gk[...])
    qd = _head_rms(dq[...], bdm[:WD, :WD], gdq[...])
    kd = _head_rms(dk[...], bdm[:WD, :WD], gdk[...])
    if rope:
        q = _rope(q, cq[...], sq[...])
        k = _rope(k, ck[...], sk[...])
    else:
        ok32[...] = k
        odk32[...] = kd
        ov32[...] = bv[...]
        odv32[...] = dv[...]
    oq[...] = q.astype(BF16)
    ok_[...] = k.astype(BF16)
    ov[...] = bv[...].astype(BF16)
    odq[...] = qd.astype(BF16)
    odk[...] = kd.astype(BF16)
    odv[...] = dv[...].astype(BF16)


def _prep(z, row0, nrows, gains, bd, rope_tabs):
    tm = 512
    nt = nrows // tm
    r0 = row0 // tm
    rope = rope_tabs is not None
    col = lambda c, w: (lambda i: (r0 + i, c // w))
    in_specs = [
        pl.BlockSpec((tm, WQ_B), col(COL_BQ, WQ_B)),
        pl.BlockSpec((tm, WK_B), col(COL_BK, WK_B)),
        pl.BlockSpec((tm, WK_B), col(COL_BV, WK_B)),
        pl.BlockSpec((tm, WD), col(COL_DQ, WD)),
        pl.BlockSpec((tm, WD), col(COL_DK, WD)),
        pl.BlockSpec((tm, WD), col(COL_DV, WD)),
        pl.BlockSpec((1, WQ_B), lambda i: (0, 0)),
        pl.BlockSpec((1, WK_B), lambda i: (0, 0)),
        pl.BlockSpec((1, WD), lambda i: (0, 0)),
        pl.BlockSpec((1, WD), lambda i: (0, 0)),
        pl.BlockSpec((WQ_B, WQ_B), lambda i: (0, 0)),
    ]
    args = [z, z, z, z, z, z, *gains, bd]
    per = DEC_SEQ // tm
    if rope:
        in_specs += [
            pl.BlockSpec((tm, WQ_B), lambda i: (i % per, 0)),
            pl.BlockSpec((tm, WQ_B), lambda i: (i % per, 0)),
            pl.BlockSpec((tm, WK_B), lambda i: (i % per, 0)),
            pl.BlockSpec((tm, WK_B), lambda i: (i % per, 0)),
        ]
        args += list(rope_tabs)
    widths = [WQ_B, WK_B, WK_B, WD, WD, WD]
    out_specs = [pl.BlockSpec((tm, w), lambda i: (i, 0)) for w in widths]
    out_shape = [jax.ShapeDtypeStruct((nrows, w), BF16) for w in widths]
    if not rope:
        out_specs += [pl.BlockSpec((tm, w), lambda i: (i, 0)) for w in (WK_B, WD, WK_B, WD)]
        out_shape += [jax.ShapeDtypeStruct((nrows, w), F32) for w in (WK_B, WD, WK_B, WD)]
    return pl.pallas_call(
        functools.partial(_prep_kernel, rope=rope),
        grid=(nt,),
        in_specs=in_specs,
        out_specs=out_specs,
        out_shape=out_shape,
        compiler_params=_params(("arbitrary",)),
        name="prep_lat" if rope else "prep_ctx",
    )(*args)


def _softmax_pv(scores, values, sink):
    m = scores[0].max(axis=-1, keepdims=True)
    for s in scores[1:]:
        m = jnp.maximum(m, s.max(axis=-1, keepdims=True))
    if sink is not None:
        m = jnp.maximum(m, sink)
    den = jnp.zeros_like(m) if sink is None else jnp.exp(sink - m)
    acc = None
    for s, v in zip(scores, values):
        p = jnp.exp(s - m)
        den = den + p.sum(axis=-1, keepdims=True)
        pv = _dot(p.astype(BF16), v)
        acc = pv if acc is None else acc + pv
    return acc / den


def _hs(x, h):
    return x[:, HEAD_DIM * h:HEAD_DIM * (h + 1)]


GQA = WIN_HEADS // WIN_KV


def _stack_group(q, sink_ref, kv, rows):
    qs = jnp.concatenate([_hs(q, GQA * kv + j) for j in range(GQA)], axis=0)
    part = lax.broadcasted_iota(jnp.int32, (GQA * rows, 1), 0) // rows
    sink = jnp.zeros((GQA * rows, 1), F32)
    for j in range(GQA):
        sink = jnp.where(part == j, sink_ref[0, GQA * kv + j], sink)
    return qs, sink


def _ctx_attn_kernel(sink_ref, q_ref, k_ref, v_ref, qd_ref, kd_ref, vd_ref, ob_ref, od_ref):
    q, k, v = q_ref[...], k_ref[...], v_ref[...]
    for kv in range(WIN_KV):
        qs, sink = _stack_group(q, sink_ref, kv, SEQ)
        s = _dot_nt(qs, _hs(k, kv)) * SCALE
        o = _softmax_pv([s], [_hs(v, kv)], sink)
        for j in range(GQA):
            h = GQA * kv + j
            ob_ref[:, HEAD_DIM * h:HEAD_DIM * (h + 1)] = o[SEQ * j:SEQ * (j + 1)].astype(BF16)
    qd, kd, vd = qd_ref[...], kd_ref[...], vd_ref[...]
    for h in range(NA_HEADS):
        s = _dot_nt(_hs(qd, h), _hs(kd, h)) * SCALE
        o = _softmax_pv([s], [_hs(vd, h)], None)
        od_ref[:, HEAD_DIM * h:HEAD_DIM * (h + 1)] = o.astype(BF16)


def _ctx_attn(sink, q, k, v, qd, kd, vd):
    blk = lambda w: pl.BlockSpec((SEQ, w), lambda b: (b, 0))
    return pl.pallas_call(
        _ctx_attn_kernel,
        grid=(BATCH,),
        in_specs=[pl.BlockSpec(memory_space=pltpu.SMEM),
                  blk(WQ_B), blk(WK_B), blk(WK_B), blk(WD), blk(WD), blk(WD)],
        out_specs=[blk(WQ_B), blk(WD)],
        out_shape=[jax.ShapeDtypeStruct((N_CTX, WQ_B), BF16), jax.ShapeDtypeStruct((N_CTX, WD), BF16)],
        compiler_params=_params(("arbitrary",)),
        name="ctx_attn",
    )(sink, q, k, v, qd, kd, vd)


WIN_SPAN = 3 * WIN


def _win_attn_kernel(sink_ref, q_ref, k_ref, v_ref, kc_ref, vc_ref, o_ref):
    qi = pl.program_id(1)
    start = pl.multiple_of(jnp.clip(qi * WIN - WIN, 0, DEC_SEQ - WIN_SPAN), WIN)
    q = q_ref[...]
    kw = k_ref[pl.ds(start, WIN_SPAN), :]
    vw = v_ref[pl.ds(start, WIN_SPAN), :]
    kc, vc = kc_ref[...], vc_ref[...]
    rows = GQA * WIN
    qpos = qi * WIN + lax.broadcasted_iota(jnp.int32, (rows, WIN_SPAN), 0) % WIN
    kpos = start + lax.broadcasted_iota(jnp.int32, (rows, WIN_SPAN), 1)
    valid = jnp.abs(qpos - kpos) <= WIN
    for kv in range(WIN_KV):
        qs, sink = _stack_group(q, sink_ref, kv, WIN)
        s_loc = jnp.where(valid, _dot_nt(qs, _hs(kw, kv)) * SCALE, NEG)
        s_ctx = _dot_nt(qs, _hs(kc, kv)) * SCALE
        o = _softmax_pv([s_loc, s_ctx], [_hs(vw, kv), _hs(vc, kv)], sink)
        for j in range(GQA):
            h = GQA * kv + j
            o_ref[:, HEAD_DIM * h:HEAD_DIM * (h + 1)] = o[WIN * j:WIN * (j + 1)].astype(BF16)


def _win_attn(sink, q, k, v, kc, vc):
    nq = DEC_SEQ // WIN
    return pl.pallas_call(
        _win_attn_kernel,
        grid=(DEC_BATCH, nq),
        in_specs=[
            pl.BlockSpec(memory_space=pltpu.SMEM),
            pl.BlockSpec((WIN, WQ_B), lambda b, i: (b * nq + i, 0)),
            pl.BlockSpec((None, DEC_SEQ, WK_B), lambda b, i: (b, 0, 0)),
            pl.BlockSpec((None, DEC_SEQ, WK_B), lambda b, i: (b, 0, 0)),
            pl.BlockSpec((None, PAST_LEN, WK_B), lambda b, i: (b, 0, 0)),
            pl.BlockSpec((None, PAST_LEN, WK_B), lambda b, i: (b, 0, 0)),
        ],
        out_specs=pl.BlockSpec((WIN, WQ_B), lambda b, i: (b * nq + i, 0)),
        out_shape=jax.ShapeDtypeStruct((N_LAT, WQ_B), BF16),
        compiler_params=_params(("arbitrary", "arbitrary")),
        name="win_attn",
    )(sink, q, k.reshape(DEC_BATCH, DEC_SEQ, WK_B), v.reshape(DEC_BATCH, DEC_SEQ, WK_B), kc, vc)


NA_ROWS = DEC_SEQ // GRID_W
NA_G = 4
NA_NG = NA_ROWS // NA_G
NA_UROWS = NA_KH + NA_G - 1
NA_UKEYS = NA_UROWS * GRID_W
NA_QROWS = NA_G * GRID_W


def _na_union_start(g):
    return jnp.clip(g * NA_G - NA_KH // 2, 0, NA_ROWS - NA_UROWS)


def _natten_kernel(q_ref, k_ref, v_ref, kc_ref, vc_ref, bias_ref, o_ref):
    start = pl.multiple_of(_na_union_start(pl.program_id(1)) * GRID_W, GRID_W)
    q = q_ref[...]
    kw = k_ref[pl.ds(start, NA_UKEYS), :]
    vw = v_ref[pl.ds(start, NA_UKEYS), :]
    kc, vc = kc_ref[...], vc_ref[...]
    for h in range(NA_HEADS):
        qh = _hs(q, h)
        s_loc = _dot_nt(qh, _hs(kw, h)) * SCALE + bias_ref[h]
        s_ctx = _dot_nt(qh, _hs(kc, h)) * SCALE
        o = _softmax_pv([s_loc, s_ctx], [_hs(vw, h), _hs(vc, h)], None)
        o_ref[:, HEAD_DIM * h:HEAD_DIM * (h + 1)] = o.astype(BF16)


def _natten(q, k, v, kc, vc, bias):
    variant = lambda g: jnp.where(g == 0, 0, jnp.where(g == NA_NG - 1, 2, 1))
    return pl.pallas_call(
        _natten_kernel,
        grid=(DEC_BATCH, NA_NG),
        in_specs=[
            pl.BlockSpec((NA_QROWS, WD), lambda b, g: (b * NA_NG + g, 0)),
            pl.BlockSpec((None, DEC_SEQ, WD), lambda b, g: (b, 0, 0)),
            pl.BlockSpec((None, DEC_SEQ, WD), lambda b, g: (b, 0, 0)),
            pl.BlockSpec((None, PAST_LEN, WD), lambda b, g: (b, 0, 0)),
            pl.BlockSpec((None, PAST_LEN, WD), lambda b, g: (b, 0, 0)),
            pl.BlockSpec((None, NA_HEADS, NA_QROWS, NA_UKEYS), lambda b, g: (variant(g), 0, 0, 0)),
        ],
        out_specs=pl.BlockSpec((NA_QROWS, WD), lambda b, g: (b * NA_NG + g, 0)),
        out_shape=jax.ShapeDtypeStruct((N_LAT, WD), BF16),
        compiler_params=_params(("arbitrary", "arbitrary")),
        name="natten",
    )(q, k.reshape(DEC_BATCH, DEC_SEQ, WD), v.reshape(DEC_BATCH, DEC_SEQ, WD), kc, vc, bias)


def _natten_bias(rpb):
    c = np.arange(GRID_W)
    cstart = np.clip(c - NA_KW // 2, 0, GRID_W - NA_KW)
    inwin = (c[None, :] >= cstart[:, None]) & (c[None, :] < cstart[:, None] + NA_KW)
    dc = np.clip(c[None, :] - c[:, None] + NA_KW - 1, 0, 2 * NA_KW - 2)
    pick = (dc[None] == np.arange(2 * NA_KW - 1)[:, None, None]).astype(np.float32)
    toep = jnp.einsum('hdj,jck->hdck', rpb.astype(F32), jnp.asarray(pick),
                      precision=lax.Precision.HIGHEST)
    toep = jnp.where(inwin[None, None], toep, NEG)
    masked = jnp.full((NA_HEADS, GRID_W, GRID_W), NEG, F32)
    variants = []
    for g in (0, 1, NA_NG - 1):
        u0 = int(np.clip(g * NA_G - NA_KH // 2, 0, NA_ROWS - NA_UROWS))
        rows = []
        for j in range(NA_G):
            r = g * NA_G + j
            w0 = int(np.clip(r - NA_KH // 2, 0, NA_ROWS - NA_KH))
            blocks = []
            for i in range(NA_UROWS):
                kr = u0 + i
                blocks.append(toep[:, kr - r + NA_KH - 1] if w0 <= kr < w0 + NA_KH else masked)
            rows.append(jnp.concatenate(blocks, axis=-1))
        variants.append(jnp.concatenate(rows, axis=-2))
    return jnp.stack(variants, axis=0)


def _merge_kernel(x_ref, m_ref, g_ref, of_ref, ob_ref, ag_ref, on_ref, bd_ref,
                  ybc_ref, ybl_ref, ycc_ref, ycl_ref, ydc_ref, ydl_ref,
                  wg_ref, bg_ref, wa_ref, wb_ref, wc_ref, wd_ref, wo_ref, o_ref):
    i = pl.program_id(0)
    is_ctx = i < N_CTX // TM
    x = x_ref[...]
    h = _prenorm(x, g_ref[...], m_ref[0:1, :], m_ref[1:2, :]).astype(BF16)
    o = of_ref[...] + ob_ref[...]
    ya = _head_rms(o, bd_ref[...], on_ref[...]) * _silu(ag_ref[...])
    yb = jnp.where(is_ctx, ybc_ref[...], ybl_ref[...])
    yc = jnp.where(is_ctx, ycc_ref[...], ycl_ref[...])
    yd = jnp.where(is_ctx, ydc_ref[...], ydl_ref[...])
    d = D_MODEL
    merged = jnp.zeros((TM, d), F32)
    branches = ((ya.astype(BF16), wa_ref), (yb, wb_ref), (yc, wc_ref), (yd, wd_ref))
    for b, (y, w_ref) in enumerate(branches):
        gate = jax.nn.sigmoid(_dot(h, wg_ref[:, d * b:d * (b + 1)]) + bg_ref[:, d * b:d * (b + 1)])
        merged = merged + gate * _dot(y, w_ref[...])
    o_ref[...] = x + m_ref[2:3, :] * _dot(merged.astype(BF16), wo_ref[...])


def _merge(x, mods, norm_g, o_f, o_b, z, onorm, bd, yb_c, yb_l, yc_c, yc_l, yd_c, yd_l,
           w_gate, b_gate, wa, wb, wc, wd, wo):
    nt = NTOK // TM
    nct = N_CTX // TM
    cspec = lambda w: pl.BlockSpec((TM, w), lambda i: (jnp.minimum(i, nct - 1), 0))
    lspec = lambda w: pl.BlockSpec((TM, w), lambda i: (jnp.maximum(i - nct, 0), 0))
    full = lambda a: pl.BlockSpec(a.shape, lambda i: (0, 0))
    return pl.pallas_call(
        _merge_kernel,
        grid=(nt,),
        in_specs=[
            pl.BlockSpec((TM, D_MODEL), lambda i: (i, 0)),
            pl.BlockSpec((None, 6, D_MODEL), lambda i: (_seg_of_tile(i, TM), 0, 0)),
            full(norm_g),
            pl.BlockSpec((TM, HG_W), lambda i: (i, 0)),
            pl.BlockSpec((TM, HG_W), lambda i: (i, 0)),
            pl.BlockSpec((TM, HG_W), lambda i: (i, COL_AG // HG_W)),
            full(onorm), full(bd),
            cspec(WQ_B), lspec(WQ_B), cspec(POOL_W), lspec(POOL_W), cspec(WD), lspec(WD),
            full(w_gate), full(b_gate), full(wa), full(wb), full(wc), full(wd), full(wo),
        ],
        out_specs=pl.BlockSpec((TM, D_MODEL), lambda i: (i, 0)),
        out_shape=jax.ShapeDtypeStruct((NTOK, D_MODEL), F32),
        compiler_params=_params(("arbitrary",)),
        name="merge",
    )(x, mods, norm_g, o_f, o_b, z, onorm, bd, yb_c, yb_l, yc_c, yc_l, yd_c, yd_l,
      w_gate, b_gate, wa, wb, wc, wd, wo)


E_PAD = 128
TR = 256
RT = 512
SLAB = 16
NT_R = NTOK // TR
S_LOC = 1536
S_MAX = NTOK * TOP_K + NT_R * N_EXPERTS * (SLAB - 1) + N_EXPERTS * (RT - SLAB)
N_XT = S_MAX // RT
XW = D_MODEL + 2 * E_PAD
assert S_LOC >= TR * TOP_K + N_EXPERTS * (SLAB - 1) and S_LOC % 128 == 0 and S_MAX % RT == 0


def _router_kernel(x_ref, m_ref, g_ref, wr_ref, br_ref, wsg_ref, wsu_ref, wsd_ref, tri_ref, ut_ref,
                   xa_ref, sl_ref, slt_ref, cnt_ref, xs_ref):
    lane = lax.broadcasted_iota(jnp.int32, (TR, E_PAD), 1)
    x = x_ref[...]
    ms = jnp.mean(x * x, axis=-1, keepdims=True)
    y = x * lax.rsqrt(ms + EPS) * g_ref[...]
    h = y * (1.0 + m_ref[4:5, :]) + m_ref[3:4, :]
    h_hi = h.astype(BF16)
    h_lo = (h - h_hi.astype(F32)).astype(BF16)
    w = wr_ref[...]
    w_hi = w.astype(BF16)
    w_lo = (w - w_hi.astype(F32)).astype(BF16)
    logits = _dot(h_hi, w_hi) + _dot(h_hi, w_lo) + _dot(h_lo, w_hi)
    scores = jax.nn.sigmoid(logits)
    sel = jnp.where(lane < N_EXPERTS, scores + br_ref[...], -jnp.inf)
    picked = jnp.zeros((TR, E_PAD), F32)
    hot = jnp.zeros((TR, E_PAD), F32)
    idxs = []
    for _ in range(TOP_K):
        mx = sel.max(axis=-1, keepdims=True)
        idx = jnp.min(jnp.where(sel == mx, lane, E_PAD), axis=-1, keepdims=True)
        hit = lane == idx
        picked = jnp.where(hit, scores, picked)
        hot = jnp.where(hit, 1.0, hot)
        sel = jnp.where(hit, -jnp.inf, sel)
        idxs.append(idx)
    wts = ROUTED_SCALE * picked / picked.sum(axis=-1, keepdims=True)

    cnt = hot.sum(axis=0, keepdims=True)
    pad = jnp.floor((cnt + (SLAB - 1.0)) * (1.0 / SLAB)) * SLAB
    loc = _dot(jnp.broadcast_to(pad, (8, E_PAD)).astype(BF16), ut_ref[...])[0:1, :]
    rank = _dot(tri_ref[...], hot.astype(BF16))
    slotmat = loc + rank
    sl = jnp.zeros((TR, E_PAD), F32)
    for k in range(TOP_K):
        s_k = jnp.sum(jnp.where(lane == idxs[k], slotmat, 0.0), axis=-1, keepdims=True)
        sl = jnp.where(lane == k, s_k, sl)
    sl_ref[...] = sl
    slt_ref[...] = sl.T[0:8, :]
    cnt_ref[...] = pad

    w16 = wts.astype(BF16)
    xa_ref[:, 0:D_MODEL] = h_hi
    xa_ref[:, D_MODEL:D_MODEL + E_PAD] = w16
    xa_ref[:, D_MODEL + E_PAD:XW] = (wts - w16.astype(F32)).astype(BF16)

    a = _silu(_dot(h_hi, wsg_ref[...])) * _dot(h_hi, wsu_ref[...])
    xs_ref[...] = x + m_ref[5:6, :] * _dot(a.astype(BF16), wsd_ref[...])


def _router(x, mods, norm_g, w_router, b_router, wsg, wsu, wsd, tri, ut):
    full = lambda a: pl.BlockSpec(a.shape, lambda i: (0,) * a.ndim)
    return pl.pallas_call(
        _router_kernel,
        grid=(NT_R,),
        in_specs=[
            pl.BlockSpec((TR, D_MODEL), lambda i: (i, 0)),
            pl.BlockSpec((None, 6, D_MODEL), lambda i: (_seg_of_tile(i, TR), 0, 0)),
            full(norm_g), full(w_router), full(b_router), full(wsg), full(wsu), full(wsd), full(tri), full(ut),
        ],
        out_specs=[
            pl.BlockSpec((TR, XW), lambda i: (i, 0)),
            pl.BlockSpec((TR, E_PAD), lambda i: (i, 0)),
            pl.BlockSpec((None, 8, TR), lambda i: (i, 0, 0)),
            pl.BlockSpec((None, 1, E_PAD), lambda i: (i, 0, 0)),
            pl.BlockSpec((TR, D_MODEL), lambda i: (i, 0)),
        ],
        out_shape=[
            jax.ShapeDtypeStruct((NTOK, XW), BF16),
            jax.ShapeDtypeStruct((NTOK, E_PAD), F32),
            jax.ShapeDtypeStruct((NT_R, 8, TR), F32),
            jax.ShapeDtypeStruct((NT_R, 1, E_PAD), F32),
            jax.ShapeDtypeStruct((NTOK, D_MODEL), F32),
        ],
        compiler_params=_params(("arbitrary",)),
        name="router",
    )(x, mods, norm_g, w_router, b_router, wsg, wsu, wsd, tri, ut)


def _slab_copy(src, src_row, dst, dst_row, sem):
    hint = lambda r: r if isinstance(r, int) else pl.multiple_of(r, SLAB)
    return pltpu.make_async_copy(src.at[pl.ds(hint(src_row), SLAB), :],
                                 dst.at[pl.ds(hint(dst_row), SLAB), :], sem)


NSL = S_LOC // SLAB


def _for_each_slab(grow_ref, ns_ref, t, fn):
    def per_slab(n, c):
        fn(n * SLAB, grow_ref[t * NSL + n])
        return c

    lax.fori_loop(0, ns_ref[t], per_slab, 0)


def _compact_kernel(grow_ref, ns_ref, nd_ref, toff_ref, tn_ref, nu_ref,
                    xa_ref, slt_ref, xs_hbm, xc_scr, zero_scr, sem):
    t = pl.program_id(0)
    slot = t % 2

    def wait_n(n, s):
        def body(_, c):
            _slab_copy(xc_scr.at[s], 0, xs_hbm, 0, sem.at[s]).wait()
            return c
        lax.fori_loop(0, n, body, 0)

    def unused_tile_copy(j):
        row = pl.multiple_of((nu_ref[0] + j) * RT, RT)
        return pltpu.make_async_copy(zero_scr, xs_hbm.at[pl.ds(row, RT), :], sem.at[2])

    @pl.when(t == 0)
    def _():
        zero_scr[...] = jnp.zeros_like(zero_scr)

        def body(j, c):
            unused_tile_copy(j).start()
            return c
        lax.fori_loop(0, N_XT - nu_ref[0], body, 0)

    @pl.when(t >= 2)
    def _():
        wait_n(nd_ref[t - 2], slot)

    row = lax.broadcasted_iota(jnp.int32, (S_LOC, TR), 0)
    slt = slt_ref[...].astype(jnp.int32)
    hit = row == slt[0:1, :]
    for k in range(1, TOP_K):
        hit = hit | (row == slt[k:k + 1, :])
    onehot = jnp.where(hit, 1.0, 0.0).astype(BF16)
    xc_scr[slot] = _dot(onehot, xa_ref[...]).astype(BF16)

    _for_each_slab(grow_ref, ns_ref, t,
                   lambda lr, gr: _slab_copy(xc_scr.at[slot], lr, xs_hbm, gr, sem.at[slot]).start())

    @pl.when(t < N_EXPERTS)
    def _():
        def body(s, c):
            _slab_copy(zero_scr, 0, xs_hbm, toff_ref[t] + s * SLAB, sem.at[slot]).start()
            return c
        lax.fori_loop(0, tn_ref[t], body, 0)

    @pl.when(t == NT_R - 1)
    def _():
        wait_n(nd_ref[t], slot)
        wait_n(nd_ref[t - 1], 1 - slot)

        def body(j, c):
            unused_tile_copy(j).wait()
            return c
        lax.fori_loop(0, N_XT - nu_ref[0], body, 0)


def _compact(meta, xa, slt):
    grid_spec = pltpu.PrefetchScalarGridSpec(
        num_scalar_prefetch=6,
        grid=(NT_R,),
        in_specs=[
            pl.BlockSpec((TR, XW), lambda i, *_: (i, 0)),
            pl.BlockSpec((None, 8, TR), lambda i, *_: (i, 0, 0)),
        ],
        out_specs=pl.BlockSpec(memory_space=pl.ANY),
        scratch_shapes=[pltpu.VMEM((2, S_LOC, XW), BF16), pltpu.VMEM((RT, XW), BF16),
                        pltpu.SemaphoreType.DMA((3,))],
    )
    return pl.pallas_call(
        _compact_kernel,
        grid_spec=grid_spec,
        out_shape=jax.ShapeDtypeStruct((S_MAX, XW), BF16),
        compiler_params=_params(("arbitrary",)),
        name="compact",
    )(meta['grow'], meta['ns'], meta['nd'], meta['toff'], meta['tn'], meta['nu'], xa, slt)


def _expert_kernel(te_ref, ti_ref, nu_ref, xs_ref, wg_ref, wu_ref, wd_ref, y_ref, wg_s, wu_s, wd_s):
    i = pl.program_id(0)
    e = te_ref[i]

    @pl.when((i == 0) | (e != te_ref[jnp.maximum(i - 1, 0)]))
    def _():
        wg_s[...] = wg_ref[...].astype(BF16)
        wu_s[...] = wu_ref[...].astype(BF16)
        wd_s[...] = wd_ref[...].astype(BF16)

    @pl.when(i < nu_ref[0])
    def _():
        x = xs_ref[:, 0:D_MODEL]
        gw = xs_ref[:, D_MODEL:D_MODEL + E_PAD].astype(F32) + xs_ref[:, D_MODEL + E_PAD:XW].astype(F32)
        lane = lax.broadcasted_iota(jnp.int32, (RT, E_PAD), 1)
        ge = jnp.sum(jnp.where(lane == e, gw, 0.0), axis=-1, keepdims=True)
        a = _silu(_dot(x, wg_s[...])) * _dot(x, wu_s[...])
        y_ref[...] = _dot((a * ge).astype(BF16), wd_s[...]).astype(BF16)

    @pl.when(i >= nu_ref[0])
    def _():
        y_ref[...] = jnp.zeros_like(y_ref)


def _experts(meta, layer, xs, w_eg, w_eu, w_ed):
    wspec = lambda r, c: pl.BlockSpec((None, None, r, c), lambda i, te, ti, nu: (layer, te[i], 0, 0))
    grid_spec = pltpu.PrefetchScalarGridSpec(
        num_scalar_prefetch=3,
        grid=(N_XT,),
        in_specs=[
            pl.BlockSpec((RT, XW), lambda i, te, ti, nu: (ti[i], 0)),
            wspec(D_MODEL, D_EXPERT), wspec(D_MODEL, D_EXPERT), wspec(D_EXPERT, D_MODEL),
        ],
        out_specs=pl.BlockSpec((RT, D_MODEL), lambda i, te, ti, nu: (i, 0)),
        scratch_shapes=[pltpu.VMEM((D_MODEL, D_EXPERT), BF16), pltpu.VMEM((D_MODEL, D_EXPERT), BF16),
                        pltpu.VMEM((D_EXPERT, D_MODEL), BF16)],
    )
    return pl.pallas_call(
        _expert_kernel,
        grid_spec=grid_spec,
        out_shape=jax.ShapeDtypeStruct((S_MAX, D_MODEL), BF16),
        compiler_params=_params(("arbitrary",)),
        name="experts",
    )(meta['te'], meta['ti'], meta['nu'], xs, w_eg, w_eu, w_ed)


def _combine_kernel(grow_ref, ns_ref, xs_ref, m_ref, sl_ref, y_hbm, *rest, split):
    if split:
        oc_ref, ol_ref, yc_scr, sem = rest
    else:
        o_ref, yc_scr, sem = rest
    t = pl.program_id(0)
    slot = t % 2

    def issue(tt, s):
        _for_each_slab(grow_ref, ns_ref, tt,
                       lambda lr, gr: _slab_copy(y_hbm, gr, yc_scr.at[s], lr, sem.at[s]).start())

    @pl.when(t == 0)
    def _():
        yc_scr[...] = jnp.zeros_like(yc_scr)
        issue(0, 0)

    @pl.when(t + 1 < NT_R)
    def _():
        issue(t + 1, 1 - slot)

    def wait_body(_, c):
        _slab_copy(y_hbm, 0, yc_scr.at[slot], 0, sem.at[slot]).wait()
        return c
    lax.fori_loop(0, ns_ref[t], wait_body, 0)

    col = lax.broadcasted_iota(jnp.int32, (TR, S_LOC), 1)
    sl = sl_ref[...].astype(jnp.int32)
    hit = col == sl[:, 0:1]
    for k in range(1, TOP_K):
        hit = hit | (col == sl[:, k:k + 1])
    onehot = jnp.where(hit, 1.0, 0.0).astype(BF16)
    out = xs_ref[...] + m_ref[5:6, :] * _dot(onehot, yc_scr[slot])
    if split:
        @pl.when(t < N_CTX // TR)
        def _():
            oc_ref[...] = out

        @pl.when(t >= N_CTX // TR)
        def _():
            ol_ref[...] = out
    else:
        o_ref[...] = out


def _combine(meta, xsh, mods, sl, y, split):
    nct = N_CTX // TR
    if split:
        out_specs = [pl.BlockSpec((TR, D_MODEL), lambda i, *_: (jnp.minimum(i, nct - 1), 0)),
                     pl.BlockSpec((TR, D_MODEL), lambda i, *_: (jnp.maximum(i - nct, 0), 0))]
        out_shape = [jax.ShapeDtypeStruct((N_CTX, D_MODEL), F32), jax.ShapeDtypeStruct((N_LAT, D_MODEL), F32)]
    else:
        out_specs = pl.BlockSpec((TR, D_MODEL), lambda i, *_: (i, 0))
        out_shape = jax.ShapeDtypeStruct((NTOK, D_MODEL), F32)
    grid_spec = pltpu.PrefetchScalarGridSpec(
        num_scalar_prefetch=2,
        grid=(NT_R,),
        in_specs=[
            pl.BlockSpec((TR, D_MODEL), lambda i, *_: (i, 0)),
            pl.BlockSpec((None, 6, D_MODEL), lambda i, *_: (_seg_of_tile(i, TR), 0, 0)),
            pl.BlockSpec((TR, E_PAD), lambda i, *_: (i, 0)),
            pl.BlockSpec(memory_space=pl.ANY),
        ],
        out_specs=out_specs,
        scratch_shapes=[pltpu.VMEM((2, S_LOC, D_MODEL), BF16), pltpu.SemaphoreType.DMA((2,))],
    )
    return pl.pallas_call(
        functools.partial(_combine_kernel, split=split),
        grid_spec=grid_spec,
        out_shape=out_shape,
        compiler_params=_params(("arbitrary",)),
        name="combine_split" if split else "combine",
    )(meta['grow'], meta['ns'], xsh, mods, sl, y)


def _route_meta(cnt):
    pc = cnt[:, 0, :N_EXPERTS].astype(jnp.int32)
    tot = pc.sum(axis=0)
    tot_pad = ((tot + RT - 1) // RT) * RT
    ends = jnp.cumsum(tot_pad)
    base = ends - tot_pad
    dst = base[None, :] + jnp.cumsum(pc, axis=0) - pc
    cum = jnp.cumsum(pc, axis=1)
    ns = cum[:, -1] // SLAB
    lrow = jnp.arange(NSL, dtype=jnp.int32)[None, :, None] * SLAB
    owner = jnp.minimum(jnp.sum(cum[:, None, :] <= lrow, axis=2), N_EXPERTS - 1)
    first = jnp.take_along_axis(cum - pc, owner, axis=1)
    grow = jnp.take_along_axis(dst, owner, axis=1) + lrow[:, :, 0] - first
    tn = (tot_pad - tot) // SLAB
    nd = ns + jnp.pad(tn, (0, NT_R - N_EXPERTS))
    n_used = ends[-1] // RT
    ti = jnp.minimum(jnp.arange(N_XT, dtype=jnp.int32), n_used - 1)
    te = jnp.minimum(jnp.sum(ends[None, :] <= (ti * RT)[:, None], axis=1), N_EXPERTS - 1)
    i32 = lambda a: a.astype(jnp.int32)
    return dict(grow=i32(grow.reshape(-1)), ns=i32(ns), nd=i32(nd), toff=i32(base + tot), tn=i32(tn),
                te=i32(te), ti=i32(ti), nu=i32(n_used.reshape(1)))


def _moe(layer, x, mods, norm_g, w_router, b_router, w_eg, w_eu, w_ed, wsg, wsu, wsd, tri, ut):
    xa, sl, slt, cnt, xsh = _router(x, mods, norm_g, w_router, b_router, wsg, wsu, wsd, tri, ut)
    meta = _route_meta(cnt)
    xs = _compact(meta, xa, slt)
    y = _experts(meta, layer, xs, w_eg, w_eu, w_ed)
    return _combine(meta, xsh, mods, sl, y, split=(layer == DEPTH - 1))


def _rope_tables(width):
    t = np.arange(DEC_SEQ)
    quarter = HEAD_DIM // 4
    inv = (ROPE_BASE ** (-np.arange(quarter) / quarter)).astype(np.float32)
    ang_r = (t // GRID_W).astype(np.float32)[:, None] * inv[None]
    ang_c = (t % GRID_W).astype(np.float32)[:, None] * inv[None]
    cos = np.concatenate([np.cos(ang_r), np.cos(ang_r), np.cos(ang_c), np.cos(ang_c)], axis=1)
    sin = np.concatenate([-np.sin(ang_r), np.sin(ang_r), -np.sin(ang_c), np.sin(ang_c)], axis=1)
    reps = width // HEAD_DIM
    return (jnp.asarray(np.tile(cos, (1, reps)), F32), jnp.asarray(np.tile(sin, (1, reps)), F32))


def _permute_w_in(w):
    a = w[:, 0:1280]
    bq, bk, bv = w[:, 1280:1664], w[:, 1664:1792], w[:, 1792:1920]
    cu = w[:, 1920:2176]
    d = w[:, 2176:2944]
    pad = jnp.zeros((w.shape[0], Z_W - 2944), w.dtype)
    return jnp.concatenate([a, cu, d, bq, bk, bv, pad], axis=1)


def _block_diag(blocks):
    g = blocks.shape[0]
    eye = jnp.eye(g, dtype=blocks.dtype)
    return jnp.einsum('gh,gij->gihj', eye, blocks).reshape(g * HEAD_DIM, g * HEAD_DIM)


def kernel(x_prompt, x_sample, cache_win_k, cache_win_v, cache_na_k, cache_na_v, state_hgrn, c, c_ctx, w_mod, b_mod, norm1_g, norm2_g, w_in, w_mgate, b_mgate, hg_lb, hg_onorm, win_qn, win_kn, win_sink, pool_w, pool_scale, na_qn, na_kn, na_rpb, w_branch, w_out, w_router, b_router, w_eg, w_eu, w_ed, w_sg, w_su, w_sd):
    lbp = jax.nn.softmax(hg_lb.astype(F32), axis=0)
    lbs = jnp.cumsum(lbp, axis=0) - lbp[0:1]

    cvec8 = jnp.concatenate([c_ctx[None], c, jnp.zeros((3, D_MODEL), F32)], axis=0)
    mods_all = _modulation(cvec8, w_mod, b_mod).reshape(DEPTH, 8, 6, D_MODEL)

    bd384 = jnp.asarray(_bd_ones(WQ_B), BF16)
    bd256 = bd384[:HG_W, :HG_W]
    bd256_f32 = jnp.asarray(_bd_ones(HG_W), F32)
    rope_q = _rope_tables(WQ_B)
    rope_k = _rope_tables(WK_B)
    tile = lambda g, reps: jnp.tile(g, reps)[None, :]
    tri = jnp.asarray(np.tril(np.ones((TR, TR), np.float32), -1), BF16)
    ut = jnp.asarray(np.triu(np.ones((E_PAD, E_PAD), np.float32), 1), BF16)

    x = jnp.concatenate([x_prompt.reshape(N_CTX, D_MODEL), x_sample.reshape(N_LAT, D_MODEL)], axis=0)
    new_k, new_v, new_kd, new_vd, new_s = [], [], [], [], []
    for l in range(DEPTH):
        mods = mods_all[l]
        z = _projection(x, mods, norm1_g[l][None], _permute_w_in(w_in[l]).astype(BF16))

        st_lat = _block_diag_states(state_hgrn[:, l])
        s0t = jnp.concatenate([jnp.zeros((BATCH, 2, HG_W, HG_W), F32), st_lat], axis=0)
        o_f, sfin_f = _hgrn(z, lbs[l], s0t, bd256_f32, rev=False)
        o_b, sfin_b = _hgrn(z, lbs[l], s0t, bd256_f32, rev=True)
        new_s.append(jnp.stack([_unblock_states(sfin_f[:BATCH]), _unblock_states(sfin_b[:BATCH])], axis=1))

        w_pool = _block_diag(pool_w[l]).astype(BF16)
        yc_c = _pool(z, 0, BATCH, SEQ, w_pool, pool_scale[l][None])
        yc_l = _pool(z, N_CTX, DEC_BATCH, DEC_SEQ, w_pool, pool_scale[l][None])

        gains = (tile(win_qn[l], WIN_HEADS), tile(win_kn[l], WIN_KV), tile(na_qn[l], NA_HEADS), tile(na_kn[l], NA_HEADS))
        qb_c, kb_c, vb_c, qd_c, kd_c, vd_c, kb32, kd32, vb32, vd32 = _prep(z, 0, N_CTX, gains, bd384, None)
        qb_l, kb_l, vb_l, qd_l, kd_l, vd_l = _prep(z, N_CTX, N_LAT, gains, bd384, rope_q + rope_k)
        sink = win_sink[l][None]
        yb_c, yd_c = _ctx_attn(sink, qb_c, kb_c, vb_c, qd_c, kd_c, vd_c)
        kc = cache_win_k[:, l].reshape(DEC_BATCH, PAST_LEN, WK_B).astype(BF16)
        vc = cache_win_v[:, l].reshape(DEC_BATCH, PAST_LEN, WK_B).astype(BF16)
        yb_l = _win_attn(sink, qb_l, kb_l, vb_l, kc, vc)
        kcd = cache_na_k[:, l].reshape(DEC_BATCH, PAST_LEN, WD).astype(BF16)
        vcd = cache_na_v[:, l].reshape(DEC_BATCH, PAST_LEN, WD).astype(BF16)
        yd_l = _natten(qd_l, kd_l, vd_l, kcd, vcd, _natten_bias(na_rpb[l]))

        new_k.append(kb32.reshape(BATCH, SEQ, WIN_KV, HEAD_DIM))
        new_v.append(vb32.reshape(BATCH, SEQ, WIN_KV, HEAD_DIM))
        new_kd.append(kd32.reshape(BATCH, SEQ, NA_HEADS, HEAD_DIM))
        new_vd.append(vd32.reshape(BATCH, SEQ, NA_HEADS, HEAD_DIM))

        wbr = w_branch[l].astype(BF16)
        x = _merge(x, mods, norm1_g[l][None], o_f, o_b, z, tile(hg_onorm[l], HG_HEADS), bd256,
                   yb_c, yb_l, yc_c, yc_l, yd_c, yd_l, w_mgate[l].astype(BF16), b_mgate[l][None],
                   wbr[0:256], wbr[256:640], wbr[640:896], wbr[896:1152], w_out[l].astype(BF16))

        wr = jnp.pad(w_router[l], ((0, 0), (0, E_PAD - N_EXPERTS)))
        br = jnp.pad(b_router[l], (0, E_PAD - N_EXPERTS))[None]
        x = _moe(l, x, mods, norm2_g[l][None], wr, br, w_eg, w_eu, w_ed,
                 w_sg[l].astype(BF16), w_su[l].astype(BF16), w_sd[l].astype(BF16), tri, ut)

    y_p = x[0].reshape(BATCH, SEQ, D_MODEL)
    y_s = x[1].reshape(DEC_BATCH, DEC_SEQ, D_MODEL)
    return (y_p, y_s, jnp.stack(new_k, axis=1), jnp.stack(new_v, axis=1), jnp.stack(new_kd, axis=1),
            jnp.stack(new_vd, axis=1), jnp.stack(new_s, axis=1))


def _block_diag_states(s):
    b = s.shape[0]
    st = jnp.swapaxes(s.astype(F32), -1, -2)
    eye = jnp.eye(HG_HEADS, dtype=F32)
    return jnp.einsum('gh,bdgvk->bdgvhk', eye, st).reshape(b, 2, HG_W, HG_W)


def _unblock_states(st):
    b = st.shape[0]
    s5 = st.reshape(b, HG_HEADS, HEAD_DIM, HG_HEADS, HEAD_DIM)
    diag = jnp.stack([s5[:, h, :, h, :] for h in range(HG_HEADS)], axis=1)
    return jnp.swapaxes(diag, -1, -2)
```

```python
import functools

import numpy as np
import jax
import jax.numpy as jnp
from jax import lax
from jax.experimental import pallas as pl
from jax.experimental.pallas import tpu as pltpu

F32 = jnp.float32
BF16 = jnp.bfloat16

D_MODEL = 1024
BATCH = 16
SEQ = 256
DEPTH = 2
DEC_BATCH = 4
DEC_SEQ = 2048
PAST_LEN = 256
NEG = -1e30
GRID_W = 64
HEAD_DIM = 64
SCALE = HEAD_DIM ** -0.5
ROPE_BASE = 10000.0
EPS = 1e-6
HG_HEADS = 4
HG_W = 256
WIN_HEADS = 6
WIN_KV = 2
WIN = 128
POOL_SIZES = (2, 4, 8, 16)
POOL_W = 256
NA_HEADS = 4
NA_KH = 8
NA_KW = 16
N_EXPERTS = 32
TOP_K = 4
D_EXPERT = 256
ROUTED_SCALE = 2.5

N_CTX = BATCH * SEQ
N_LAT = DEC_BATCH * DEC_SEQ
NTOK = N_CTX + N_LAT
WQ_B = WIN_HEADS * HEAD_DIM
WK_B = WIN_KV * HEAD_DIM
WD = NA_HEADS * HEAD_DIM

Z_W = 3072
COL_AQ, COL_AFF, COL_AFB, COL_AI, COL_AG, COL_CU, COL_DQ, COL_DK, COL_DV = (
    0, 256, 512, 768, 1024, 1280, 1536, 1792, 2048)
COL_BQ, COL_BK, COL_BV = 2304, 2688, 2816

TM = 512
TMP = 2048
TN = 512
HB = 128
VMEM_LIMIT = 56 * 1024 * 1024


def _params(sem, vmem=VMEM_LIMIT):
    return pltpu.CompilerParams(dimension_semantics=sem, vmem_limit_bytes=vmem)


def _seg_of_tile(i, tile):
    nct = N_CTX // tile
    per = DEC_SEQ // tile
    return jnp.where(i < nct, 0, 1 + (i - nct) // per)


def _bd_ones(w):
    idx = np.arange(w) // HEAD_DIM
    return (idx[:, None] == idx[None, :]).astype(np.float32)


def _dot(a, b):
    return jnp.dot(a, b, preferred_element_type=F32)


def _dot_nt(a, b):
    return lax.dot_general(a, b, (((1,), (1,)), ((), ())), preferred_element_type=F32)


def _split_dot(x, w_bf16):
    hi = x.astype(BF16)
    lo = (x - hi.astype(F32)).astype(BF16)
    return _dot(hi, w_bf16) + _dot(lo, w_bf16)


def _head_rms(x, bd, gain):
    ms = _split_dot(x * x, bd) * (1.0 / HEAD_DIM)
    return x * lax.rsqrt(ms + EPS) * gain


def _silu(x):
    return x * jax.nn.sigmoid(x)


def _mod_kernel(c_ref, w_ref, b_ref, o_ref):
    c = c_ref[...]
    a = _silu(c).astype(BF16)
    o_ref[...] = _dot(a, w_ref[...].astype(BF16)) + b_ref[...]


def _modulation(cvec8, w_mod, b_mod):
    n = 6 * D_MODEL
    tn = 1536
    return pl.pallas_call(
        _mod_kernel,
        grid=(DEPTH, n // tn),
        in_specs=[
            pl.BlockSpec((8, D_MODEL), lambda l, j: (0, 0)),
            pl.BlockSpec((None, D_MODEL, tn), lambda l, j: (l, 0, j)),
            pl.BlockSpec((None, 1, tn), lambda l, j: (l, 0, j)),
        ],
        out_specs=pl.BlockSpec((None, 8, tn), lambda l, j: (l, 0, j)),
        out_shape=jax.ShapeDtypeStruct((DEPTH, 8, n), F32),
        compiler_params=_params(("arbitrary", "arbitrary")),
        name="modulation",
    )(cvec8, w_mod, b_mod.reshape(DEPTH, 1, n))


def _prenorm(x, gain, shift, scale):
    ms = jnp.mean(x * x, axis=-1, keepdims=True)
    return x * lax.rsqrt(ms + EPS) * gain * (1.0 + scale) + shift


def _two_group_specs(tile, width, nargs=1):
    nct = N_CTX // tile
    if nargs == 1:
        return [pl.BlockSpec((tile, width), lambda i: (jnp.minimum(i, nct - 1), 0)),
                pl.BlockSpec((tile, width), lambda i: (jnp.maximum(i - nct, 0), 0))]
    return [pl.BlockSpec((tile, width), lambda i, j: (jnp.minimum(i, nct - 1), 0)),
            pl.BlockSpec((tile, width), lambda i, j: (jnp.maximum(i - nct, 0), 0))]


def _pick_group(tile, c_ref, l_ref):
    return jnp.where(pl.program_id(0) < N_CTX // tile, c_ref[...], l_ref[...])


def _proj_kernel(xc_ref, xl_ref, m_ref, g_ref, win_ref, z_ref, h_scr):
    @pl.when(pl.program_id(1) == 0)
    def _():
        x = _pick_group(TMP, xc_ref, xl_ref)
        h_scr[...] = _prenorm(x, g_ref[...], m_ref[0:1, :], m_ref[1:2, :]).astype(BF16)

    z_ref[...] = _dot(h_scr[...], win_ref[...])


def _projection(x_c, x_l, mods, norm_g, w_in_p):
    return pl.pallas_call(
        _proj_kernel,
        grid=(NTOK // TMP, Z_W // TN),
        in_specs=_two_group_specs(TMP, D_MODEL, nargs=2) + [
            pl.BlockSpec((None, 6, D_MODEL), lambda i, j: (_seg_of_tile(i, TMP), 0, 0)),
            pl.BlockSpec((1, D_MODEL), lambda i, j: (0, 0)),
            pl.BlockSpec((D_MODEL, TN), lambda i, j: (0, j)),
        ],
        out_specs=pl.BlockSpec((TMP, TN), lambda i, j: (i, j)),
        out_shape=jax.ShapeDtypeStruct((NTOK, Z_W), F32),
        scratch_shapes=[pltpu.VMEM((TMP, D_MODEL), BF16)],
        compiler_params=_params(("arbitrary", "arbitrary")),
        name="projection",
    )(x_c, x_l, mods, norm_g, w_in_p)


def _hgrn_kernel(qf_ref, ff_ref, vf_ref, qb_ref, fb_ref, vb_ref, lb_ref, s0f_ref, s0b_ref, bd_ref,
                 of_ref, ob_ref, sff_ref, sfb_ref, sf_scr, sb_scr):
    _hgrn_direction(False, qf_ref, ff_ref, vf_ref, lb_ref[0], s0f_ref, bd_ref, of_ref, sff_ref, sf_scr)
    _hgrn_direction(True, qb_ref, fb_ref, vb_ref, lb_ref[1], s0b_ref, bd_ref, ob_ref, sfb_ref, sb_scr)


def _hgrn_direction(rev, q_ref, f_ref, v_ref, lb, s0_ref, bd_ref, o_ref, sfin_ref, s_scr):
    i = pl.program_id(0)
    blk = (pl.num_programs(0) - 1 - i) if rev else i
    nct = N_CTX // HB
    per_c = SEQ // HB
    per_l = DEC_SEQ // HB
    is_ctx = blk < nct
    pos = jnp.where(is_ctx, blk % per_c, (blk - nct) % per_l)
    last = jnp.where(is_ctx, per_c - 1, per_l - 1)
    first_pos = last if rev else 0
    final_pos = 0 if rev else last

    @pl.when(pos == first_pos)
    def _():
        s_scr[...] = s0_ref[...]

    q = q_ref[...]
    v = v_ref[...]
    f = lb + (1.0 - lb) * jax.nn.sigmoid(f_ref[...])
    lf = jnp.log(f)
    kk = 1.0 - f

    row = lax.broadcasted_iota(jnp.int32, (HB, HG_W), 0)
    tq = lax.broadcasted_iota(jnp.int32, (HB, HB), 0)
    tk = lax.broadcasted_iota(jnp.int32, (HB, HB), 1)

    def before(x, m):
        return pltpu.roll(x, (HB - m) if rev else m, 0)

    def after(x, m):
        return pltpu.roll(x, m if rev else (HB - m), 0)

    q16 = q.astype(BF16)
    k16 = kk.astype(BF16)
    att = [jnp.where(tq == tk, _dot_nt(_hs(q16, h), _hs(k16, h)), 0.0) for h in range(HG_HEADS)]

    tot = lf
    pin = lf
    sex = jnp.zeros_like(lf)
    m = 1
    while m < HB:
        late = ((row & (2 * m - 1)) < m) if rev else ((row & (2 * m - 1)) >= m)
        qm = jnp.where(late, q * jnp.exp(pin), 0.0).astype(BF16)
        km = jnp.where(late, 0.0, kk * jnp.exp(sex)).astype(BF16)
        shift = (2 * m).bit_length() - 1
        same = (tq >> shift) == (tk >> shift)
        for h in range(HG_HEADS):
            sc = _dot_nt(qm[:, 64 * h:64 * h + 64], km[:, 64 * h:64 * h + 64])
            att[h] = att[h] + (sc if 2 * m == HB else jnp.where(same, sc, 0.0))
        tb = before(tot, m)
        ta = after(tot, m)
        pin = pin + jnp.where(late, tb, 0.0)
        sex = sex + jnp.where(late, 0.0, ta)
        tot = tot + jnp.where(late, tb, ta)
        m *= 2

    lane_head = lax.broadcasted_iota(jnp.int32, (HB, HG_W), 1) // HEAD_DIM
    o = jnp.zeros((HB, HG_W), F32)
    for h in range(HG_HEADS):
        vh = jnp.where(lane_head == h, v, 0.0).astype(BF16)
        o = o + _dot(att[h].astype(BF16), vh)

    s_t = s_scr[...]
    qt = (q * jnp.exp(pin)).astype(BF16)
    o = o + _dot_nt(qt, s_t.astype(BF16))
    o_ref[...] = o

    kt = (kk * jnp.exp(sex)).astype(BF16)
    dec = jnp.exp(tot[0:1, :])
    ds_t = _dot(v.T.astype(BF16), kt)
    s_new = s_t * dec + ds_t * bd_ref[...]
    s_scr[...] = s_new

    @pl.when(pos == final_pos)
    def _():
        sfin_ref[...] = s_new


def _hgrn(z, lbs_l, s0t, bd):
    nb = NTOK // HB
    nct = N_CTX // HB
    nseq = BATCH + DEC_BATCH
    rblk = lambda i: nb - 1 - i

    def seq_of(b):
        return jnp.where(b < nct, b // (SEQ // HB), BATCH + (b - nct) // (DEC_SEQ // HB))

    col = lambda blk, c: pl.BlockSpec((HB, HG_W), lambda i: (blk(i), c // HG_W))
    state = lambda blk, d: pl.BlockSpec((None, None, HG_W, HG_W), lambda i: (seq_of(blk(i)), d, 0, 0))
    fwd = lambda i: i
    return pl.pallas_call(
        _hgrn_kernel,
        grid=(nb,),
        in_specs=[
            col(fwd, COL_AQ), col(fwd, COL_AFF), col(fwd, COL_AI),
            col(rblk, COL_AQ), col(rblk, COL_AFB), col(rblk, COL_AI),
            pl.BlockSpec((2, 1, HG_W), lambda i: (0, 0, 0)),
            state(fwd, 0), state(rblk, 1),
            pl.BlockSpec((HG_W, HG_W), lambda i: (0, 0)),
        ],
        out_specs=[
            col(fwd, 0), col(rblk, 0),
            pl.BlockSpec((None, HG_W, HG_W), lambda i: (seq_of(i), 0, 0)),
            pl.BlockSpec((None, HG_W, HG_W), lambda i: (seq_of(rblk(i)), 0, 0)),
        ],
        out_shape=[
            jax.ShapeDtypeStruct((NTOK, HG_W), F32),
            jax.ShapeDtypeStruct((NTOK, HG_W), F32),
            jax.ShapeDtypeStruct((nseq, HG_W, HG_W), F32),
            jax.ShapeDtypeStruct((nseq, HG_W, HG_W), F32),
        ],
        scratch_shapes=[pltpu.VMEM((HG_W, HG_W), F32), pltpu.VMEM((HG_W, HG_W), F32)],
        compiler_params=_params(("arbitrary",)),
        name="hgrn",
    )(z, z, z, z, z, z, lbs_l.reshape(2, 1, HG_W), s0t, s0t, bd)


def _pool_kernel(u_ref, w_ref, sc_ref, o_ref, *, t_len):
    u = u_ref[...]
    row = lax.broadcasted_iota(jnp.int32, (t_len, POOL_W), 0)
    grp = lax.broadcasted_iota(jnp.int32, (t_len, POOL_W), 1) // HEAD_DIM
    half = jnp.left_shift(1, grp)
    acc = jnp.zeros_like(u)
    for j in range(-8, 8):
        src = row + j
        ok = (j >= -half) & (j < half) & (src >= 0) & (src < t_len)
        shifted = u if j == 0 else pltpu.roll(u, (-j) % t_len, 0)
        acc = acc + jnp.where(ok, shifted, 0.0)
    cnt = (jnp.minimum(row + half, t_len) - jnp.maximum(row - half, 0)).astype(F32)
    y = _dot((acc / cnt - u).astype(BF16), w_ref[...]) * sc_ref[...]
    o_ref[...] = y.astype(BF16)


def _pool(z, row0, nseq, t_len, w_bd, scale):
    return pl.pallas_call(
        functools.partial(_pool_kernel, t_len=t_len),
        grid=(nseq,),
        in_specs=[
            pl.BlockSpec((t_len, POOL_W), lambda b: (row0 // t_len + b, COL_CU // POOL_W)),
            pl.BlockSpec((POOL_W, POOL_W), lambda b: (0, 0)),
            pl.BlockSpec((1, POOL_W), lambda b: (0, 0)),
        ],
        out_specs=pl.BlockSpec((t_len, POOL_W), lambda b: (b, 0)),
        out_shape=jax.ShapeDtypeStruct((nseq * t_len, POOL_W), BF16),
        compiler_params=_params(("arbitrary",)),
        name="pool",
    )(z, w_bd, scale)


def _rope(x, cos, sin):
    w = x.shape[-1]
    lane = lax.broadcasted_iota(jnp.int32, x.shape, 1)
    up = pltpu.roll(x, w - 16, 1)
    dn = pltpu.roll(x, 16, 1)
    return x * cos + jnp.where((lane & 31) < 16, up, dn) * sin


def _prep_kernel(*refs, rope):
    if rope:
        (bq, bk, bv, dq, dk, dv, gq, gk, gdq, gdk, bd, cq, sq, ck, sk,
         oq, ok_, ov, odq, odk, odv) = refs
    else:
        (bq, bk, bv, dq, dk, dv, gq, gk, gdq, gdk, bd,
         oq, ok_, ov, odq, odk, odv, ok32, odk32, ov32, odv32) = refs
    bdm = bd[...]
    q = _head_rms(bq[...], bdm, gq[...])
    k = _head_rms(bk[...], bdm[:WK_B, :WK_B], gk[...])
    qd = _head_rms(dq[...], bdm[:WD, :WD], gdq[...])
    kd = _head_rms(dk[...], bdm[:WD, :WD], gdk[...])
    if rope:
        q = _rope(q, cq[...], sq[...])
        k = _rope(k, ck[...], sk[...])
    else:
        ok32[...] = k
        odk32[...] = kd
        ov32[...] = bv[...]
        odv32[...] = dv[...]
    oq[...] = q.astype(BF16)
    ok_[...] = k.astype(BF16)
    ov[...] = bv[...].astype(BF16)
    odq[...] = qd.astype(BF16)
    odk[...] = kd.astype(BF16)
    odv[...] = dv[...].astype(BF16)


def _prep(z, row0, nrows, gains, bd, rope_tabs):
    tm = 512
    nt = nrows // tm
    r0 = row0 // tm
    rope = rope_tabs is not None
    col = lambda c, w: (lambda i: (r0 + i, c // w))
    in_specs = [
        pl.BlockSpec((tm, WQ_B), col(COL_BQ, WQ_B)),
        pl.BlockSpec((tm, WK_B), col(COL_BK, WK_B)),
        pl.BlockSpec((tm, WK_B), col(COL_BV, WK_B)),
        pl.BlockSpec((tm, WD), col(COL_DQ, WD)),
        pl.BlockSpec((tm, WD), col(COL_DK, WD)),
        pl.BlockSpec((tm, WD), col(COL_DV, WD)),
        pl.BlockSpec((1, WQ_B), lambda i: (0, 0)),
        pl.BlockSpec((1, WK_B), lambda i: (0, 0)),
        pl.BlockSpec((1, WD), lambda i: (0, 0)),
        pl.BlockSpec((1, WD), lambda i: (0, 0)),
        pl.BlockSpec((WQ_B, WQ_B), lambda i: (0, 0)),
    ]
    args = [z, z, z, z, z, z, *gains, bd]
    per = DEC_SEQ // tm
    if rope:
        in_specs += [
            pl.BlockSpec((tm, WQ_B), lambda i: (i % per, 0)),
            pl.BlockSpec((tm, WQ_B), lambda i: (i % per, 0)),
            pl.BlockSpec((tm, WK_B), lambda i: (i % per, 0)),
            pl.BlockSpec((tm, WK_B), lambda i: (i % per, 0)),
        ]
        args += list(rope_tabs)
    widths = [WQ_B, WK_B, WK_B, WD, WD, WD]
    out_specs = [pl.BlockSpec((tm, w), lambda i: (i, 0)) for w in widths]
    out_shape = [jax.ShapeDtypeStruct((nrows, w), BF16) for w in widths]
    if not rope:
        out_specs += [pl.BlockSpec((tm, w), lambda i: (i, 0)) for w in (WK_B, WD, WK_B, WD)]
        out_shape += [jax.ShapeDtypeStruct((nrows, w), F32) for w in (WK_B, WD, WK_B, WD)]
    return pl.pallas_call(
        functools.partial(_prep_kernel, rope=rope),
        grid=(nt,),
        in_specs=in_specs,
        out_specs=out_specs,
        out_shape=out_shape,
        compiler_params=_params(("arbitrary",)),
        name="prep_lat" if rope else "prep_ctx",
    )(*args)


def _softmax_pv(scores, values, sink):
    m = scores[0].max(axis=-1, keepdims=True)
    for s in scores[1:]:
        m = jnp.maximum(m, s.max(axis=-1, keepdims=True))
    if sink is not None:
        m = jnp.maximum(m, sink)
    den = jnp.zeros_like(m) if sink is None else jnp.exp(sink - m)
    acc = None
    for s, v in zip(scores, values):
        p = jnp.exp(s - m)
        den = den + p.sum(axis=-1, keepdims=True)
        pv = _dot(p.astype(BF16), v)
        acc = pv if acc is None else acc + pv
    return acc / den


def _hs(x, h):
    return x[:, HEAD_DIM * h:HEAD_DIM * (h + 1)]


GQA = WIN_HEADS // WIN_KV


def _stack_group(q, sink_ref, kv, rows):
    qs = jnp.concatenate([_hs(q, GQA * kv + j) for j in range(GQA)], axis=0)
    part = lax.broadcasted_iota(jnp.int32, (GQA * rows, 1), 0) // rows
    sink = jnp.zeros((GQA * rows, 1), F32)
    for j in range(GQA):
        sink = jnp.where(part == j, sink_ref[0, GQA * kv + j], sink)
    return qs, sink


def _ctx_attn_kernel(sink_ref, q_ref, k_ref, v_ref, qd_ref, kd_ref, vd_ref, ob_ref, od_ref):
    q, k, v = q_ref[...], k_ref[...], v_ref[...]
    for kv in range(WIN_KV):
        qs, sink = _stack_group(q, sink_ref, kv, SEQ)
        s = _dot_nt(qs, _hs(k, kv)) * SCALE
        o = _softmax_pv([s], [_hs(v, kv)], sink)
        for j in range(GQA):
            h = GQA * kv + j
            ob_ref[:, HEAD_DIM * h:HEAD_DIM * (h + 1)] = o[SEQ * j:SEQ * (j + 1)].astype(BF16)
    qd, kd, vd = qd_ref[...], kd_ref[...], vd_ref[...]
    for h in range(NA_HEADS):
        s = _dot_nt(_hs(qd, h), _hs(kd, h)) * SCALE
        o = _softmax_pv([s], [_hs(vd, h)], None)
        od_ref[:, HEAD_DIM * h:HEAD_DIM * (h + 1)] = o.astype(BF16)


def _ctx_attn(sink, q, k, v, qd, kd, vd):
    blk = lambda w: pl.BlockSpec((SEQ, w), lambda b: (b, 0))
    return pl.pallas_call(
        _ctx_attn_kernel,
        grid=(BATCH,),
        in_specs=[pl.BlockSpec(memory_space=pltpu.SMEM),
                  blk(WQ_B), blk(WK_B), blk(WK_B), blk(WD), blk(WD), blk(WD)],
        out_specs=[blk(WQ_B), blk(WD)],
        out_shape=[jax.ShapeDtypeStruct((N_CTX, WQ_B), BF16), jax.ShapeDtypeStruct((N_CTX, WD), BF16)],
        compiler_params=_params(("arbitrary",)),
        name="ctx_attn",
    )(sink, q, k, v, qd, kd, vd)


WIN_SPAN = 3 * WIN


def _win_attn_kernel(sink_ref, q_ref, k_ref, v_ref, kc_ref, vc_ref, o_ref):
    qi = pl.program_id(1)
    start = pl.multiple_of(jnp.clip(qi * WIN - WIN, 0, DEC_SEQ - WIN_SPAN), WIN)
    q = q_ref[...]
    kw = k_ref[pl.ds(start, WIN_SPAN), :]
    vw = v_ref[pl.ds(start, WIN_SPAN), :]
    kc, vc = kc_ref[...], vc_ref[...]
    rows = GQA * WIN
    qpos = qi * WIN + lax.broadcasted_iota(jnp.int32, (rows, WIN_SPAN), 0) % WIN
    kpos = start + lax.broadcasted_iota(jnp.int32, (rows, WIN_SPAN), 1)
    valid = jnp.abs(qpos - kpos) <= WIN
    for kv in range(WIN_KV):
        qs, sink = _stack_group(q, sink_ref, kv, WIN)
        s_loc = jnp.where(valid, _dot_nt(qs, _hs(kw, kv)) * SCALE, NEG)
        s_ctx = _dot_nt(qs, _hs(kc, kv)) * SCALE
        o = _softmax_pv([s_loc, s_ctx], [_hs(vw, kv), _hs(vc, kv)], sink)
        for j in range(GQA):
            h = GQA * kv + j
            o_ref[:, HEAD_DIM * h:HEAD_DIM * (h + 1)] = o[WIN * j:WIN * (j + 1)].astype(BF16)


def _win_attn(sink, q, k, v, kc, vc):
    nq = DEC_SEQ // WIN
    return pl.pallas_call(
        _win_attn_kernel,
        grid=(DEC_BATCH, nq),
        in_specs=[
            pl.BlockSpec(memory_space=pltpu.SMEM),
            pl.BlockSpec((WIN, WQ_B), lambda b, i: (b * nq + i, 0)),
            pl.BlockSpec((None, DEC_SEQ, WK_B), lambda b, i: (b, 0, 0)),
            pl.BlockSpec((None, DEC_SEQ, WK_B), lambda b, i: (b, 0, 0)),
            pl.BlockSpec((None, PAST_LEN, WK_B), lambda b, i: (b, 0, 0)),
            pl.BlockSpec((None, PAST_LEN, WK_B), lambda b, i: (b, 0, 0)),
        ],
        out_specs=pl.BlockSpec((WIN, WQ_B), lambda b, i: (b * nq + i, 0)),
        out_shape=jax.ShapeDtypeStruct((N_LAT, WQ_B), BF16),
        compiler_params=_params(("arbitrary", "arbitrary")),
        name="win_attn",
    )(sink, q, k.reshape(DEC_BATCH, DEC_SEQ, WK_B), v.reshape(DEC_BATCH, DEC_SEQ, WK_B), kc, vc)


NA_ROWS = DEC_SEQ // GRID_W
NA_G = 4
NA_NG = NA_ROWS // NA_G
NA_UROWS = NA_KH + NA_G - 1
NA_UKEYS = NA_UROWS * GRID_W
NA_QROWS = NA_G * GRID_W


def _na_union_start(g):
    return jnp.clip(g * NA_G - NA_KH // 2, 0, NA_ROWS - NA_UROWS)


def _natten_kernel(q_ref, k_ref, v_ref, kc_ref, vc_ref, bias_ref, o_ref):
    start = pl.multiple_of(_na_union_start(pl.program_id(1)) * GRID_W, GRID_W)
    q = q_ref[...]
    kw = k_ref[pl.ds(start, NA_UKEYS), :]
    vw = v_ref[pl.ds(start, NA_UKEYS), :]
    kc, vc = kc_ref[...], vc_ref[...]
    for h in range(NA_HEADS):
        qh = _hs(q, h)
        s_loc = _dot_nt(qh, _hs(kw, h)) * SCALE + bias_ref[h]
        s_ctx = _dot_nt(qh, _hs(kc, h)) * SCALE
        o = _softmax_pv([s_loc, s_ctx], [_hs(vw, h), _hs(vc, h)], None)
        o_ref[:, HEAD_DIM * h:HEAD_DIM * (h + 1)] = o.astype(BF16)


def _natten(q, k, v, kc, vc, bias):
    variant = lambda g: jnp.where(g == 0, 0, jnp.where(g == NA_NG - 1, 2, 1))
    return pl.pallas_call(
        _natten_kernel,
        grid=(DEC_BATCH, NA_NG),
        in_specs=[
            pl.BlockSpec((NA_QROWS, WD), lambda b, g: (b * NA_NG + g, 0)),
            pl.BlockSpec((None, DEC_SEQ, WD), lambda b, g: (b, 0, 0)),
            pl.BlockSpec((None, DEC_SEQ, WD), lambda b, g: (b, 0, 0)),
            pl.BlockSpec((None, PAST_LEN, WD), lambda b, g: (b, 0, 0)),
            pl.BlockSpec((None, PAST_LEN, WD), lambda b, g: (b, 0, 0)),
            pl.BlockSpec((None, NA_HEADS, NA_QROWS, NA_UKEYS), lambda b, g: (variant(g), 0, 0, 0)),
        ],
        out_specs=pl.BlockSpec((NA_QROWS, WD), lambda b, g: (b * NA_NG + g, 0)),
        out_shape=jax.ShapeDtypeStruct((N_LAT, WD), BF16),
        compiler_params=_params(("arbitrary", "arbitrary")),
        name="natten",
    )(q, k.reshape(DEC_BATCH, DEC_SEQ, WD), v.reshape(DEC_BATCH, DEC_SEQ, WD), kc, vc, bias)


def _natten_bias(rpb):
    c = np.arange(GRID_W)
    cstart = np.clip(c - NA_KW // 2, 0, GRID_W - NA_KW)
    inwin = (c[None, :] >= cstart[:, None]) & (c[None, :] < cstart[:, None] + NA_KW)
    dc = np.clip(c[None, :] - c[:, None] + NA_KW - 1, 0, 2 * NA_KW - 2)
    pick = (dc[None] == np.arange(2 * NA_KW - 1)[:, None, None]).astype(np.float32)
    toep = jnp.einsum('hdj,jck->hdck', rpb.astype(F32), jnp.asarray(pick),
                      precision=lax.Precision.HIGHEST)
    toep = jnp.where(inwin[None, None], toep, NEG)
    masked = jnp.full((NA_HEADS, GRID_W, GRID_W), NEG, F32)
    variants = []
    for g in (0, 1, NA_NG - 1):
        u0 = int(np.clip(g * NA_G - NA_KH // 2, 0, NA_ROWS - NA_UROWS))
        rows = []
        for j in range(NA_G):
            r = g * NA_G + j
            w0 = int(np.clip(r - NA_KH // 2, 0, NA_ROWS - NA_KH))
            blocks = []
            for i in range(NA_UROWS):
                kr = u0 + i
                blocks.append(toep[:, kr - r + NA_KH - 1] if w0 <= kr < w0 + NA_KH else masked)
            rows.append(jnp.concatenate(blocks, axis=-1))
        variants.append(jnp.concatenate(rows, axis=-2))
    return jnp.stack(variants, axis=0)


def _merge_kernel(xc_ref, xl_ref, m_ref, g_ref, of_ref, ob_ref, ag_ref, on_ref, bd_ref,
                  ybc_ref, ybl_ref, ycc_ref, ycl_ref, ydc_ref, ydl_ref,
                  wg_ref, bg_ref, wa_ref, wb_ref, wc_ref, wd_ref, wo_ref, o_ref):
    x = _pick_group(TM, xc_ref, xl_ref)
    h = _prenorm(x, g_ref[...], m_ref[0:1, :], m_ref[1:2, :]).astype(BF16)
    o = of_ref[...] + ob_ref[...]
    ya = _head_rms(o, bd_ref[...], on_ref[...]) * _silu(ag_ref[...])
    yb = _pick_group(TM, ybc_ref, ybl_ref)
    yc = _pick_group(TM, ycc_ref, ycl_ref)
    yd = _pick_group(TM, ydc_ref, ydl_ref)
    d = D_MODEL
    merged = jnp.zeros((TM, d), F32)
    branches = ((ya.astype(BF16), wa_ref), (yb, wb_ref), (yc, wc_ref), (yd, wd_ref))
    for b, (y, w_ref) in enumerate(branches):
        gate = jax.nn.sigmoid(_dot(h, wg_ref[:, d * b:d * (b + 1)]) + bg_ref[:, d * b:d * (b + 1)])
        merged = merged + gate * _dot(y, w_ref[...])
    o_ref[...] = x + m_ref[2:3, :] * _dot(merged.astype(BF16), wo_ref[...])


def _merge(x_c, x_l, mods, norm_g, o_f, o_b, z, onorm, bd, yb_c, yb_l, yc_c, yc_l, yd_c, yd_l,
           w_gate, b_gate, wa, wb, wc, wd, wo):
    full = lambda a: pl.BlockSpec(a.shape, lambda i: (0, 0))
    return pl.pallas_call(
        _merge_kernel,
        grid=(NTOK // TM,),
        in_specs=_two_group_specs(TM, D_MODEL) + [
            pl.BlockSpec((None, 6, D_MODEL), lambda i: (_seg_of_tile(i, TM), 0, 0)),
            full(norm_g),
            pl.BlockSpec((TM, HG_W), lambda i: (i, 0)),
            pl.BlockSpec((TM, HG_W), lambda i: (i, 0)),
            pl.BlockSpec((TM, HG_W), lambda i: (i, COL_AG // HG_W)),
            full(onorm), full(bd),
        ] + _two_group_specs(TM, WQ_B) + _two_group_specs(TM, POOL_W) + _two_group_specs(TM, WD) + [
            full(w_gate), full(b_gate), full(wa), full(wb), full(wc), full(wd), full(wo),
        ],
        out_specs=pl.BlockSpec((TM, D_MODEL), lambda i: (i, 0)),
        out_shape=jax.ShapeDtypeStruct((NTOK, D_MODEL), F32),
        compiler_params=_params(("arbitrary",)),
        name="merge",
    )(x_c, x_l, mods, norm_g, o_f, o_b, z, onorm, bd, yb_c, yb_l, yc_c, yc_l, yd_c, yd_l,
      w_gate, b_gate, wa, wb, wc, wd, wo)


E_PAD = 128
TR = 256
RT = 512
SLAB = 16
NT_R = NTOK // TR
S_LOC = 1536
S_MAX = NTOK * TOP_K + NT_R * N_EXPERTS * (SLAB - 1) + N_EXPERTS * (RT - SLAB)
N_XT = S_MAX // RT
XW = D_MODEL + 2 * E_PAD
assert S_LOC >= TR * TOP_K + N_EXPERTS * (SLAB - 1) and S_LOC % 128 == 0 and S_MAX % RT == 0


def _router_kernel(x_ref, m_ref, g_ref, wr_ref, br_ref, wsg_ref, wsu_ref, wsd_ref, tri_ref, ut_ref,
                   xa_ref, sl_ref, slt_ref, cnt_ref, xs_ref):
    lane = lax.broadcasted_iota(jnp.int32, (TR, E_PAD), 1)
    x = x_ref[...]
    ms = jnp.mean(x * x, axis=-1, keepdims=True)
    y = x * lax.rsqrt(ms + EPS) * g_ref[...]
    h = y * (1.0 + m_ref[4:5, :]) + m_ref[3:4, :]
    h_hi = h.astype(BF16)
    h_lo = (h - h_hi.astype(F32)).astype(BF16)
    w = wr_ref[...]
    w_hi = w.astype(BF16)
    w_lo = (w - w_hi.astype(F32)).astype(BF16)
    logits = _dot(h_hi, w_hi) + _dot(h_hi, w_lo) + _dot(h_lo, w_hi)
    scores = jax.nn.sigmoid(logits)
    sel = jnp.where(lane < N_EXPERTS, scores + br_ref[...], -jnp.inf)
    picked = jnp.zeros((TR, E_PAD), F32)
    hot = jnp.zeros((TR, E_PAD), F32)
    idxs = []
    for _ in range(TOP_K):
        mx = sel.max(axis=-1, keepdims=True)
        idx = jnp.min(jnp.where(sel == mx, lane, E_PAD), axis=-1, keepdims=True)
        hit = lane == idx
        picked = jnp.where(hit, scores, picked)
        hot = jnp.where(hit, 1.0, hot)
        sel = jnp.where(hit, -jnp.inf, sel)
        idxs.append(idx)
    wts = ROUTED_SCALE * picked / picked.sum(axis=-1, keepdims=True)

    cnt = hot.sum(axis=0, keepdims=True)
    pad = jnp.floor((cnt + (SLAB - 1.0)) * (1.0 / SLAB)) * SLAB
    loc = _dot(jnp.broadcast_to(pad, (8, E_PAD)).astype(BF16), ut_ref[...])[0:1, :]
    rank = _dot(tri_ref[...], hot.astype(BF16))
    slotmat = loc + rank
    sl = jnp.zeros((TR, E_PAD), F32)
    for k in range(TOP_K):
        s_k = jnp.sum(jnp.where(lane == idxs[k], slotmat, 0.0), axis=-1, keepdims=True)
        sl = jnp.where(lane == k, s_k, sl)
    sl_ref[...] = sl
    slt_ref[...] = sl.T[0:8, :]
    cnt_ref[...] = pad

    w16 = wts.astype(BF16)
    xa_ref[:, 0:D_MODEL] = h_hi
    xa_ref[:, D_MODEL:D_MODEL + E_PAD] = w16
    xa_ref[:, D_MODEL + E_PAD:XW] = (wts - w16.astype(F32)).astype(BF16)

    a = _silu(_dot(h_hi, wsg_ref[...])) * _dot(h_hi, wsu_ref[...])
    xs_ref[...] = x + m_ref[5:6, :] * _dot(a.astype(BF16), wsd_ref[...])


def _router(x, mods, norm_g, w_router, b_router, wsg, wsu, wsd, tri, ut):
    full = lambda a: pl.BlockSpec(a.shape, lambda i: (0,) * a.ndim)
    return pl.pallas_call(
        _router_kernel,
        grid=(NT_R,),
        in_specs=[
            pl.BlockSpec((TR, D_MODEL), lambda i: (i, 0)),
            pl.BlockSpec((None, 6, D_MODEL), lambda i: (_seg_of_tile(i, TR), 0, 0)),
            full(norm_g), full(w_router), full(b_router), full(wsg), full(wsu), full(wsd), full(tri), full(ut),
        ],
        out_specs=[
            pl.BlockSpec((TR, XW), lambda i: (i, 0)),
            pl.BlockSpec((TR, E_PAD), lambda i: (i, 0)),
            pl.BlockSpec((None, 8, TR), lambda i: (i, 0, 0)),
            pl.BlockSpec((None, 1, E_PAD), lambda i: (i, 0, 0)),
            pl.BlockSpec((TR, D_MODEL), lambda i: (i, 0)),
        ],
        out_shape=[
            jax.ShapeDtypeStruct((NTOK, XW), BF16),
            jax.ShapeDtypeStruct((NTOK, E_PAD), F32),
            jax.ShapeDtypeStruct((NT_R, 8, TR), F32),
            jax.ShapeDtypeStruct((NT_R, 1, E_PAD), F32),
            jax.ShapeDtypeStruct((NTOK, D_MODEL), F32),
        ],
        compiler_params=_params(("arbitrary",)),
        name="router",
    )(x, mods, norm_g, w_router, b_router, wsg, wsu, wsd, tri, ut)


def _slab_copy(src, src_row, dst, dst_row, sem):
    hint = lambda r: r if isinstance(r, int) else pl.multiple_of(r, SLAB)
    return pltpu.make_async_copy(src.at[pl.ds(hint(src_row), SLAB), :],
                                 dst.at[pl.ds(hint(dst_row), SLAB), :], sem)


NSL = S_LOC // SLAB


def _for_each_slab(grow_ref, ns_ref, t, fn):
    def per_slab(n, c):
        fn(n * SLAB, grow_ref[t * NSL + n])
        return c

    lax.fori_loop(0, ns_ref[t], per_slab, 0)


def _compact_kernel(grow_ref, ns_ref, nd_ref, toff_ref, tn_ref, nu_ref,
                    xa_ref, slt_ref, xs_hbm, xc_scr, zero_scr, sem):
    t = pl.program_id(0)
    slot = t % 2

    def wait_n(n, s):
        def body(_, c):
            _slab_copy(xc_scr.at[s], 0, xs_hbm, 0, sem.at[s]).wait()
            return c
        lax.fori_loop(0, n, body, 0)

    def unused_tile_copy(j):
        row = pl.multiple_of((nu_ref[0] + j) * RT, RT)
        return pltpu.make_async_copy(zero_scr, xs_hbm.at[pl.ds(row, RT), :], sem.at[2])

    @pl.when(t == 0)
    def _():
        zero_scr[...] = jnp.zeros_like(zero_scr)

        def body(j, c):
            unused_tile_copy(j).start()
            return c
        lax.fori_loop(0, N_XT - nu_ref[0], body, 0)

    @pl.when(t >= 2)
    def _():
        wait_n(nd_ref[t - 2], slot)

    row = lax.broadcasted_iota(jnp.int32, (S_LOC, TR), 0)
    slt = slt_ref[...].astype(jnp.int32)
    hit = row == slt[0:1, :]
    for k in range(1, TOP_K):
        hit = hit | (row == slt[k:k + 1, :])
    onehot = jnp.where(hit, 1.0, 0.0).astype(BF16)
    xc_scr[slot] = _dot(onehot, xa_ref[...]).astype(BF16)

    _for_each_slab(grow_ref, ns_ref, t,
                   lambda lr, gr: _slab_copy(xc_scr.at[slot], lr, xs_hbm, gr, sem.at[slot]).start())

    @pl.when(t < N_EXPERTS)
    def _():
        def body(s, c):
            _slab_copy(zero_scr, 0, xs_hbm, toff_ref[t] + s * SLAB, sem.at[slot]).start()
            return c
        lax.fori_loop(0, tn_ref[t], body, 0)

    @pl.when(t == NT_R - 1)
    def _():
        wait_n(nd_ref[t], slot)
        wait_n(nd_ref[t - 1], 1 - slot)

        def body(j, c):
            unused_tile_copy(j).wait()
            return c
        lax.fori_loop(0, N_XT - nu_ref[0], body, 0)


def _compact(meta, xa, slt):
    grid_spec = pltpu.PrefetchScalarGridSpec(
        num_scalar_prefetch=6,
        grid=(NT_R,),
        in_specs=[
            pl.BlockSpec((TR, XW), lambda i, *_: (i, 0)),
            pl.BlockSpec((None, 8, TR), lambda i, *_: (i, 0, 0)),
        ],
        out_specs=pl.BlockSpec(memory_space=pl.ANY),
        scratch_shapes=[pltpu.VMEM((2, S_LOC, XW), BF16), pltpu.VMEM((RT, XW), BF16),
                        pltpu.SemaphoreType.DMA((3,))],
    )
    return pl.pallas_call(
        _compact_kernel,
        grid_spec=grid_spec,
        out_shape=jax.ShapeDtypeStruct((S_MAX, XW), BF16),
        compiler_params=_params(("arbitrary",)),
        name="compact",
    )(meta['grow'], meta['ns'], meta['nd'], meta['toff'], meta['tn'], meta['nu'], xa, slt)


def _expert_kernel(te_ref, ti_ref, nu_ref, xs_ref, wg_ref, wu_ref, wd_ref, y_ref, wgu_s, wd_s):
    i = pl.program_id(0)
    e = te_ref[i]

    @pl.when((i == 0) | (e != te_ref[jnp.maximum(i - 1, 0)]))
    def _():
        wgu_s[:, 0:D_EXPERT] = wg_ref[...].astype(BF16)
        wgu_s[:, D_EXPERT:2 * D_EXPERT] = wu_ref[...].astype(BF16)
        wd_s[...] = wd_ref[...].astype(BF16)

    @pl.when(i < nu_ref[0])
    def _():
        x = xs_ref[:, 0:D_MODEL]
        gw = xs_ref[:, D_MODEL:D_MODEL + E_PAD].astype(F32) + xs_ref[:, D_MODEL + E_PAD:XW].astype(F32)
        lane = lax.broadcasted_iota(jnp.int32, (RT, E_PAD), 1)
        ge = jnp.sum(jnp.where(lane == e, gw, 0.0), axis=-1, keepdims=True)
        gu = _dot(x, wgu_s[...])
        a = _silu(gu[:, 0:D_EXPERT]) * gu[:, D_EXPERT:2 * D_EXPERT]
        y_ref[...] = _dot((a * ge).astype(BF16), wd_s[...]).astype(BF16)

    @pl.when(i >= nu_ref[0])
    def _():
        y_ref[...] = jnp.zeros_like(y_ref)


def _experts(meta, layer, xs, w_eg, w_eu, w_ed):
    wspec = lambda r, c: pl.BlockSpec((None, None, r, c), lambda i, te, ti, nu: (layer, te[i], 0, 0))
    grid_spec = pltpu.PrefetchScalarGridSpec(
        num_scalar_prefetch=3,
        grid=(N_XT,),
        in_specs=[
            pl.BlockSpec((RT, XW), lambda i, te, ti, nu: (ti[i], 0)),
            wspec(D_MODEL, D_EXPERT), wspec(D_MODEL, D_EXPERT), wspec(D_EXPERT, D_MODEL),
        ],
        out_specs=pl.BlockSpec((RT, D_MODEL), lambda i, te, ti, nu: (i, 0)),
        scratch_shapes=[pltpu.VMEM((D_MODEL, 2 * D_EXPERT), BF16), pltpu.VMEM((D_EXPERT, D_MODEL), BF16)],
    )
    return pl.pallas_call(
        _expert_kernel,
        grid_spec=grid_spec,
        out_shape=jax.ShapeDtypeStruct((S_MAX, D_MODEL), BF16),
        compiler_params=_params(("arbitrary",)),
        name="experts",
    )(meta['te'], meta['ti'], meta['nu'], xs, w_eg, w_eu, w_ed)


def _combine_kernel(grow_ref, ns_ref, xs_ref, m_ref, sl_ref, y_hbm, oc_ref, ol_ref, yc_scr, sem):
    t = pl.program_id(0)
    slot = t % 2

    def issue(tt, s):
        _for_each_slab(grow_ref, ns_ref, tt,
                       lambda lr, gr: _slab_copy(y_hbm, gr, yc_scr.at[s], lr, sem.at[s]).start())

    @pl.when(t == 0)
    def _():
        yc_scr[...] = jnp.zeros_like(yc_scr)
        issue(0, 0)

    @pl.when(t + 1 < NT_R)
    def _():
        issue(t + 1, 1 - slot)

    def wait_body(_, c):
        _slab_copy(y_hbm, 0, yc_scr.at[slot], 0, sem.at[slot]).wait()
        return c
    lax.fori_loop(0, ns_ref[t], wait_body, 0)

    col = lax.broadcasted_iota(jnp.int32, (TR, S_LOC), 1)
    sl = sl_ref[...].astype(jnp.int32)
    hit = col == sl[:, 0:1]
    for k in range(1, TOP_K):
        hit = hit | (col == sl[:, k:k + 1])
    onehot = jnp.where(hit, 1.0, 0.0).astype(BF16)
    out = xs_ref[...] + m_ref[5:6, :] * _dot(onehot, yc_scr[slot])

    @pl.when(t < N_CTX // TR)
    def _():
        oc_ref[...] = out

    @pl.when(t >= N_CTX // TR)
    def _():
        ol_ref[...] = out


def _combine(meta, xsh, mods, sl, y):
    nct = N_CTX // TR
    out_specs = [pl.BlockSpec((TR, D_MODEL), lambda i, *_: (jnp.minimum(i, nct - 1), 0)),
                 pl.BlockSpec((TR, D_MODEL), lambda i, *_: (jnp.maximum(i - nct, 0), 0))]
    out_shape = [jax.ShapeDtypeStruct((N_CTX, D_MODEL), F32), jax.ShapeDtypeStruct((N_LAT, D_MODEL), F32)]
    grid_spec = pltpu.PrefetchScalarGridSpec(
        num_scalar_prefetch=2,
        grid=(NT_R,),
        in_specs=[
            pl.BlockSpec((TR, D_MODEL), lambda i, *_: (i, 0)),
            pl.BlockSpec((None, 6, D_MODEL), lambda i, *_: (_seg_of_tile(i, TR), 0, 0)),
            pl.BlockSpec((TR, E_PAD), lambda i, *_: (i, 0)),
            pl.BlockSpec(memory_space=pl.ANY),
        ],
        out_specs=out_specs,
        scratch_shapes=[pltpu.VMEM((2, S_LOC, D_MODEL), BF16), pltpu.SemaphoreType.DMA((2,))],
    )
    return pl.pallas_call(
        _combine_kernel,
        grid_spec=grid_spec,
        out_shape=out_shape,
        compiler_params=_params(("arbitrary",)),
        name="combine",
    )(meta['grow'], meta['ns'], xsh, mods, sl, y)


def _route_meta(cnt):
    pc = cnt[:, 0, :N_EXPERTS].astype(jnp.int32)
    tot = pc.sum(axis=0)
    tot_pad = ((tot + RT - 1) // RT) * RT
    ends = jnp.cumsum(tot_pad)
    base = ends - tot_pad
    dst = base[None, :] + jnp.cumsum(pc, axis=0) - pc
    cum = jnp.cumsum(pc, axis=1)
    ns = cum[:, -1] // SLAB
    lrow = jnp.arange(NSL, dtype=jnp.int32)[None, :, None] * SLAB
    owner = jnp.minimum(jnp.sum(cum[:, None, :] <= lrow, axis=2), N_EXPERTS - 1)
    mine = owner[:, :, None] == jnp.arange(N_EXPERTS, dtype=jnp.int32)[None, None, :]
    grow = jnp.sum(jnp.where(mine, (dst - cum + pc)[:, None, :], 0), axis=2) + lrow[:, :, 0]
    tn = (tot_pad - tot) // SLAB
    nd = ns + jnp.pad(tn, (0, NT_R - N_EXPERTS))
    n_used = ends[-1] // RT
    ti = jnp.minimum(jnp.arange(N_XT, dtype=jnp.int32), n_used - 1)
    te = jnp.minimum(jnp.sum(ends[None, :] <= (ti * RT)[:, None], axis=1), N_EXPERTS - 1)
    i32 = lambda a: a.astype(jnp.int32)
    return dict(grow=i32(grow.reshape(-1)), ns=i32(ns), nd=i32(nd), toff=i32(base + tot), tn=i32(tn),
                te=i32(te), ti=i32(ti), nu=i32(n_used.reshape(1)))


def _moe(layer, x, mods, norm_g, w_router, b_router, w_eg, w_eu, w_ed, wsg, wsu, wsd, tri, ut):
    xa, sl, slt, cnt, xsh = _router(x, mods, norm_g, w_router, b_router, wsg, wsu, wsd, tri, ut)
    meta = _route_meta(cnt)
    xs = _compact(meta, xa, slt)
    y = _experts(meta, layer, xs, w_eg, w_eu, w_ed)
    return _combine(meta, xsh, mods, sl, y)


def _rope_tables(width):
    t = np.arange(DEC_SEQ)
    quarter = HEAD_DIM // 4
    inv = (ROPE_BASE ** (-np.arange(quarter) / quarter)).astype(np.float32)
    ang_r = (t // GRID_W).astype(np.float32)[:, None] * inv[None]
    ang_c = (t % GRID_W).astype(np.float32)[:, None] * inv[None]
    cos = np.concatenate([np.cos(ang_r), np.cos(ang_r), np.cos(ang_c), np.cos(ang_c)], axis=1)
    sin = np.concatenate([-np.sin(ang_r), np.sin(ang_r), -np.sin(ang_c), np.sin(ang_c)], axis=1)
    reps = width // HEAD_DIM
    return (jnp.asarray(np.tile(cos, (1, reps)), F32), jnp.asarray(np.tile(sin, (1, reps)), F32))


def _permute_w_in(w):
    a = w[:, 0:1280]
    bq, bk, bv = w[:, 1280:1664], w[:, 1664:1792], w[:, 1792:1920]
    cu = w[:, 1920:2176]
    d = w[:, 2176:2944]
    pad = jnp.zeros((w.shape[0], Z_W - 2944), w.dtype)
    return jnp.concatenate([a, cu, d, bq, bk, bv, pad], axis=1)


def _block_diag(blocks):
    g = blocks.shape[0]
    eye = jnp.eye(g, dtype=blocks.dtype)
    return jnp.einsum('gh,gij->gihj', eye, blocks).reshape(g * HEAD_DIM, g * HEAD_DIM)


def kernel(x_prompt, x_sample, cache_win_k, cache_win_v, cache_na_k, cache_na_v, state_hgrn, c, c_ctx, w_mod, b_mod, norm1_g, norm2_g, w_in, w_mgate, b_mgate, hg_lb, hg_onorm, win_qn, win_kn, win_sink, pool_w, pool_scale, na_qn, na_kn, na_rpb, w_branch, w_out, w_router, b_router, w_eg, w_eu, w_ed, w_sg, w_su, w_sd):
    lbp = jax.nn.softmax(hg_lb.astype(F32), axis=0)
    lbs = jnp.cumsum(lbp, axis=0) - lbp[0:1]

    cvec8 = jnp.concatenate([c_ctx[None], c, jnp.zeros((3, D_MODEL), F32)], axis=0)
    mods_all = _modulation(cvec8, w_mod, b_mod).reshape(DEPTH, 8, 6, D_MODEL)

    bd384 = jnp.asarray(_bd_ones(WQ_B), BF16)
    bd256 = bd384[:HG_W, :HG_W]
    bd256_f32 = jnp.asarray(_bd_ones(HG_W), F32)
    rope_q = _rope_tables(WQ_B)
    rope_k = _rope_tables(WK_B)
    tile = lambda g, reps: jnp.tile(g, reps)[None, :]
    tri = jnp.asarray(np.tril(np.ones((TR, TR), np.float32), -1), BF16)
    ut = jnp.asarray(np.triu(np.ones((E_PAD, E_PAD), np.float32), 1), BF16)

    x_c, x_l = x_prompt.reshape(N_CTX, D_MODEL), x_sample.reshape(N_LAT, D_MODEL)
    new_k, new_v, new_kd, new_vd, new_s = [], [], [], [], []
    for l in range(DEPTH):
        mods = mods_all[l]
        z = _projection(x_c, x_l, mods, norm1_g[l][None], _permute_w_in(w_in[l]).astype(BF16))

        st_lat = _block_diag_states(state_hgrn[:, l])
        s0t = jnp.concatenate([jnp.zeros((BATCH, 2, HG_W, HG_W), F32), st_lat], axis=0)
        o_f, o_b, sfin_f, sfin_b = _hgrn(z, lbs[l], s0t, bd256_f32)
        new_s.append(jnp.stack([_unblock_states(sfin_f[:BATCH]), _unblock_states(sfin_b[:BATCH])], axis=1))

        w_pool = _block_diag(pool_w[l]).astype(BF16)
        yc_c = _pool(z, 0, BATCH, SEQ, w_pool, pool_scale[l][None])
        yc_l = _pool(z, N_CTX, DEC_BATCH, DEC_SEQ, w_pool, pool_scale[l][None])

        gains = (tile(win_qn[l], WIN_HEADS), tile(win_kn[l], WIN_KV), tile(na_qn[l], NA_HEADS), tile(na_kn[l], NA_HEADS))
        qb_c, kb_c, vb_c, qd_c, kd_c, vd_c, kb32, kd32, vb32, vd32 = _prep(z, 0, N_CTX, gains, bd384, None)
        qb_l, kb_l, vb_l, qd_l, kd_l, vd_l = _prep(z, N_CTX, N_LAT, gains, bd384, rope_q + rope_k)
        sink = win_sink[l][None]
        yb_c, yd_c = _ctx_attn(sink, qb_c, kb_c, vb_c, qd_c, kd_c, vd_c)
        kc = cache_win_k[:, l].reshape(DEC_BATCH, PAST_LEN, WK_B).astype(BF16)
        vc = cache_win_v[:, l].reshape(DEC_BATCH, PAST_LEN, WK_B).astype(BF16)
        yb_l = _win_attn(sink, qb_l, kb_l, vb_l, kc, vc)
        kcd = cache_na_k[:, l].reshape(DEC_BATCH, PAST_LEN, WD).astype(BF16)
        vcd = cache_na_v[:, l].reshape(DEC_BATCH, PAST_LEN, WD).astype(BF16)
        yd_l = _natten(qd_l, kd_l, vd_l, kcd, vcd, _natten_bias(na_rpb[l]))

        new_k.append(kb32.reshape(BATCH, SEQ, WIN_KV, HEAD_DIM))
        new_v.append(vb32.reshape(BATCH, SEQ, WIN_KV, HEAD_DIM))
        new_kd.append(kd32.reshape(BATCH, SEQ, NA_HEADS, HEAD_DIM))
        new_vd.append(vd32.reshape(BATCH, SEQ, NA_HEADS, HEAD_DIM))

        wbr = w_branch[l].astype(BF16)
        x1 = _merge(x_c, x_l, mods, norm1_g[l][None], o_f, o_b, z, tile(hg_onorm[l], HG_HEADS), bd256,
                    yb_c, yb_l, yc_c, yc_l, yd_c, yd_l, w_mgate[l].astype(BF16), b_mgate[l][None],
                    wbr[0:256], wbr[256:640], wbr[640:896], wbr[896:1152], w_out[l].astype(BF16))

        wr = jnp.pad(w_router[l], ((0, 0), (0, E_PAD - N_EXPERTS)))
        br = jnp.pad(b_router[l], (0, E_PAD - N_EXPERTS))[None]
        x_c, x_l = _moe(l, x1, mods, norm2_g[l][None], wr, br, w_eg, w_eu, w_ed,
                        w_sg[l].astype(BF16), w_su[l].astype(BF16), w_sd[l].astype(BF16), tri, ut)

    y_p = x_c.reshape(BATCH, SEQ, D_MODEL)
    y_s = x_l.reshape(DEC_BATCH, DEC_SEQ, D_MODEL)
    return (y_p, y_s, jnp.stack(new_k, axis=1), jnp.stack(new_v, axis=1), jnp.stack(new_kd, axis=1),
            jnp.stack(new_vd, axis=1), jnp.stack(new_s, axis=1))


def _block_diag_states(s):
    b = s.shape[0]
    st = jnp.swapaxes(s.astype(F32), -1, -2)
    eye = jnp.eye(HG_HEADS, dtype=F32)
    return jnp.einsum('gh,bdgvk->bdgvhk', eye, st).reshape(b, 2, HG_W, HG_W)


def _unblock_states(st):
    b = st.shape[0]
    s5 = st.reshape(b, HG_HEADS, HEAD_DIM, HG_HEADS, HEAD_DIM)
    diag = jnp.stack([s5[:, h, :, h, :] for h in range(HG_HEADS)], axis=1)
    return jnp.swapaxes(diag, -1, -2)
```

```python
import functools

import numpy as np
import jax
import jax.numpy as jnp
from jax import lax
from jax.experimental import pallas as pl
from jax.experimental.pallas import tpu as pltpu

F32 = jnp.float32
BF16 = jnp.bfloat16

D_MODEL = 1024
BATCH = 16
SEQ = 256
DEPTH = 2
DEC_BATCH = 4
DEC_SEQ = 2048
PAST_LEN = 256
NEG = -1e30
GRID_W = 64
HEAD_DIM = 64
SCALE = HEAD_DIM ** -0.5
ROPE_BASE = 10000.0
EPS = 1e-6
HG_HEADS = 4
HG_W = 256
WIN_HEADS = 6
WIN_KV = 2
WIN = 128
POOL_SIZES = (2, 4, 8, 16)
POOL_W = 256
NA_HEADS = 4
NA_KH = 8
NA_KW = 16
N_EXPERTS = 32
TOP_K = 4
D_EXPERT = 256
ROUTED_SCALE = 2.5

N_CTX = BATCH * SEQ
N_LAT = DEC_BATCH * DEC_SEQ
NTOK = N_CTX + N_LAT
WQ_B = WIN_HEADS * HEAD_DIM
WK_B = WIN_KV * HEAD_DIM
WD = NA_HEADS * HEAD_DIM

Z_W = 3072
COL_AQ, COL_AFF, COL_AFB, COL_AI, COL_AG, COL_CU, COL_DQ, COL_DK, COL_DV = (
    0, 256, 512, 768, 1024, 1280, 1536, 1792, 2048)
COL_BQ, COL_BK, COL_BV = 2304, 2688, 2816

TM = 512
TMP = 2048
TN = 512
HB = 128
VMEM_LIMIT = 56 * 1024 * 1024


def _params(sem, vmem=VMEM_LIMIT):
    return pltpu.CompilerParams(dimension_semantics=sem, vmem_limit_bytes=vmem)


def _seg_of_tile(i, tile):
    nct = N_CTX // tile
    per = DEC_SEQ // tile
    return jnp.where(i < nct, 0, 1 + (i - nct) // per)


def _bd_ones(w):
    idx = np.arange(w) // HEAD_DIM
    return (idx[:, None] == idx[None, :]).astype(np.float32)


def _dot(a, b):
    return jnp.dot(a, b, preferred_element_type=F32)


def _dot_nt(a, b):
    return lax.dot_general(a, b, (((1,), (1,)), ((), ())), preferred_element_type=F32)


def _split_dot(x, w_bf16):
    hi = x.astype(BF16)
    lo = (x - hi.astype(F32)).astype(BF16)
    return _dot(hi, w_bf16) + _dot(lo, w_bf16)


def _head_rms(x, bd, gain):
    ms = _split_dot(x * x, bd) * (1.0 / HEAD_DIM)
    return x * lax.rsqrt(ms + EPS) * gain


def _silu(x):
    return x * jax.nn.sigmoid(x)


def _mod_kernel(c_ref, w_ref, b_ref, o_ref):
    c = c_ref[...]
    a = _silu(c).astype(BF16)
    o_ref[...] = _dot(a, w_ref[...].astype(BF16)) + b_ref[...]


def _modulation(cvec8, w_mod, b_mod):
    n = 6 * D_MODEL
    tn = 1536
    return pl.pallas_call(
        _mod_kernel,
        grid=(DEPTH, n // tn),
        in_specs=[
            pl.BlockSpec((8, D_MODEL), lambda l, j: (0, 0)),
            pl.BlockSpec((None, D_MODEL, tn), lambda l, j: (l, 0, j)),
            pl.BlockSpec((None, 1, tn), lambda l, j: (l, 0, j)),
        ],
        out_specs=pl.BlockSpec((None, 8, tn), lambda l, j: (l, 0, j)),
        out_shape=jax.ShapeDtypeStruct((DEPTH, 8, n), F32),
        compiler_params=_params(("arbitrary", "arbitrary")),
        name="modulation",
    )(cvec8, w_mod, b_mod.reshape(DEPTH, 1, n))


def _prenorm(x, gain, shift, scale):
    ms = jnp.mean(x * x, axis=-1, keepdims=True)
    return x * lax.rsqrt(ms + EPS) * gain * (1.0 + scale) + shift


def _two_group_specs(tile, width, nargs=1):
    nct = N_CTX // tile
    if nargs == 1:
        return [pl.BlockSpec((tile, width), lambda i: (jnp.minimum(i, nct - 1), 0)),
                pl.BlockSpec((tile, width), lambda i: (jnp.maximum(i - nct, 0), 0))]
    return [pl.BlockSpec((tile, width), lambda i, j: (jnp.minimum(i, nct - 1), 0)),
            pl.BlockSpec((tile, width), lambda i, j: (jnp.maximum(i - nct, 0), 0))]


def _pick_group(tile, c_ref, l_ref):
    return jnp.where(pl.program_id(0) < N_CTX // tile, c_ref[...], l_ref[...])


def _proj_kernel(xc_ref, xl_ref, m_ref, g_ref, win_ref, z_ref, h_scr):
    @pl.when(pl.program_id(1) == 0)
    def _():
        x = _pick_group(TMP, xc_ref, xl_ref)
        h_scr[...] = _prenorm(x, g_ref[...], m_ref[0:1, :], m_ref[1:2, :]).astype(BF16)

    z_ref[...] = _dot(h_scr[...], win_ref[...])


def _projection(x_c, x_l, mods, norm_g, w_in_p):
    return pl.pallas_call(
        _proj_kernel,
        grid=(NTOK // TMP, Z_W // TN),
        in_specs=_two_group_specs(TMP, D_MODEL, nargs=2) + [
            pl.BlockSpec((None, 6, D_MODEL), lambda i, j: (_seg_of_tile(i, TMP), 0, 0)),
            pl.BlockSpec((1, D_MODEL), lambda i, j: (0, 0)),
            pl.BlockSpec((D_MODEL, TN), lambda i, j: (0, j)),
        ],
        out_specs=pl.BlockSpec((TMP, TN), lambda i, j: (i, j)),
        out_shape=jax.ShapeDtypeStruct((NTOK, Z_W), F32),
        scratch_shapes=[pltpu.VMEM((TMP, D_MODEL), BF16)],
        compiler_params=_params(("arbitrary", "arbitrary")),
        name="projection",
    )(x_c, x_l, mods, norm_g, w_in_p)


def _hgrn_kernel(qf_ref, ff_ref, vf_ref, qb_ref, fb_ref, vb_ref, lb_ref, s0f_ref, s0b_ref, bd_ref,
                 of_ref, ob_ref, sff_ref, sfb_ref, sf_scr, sb_scr):
    _hgrn_direction(False, qf_ref, ff_ref, vf_ref, lb_ref[0], s0f_ref, bd_ref, of_ref, sff_ref, sf_scr)
    _hgrn_direction(True, qb_ref, fb_ref, vb_ref, lb_ref[1], s0b_ref, bd_ref, ob_ref, sfb_ref, sb_scr)


def _hgrn_direction(rev, q_ref, f_ref, v_ref, lb, s0_ref, bd_ref, o_ref, sfin_ref, s_scr):
    i = pl.program_id(0)
    blk = (pl.num_programs(0) - 1 - i) if rev else i
    nct = N_CTX // HB
    per_c = SEQ // HB
    per_l = DEC_SEQ // HB
    is_ctx = blk < nct
    pos = jnp.where(is_ctx, blk % per_c, (blk - nct) % per_l)
    last = jnp.where(is_ctx, per_c - 1, per_l - 1)
    first_pos = last if rev else 0
    final_pos = 0 if rev else last

    @pl.when(pos == first_pos)
    def _():
        s_scr[...] = jnp.zeros_like(s_scr)

    @pl.when((pos == first_pos) & jnp.logical_not(is_ctx))
    def _():
        for h in range(HG_HEADS):
            hs = slice(HEAD_DIM * h, HEAD_DIM * (h + 1))
            s_scr[hs, hs] = s0_ref[h].T

    q = q_ref[...]
    v = v_ref[...]
    f = lb + (1.0 - lb) * jax.nn.sigmoid(f_ref[...])
    lf = jnp.log2(f)
    kk = 1.0 - f

    row = lax.broadcasted_iota(jnp.int32, (HB, HG_W), 0)
    tq = lax.broadcasted_iota(jnp.int32, (HB, HB), 0)
    tk = lax.broadcasted_iota(jnp.int32, (HB, HB), 1)

    def before(x, m):
        return pltpu.roll(x, (HB - m) if rev else m, 0)

    def after(x, m):
        return pltpu.roll(x, m if rev else (HB - m), 0)

    q16 = q.astype(BF16)
    k16 = kk.astype(BF16)
    att = [jnp.where(tq == tk, _dot_nt(_hs(q16, h), _hs(k16, h)), 0.0) for h in range(HG_HEADS)]

    tot = lf
    pin = lf
    sex = jnp.zeros_like(lf)
    m = 1
    while m < HB:
        late = ((row & (2 * m - 1)) < m) if rev else ((row & (2 * m - 1)) >= m)
        qm = jnp.where(late, q * jnp.exp2(pin), 0.0).astype(BF16)
        km = jnp.where(late, 0.0, kk * jnp.exp2(sex)).astype(BF16)
        shift = (2 * m).bit_length() - 1
        same = (tq >> shift) == (tk >> shift)
        for h in range(HG_HEADS):
            sc = _dot_nt(qm[:, 64 * h:64 * h + 64], km[:, 64 * h:64 * h + 64])
            att[h] = att[h] + (sc if 2 * m == HB else jnp.where(same, sc, 0.0))
        tb = before(tot, m)
        ta = after(tot, m)
        pin = pin + jnp.where(late, tb, 0.0)
        sex = sex + jnp.where(late, 0.0, ta)
        tot = tot + jnp.where(late, tb, ta)
        m *= 2

    lane_head = lax.broadcasted_iota(jnp.int32, (HB, HG_W), 1) // HEAD_DIM
    o = jnp.zeros((HB, HG_W), F32)
    for h in range(HG_HEADS):
        vh = jnp.where(lane_head == h, v, 0.0).astype(BF16)
        o = o + _dot(att[h].astype(BF16), vh)

    s_t = s_scr[...]
    qt = (q * jnp.exp2(pin)).astype(BF16)
    o = o + _dot_nt(qt, s_t.astype(BF16))
    o_ref[...] = o

    kt = (kk * jnp.exp2(sex)).astype(BF16)
    dec = jnp.exp2(tot[0:1, :])
    ds_t = _dot(v.T.astype(BF16), kt)
    s_new = s_t * dec + ds_t * bd_ref[...]
    s_scr[...] = s_new

    @pl.when((pos == final_pos) & is_ctx)
    def _():
        for h in range(HG_HEADS):
            hs = slice(HEAD_DIM * h, HEAD_DIM * (h + 1))
            sfin_ref[h] = s_new[hs, hs].T


def _hgrn(z, lbs_l, state_hgrn, layer, bd):
    nb = NTOK // HB
    nct = N_CTX // HB
    rblk = lambda i: nb - 1 - i

    def seq_of(b):
        return jnp.where(b < nct, b // (SEQ // HB), BATCH + (b - nct) // (DEC_SEQ // HB))

    col = lambda blk, c: pl.BlockSpec((HB, HG_W), lambda i: (blk(i), c // HG_W))
    state = lambda blk, d: pl.BlockSpec(
        (None, None, None, HG_HEADS, HEAD_DIM, HEAD_DIM),
        lambda i: (jnp.maximum(seq_of(blk(i)) - BATCH, 0), layer, d, 0, 0, 0))
    final = lambda blk: pl.BlockSpec((None, HG_HEADS, HEAD_DIM, HEAD_DIM),
                                     lambda i: (jnp.minimum(seq_of(blk(i)), BATCH - 1), 0, 0, 0))
    fwd = lambda i: i
    return pl.pallas_call(
        _hgrn_kernel,
        grid=(nb,),
        in_specs=[
            col(fwd, COL_AQ), col(fwd, COL_AFF), col(fwd, COL_AI),
            col(rblk, COL_AQ), col(rblk, COL_AFB), col(rblk, COL_AI),
            pl.BlockSpec((2, 1, HG_W), lambda i: (0, 0, 0)),
            state(fwd, 0), state(rblk, 1),
            pl.BlockSpec((HG_W, HG_W), lambda i: (0, 0)),
        ],
        out_specs=[col(fwd, 0), col(rblk, 0), final(fwd), final(rblk)],
        out_shape=[
            jax.ShapeDtypeStruct((NTOK, HG_W), F32),
            jax.ShapeDtypeStruct((NTOK, HG_W), F32),
            jax.ShapeDtypeStruct((BATCH, HG_HEADS, HEAD_DIM, HEAD_DIM), F32),
            jax.ShapeDtypeStruct((BATCH, HG_HEADS, HEAD_DIM, HEAD_DIM), F32),
        ],
        scratch_shapes=[pltpu.VMEM((HG_W, HG_W), F32), pltpu.VMEM((HG_W, HG_W), F32)],
        compiler_params=_params(("arbitrary",)),
        name="hgrn",
    )(z, z, z, z, z, z, lbs_l.reshape(2, 1, HG_W), state_hgrn, state_hgrn, bd)


def _pool_kernel(u_ref, w_ref, sc_ref, o_ref, *, t_len):
    u = u_ref[...]
    row = lax.broadcasted_iota(jnp.int32, (t_len, POOL_W), 0)
    grp = lax.broadcasted_iota(jnp.int32, (t_len, POOL_W), 1) // HEAD_DIM
    half = jnp.left_shift(1, grp)
    acc = jnp.zeros_like(u)
    for j in range(-8, 8):
        src = row + j
        ok = (j >= -half) & (j < half) & (src >= 0) & (src < t_len)
        shifted = u if j == 0 else pltpu.roll(u, (-j) % t_len, 0)
        acc = acc + jnp.where(ok, shifted, 0.0)
    cnt = (jnp.minimum(row + half, t_len) - jnp.maximum(row - half, 0)).astype(F32)
    y = _dot((acc / cnt - u).astype(BF16), w_ref[...]) * sc_ref[...]
    o_ref[...] = y.astype(BF16)


def _pool(z, row0, nseq, t_len, w_bd, scale):
    return pl.pallas_call(
        functools.partial(_pool_kernel, t_len=t_len),
        grid=(nseq,),
        in_specs=[
            pl.BlockSpec((t_len, POOL_W), lambda b: (row0 // t_len + b, COL_CU // POOL_W)),
            pl.BlockSpec((POOL_W, POOL_W), lambda b: (0, 0)),
            pl.BlockSpec((1, POOL_W), lambda b: (0, 0)),
        ],
        out_specs=pl.BlockSpec((t_len, POOL_W), lambda b: (b, 0)),
        out_shape=jax.ShapeDtypeStruct((nseq * t_len, POOL_W), BF16),
        compiler_params=_params(("arbitrary",)),
        name="pool",
    )(z, w_bd, scale)


def _rope(x, cos, sin):
    w = x.shape[-1]
    lane = lax.broadcasted_iota(jnp.int32, x.shape, 1)
    up = pltpu.roll(x, w - 16, 1)
    dn = pltpu.roll(x, 16, 1)
    return x * cos + jnp.where((lane & 31) < 16, up, dn) * sin


def _prep_kernel(*refs, rope):
    if rope:
        (bq, bk, bv, dq, dk, dv, gq, gk, gdq, gdk, bd, cq, sq, ck, sk,
         oq, ok_, ov, odq, odk, odv) = refs
    else:
        (bq, bk, bv, dq, dk, dv, gq, gk, gdq, gdk, bd,
         oq, ok_, ov, odq, odk, odv, ok32, odk32, ov32, odv32) = refs
    bdm = bd[...]
    q = _head_rms(bq[...], bdm, gq[...])
    k = _head_rms(bk[...], bdm[:WK_B, :WK_B], gk[...])
    qd = _head_rms(dq[...], bdm[:WD, :WD], gdq[...])
    kd = _head_rms(dk[...], bdm[:WD, :WD], gdk[...])
    if rope:
        q = _rope(q, cq[...], sq[...])
        k = _rope(k, ck[...], sk[...])
    else:
        ok32[...] = k
        odk32[...] = kd
        ov32[...] = bv[...]
        odv32[...] = dv[...]
    oq[...] = (q * SCALE).astype(BF16)
    ok_[...] = k.astype(BF16)
    ov[...] = bv[...].astype(BF16)
    odq[...] = (qd * SCALE).astype(BF16)
    odk[...] = kd.astype(BF16)
    odv[...] = dv[...].astype(BF16)


def _prep(z, row0, nrows, gains, bd, rope_tabs):
    tm = 512
    nt = nrows // tm
    r0 = row0 // tm
    rope = rope_tabs is not None
    col = lambda c, w: (lambda i: (r0 + i, c // w))
    in_specs = [
        pl.BlockSpec((tm, WQ_B), col(COL_BQ, WQ_B)),
        pl.BlockSpec((tm, WK_B), col(COL_BK, WK_B)),
        pl.BlockSpec((tm, WK_B), col(COL_BV, WK_B)),
        pl.BlockSpec((tm, WD), col(COL_DQ, WD)),
        pl.BlockSpec((tm, WD), col(COL_DK, WD)),
        pl.BlockSpec((tm, WD), col(COL_DV, WD)),
        pl.BlockSpec((1, WQ_B), lambda i: (0, 0)),
        pl.BlockSpec((1, WK_B), lambda i: (0, 0)),
        pl.BlockSpec((1, WD), lambda i: (0, 0)),
        pl.BlockSpec((1, WD), lambda i: (0, 0)),
        pl.BlockSpec((WQ_B, WQ_B), lambda i: (0, 0)),
    ]
    args = [z, z, z, z, z, z, *gains, bd]
    per = DEC_SEQ // tm
    if rope:
        in_specs += [
            pl.BlockSpec((tm, WQ_B), lambda i: (i % per, 0)),
            pl.BlockSpec((tm, WQ_B), lambda i: (i % per, 0)),
            pl.BlockSpec((tm, WK_B), lambda i: (i % per, 0)),
            pl.BlockSpec((tm, WK_B), lambda i: (i % per, 0)),
        ]
        args += list(rope_tabs)
    widths = [WQ_B, WK_B, WK_B, WD, WD, WD]
    out_specs = [pl.BlockSpec((tm, w), lambda i: (i, 0)) for w in widths]
    out_shape = [jax.ShapeDtypeStruct((nrows, w), BF16) for w in widths]
    if not rope:
        out_specs += [pl.BlockSpec((tm, w), lambda i: (i, 0)) for w in (WK_B, WD, WK_B, WD)]
        out_shape += [jax.ShapeDtypeStruct((nrows, w), F32) for w in (WK_B, WD, WK_B, WD)]
    return pl.pallas_call(
        functools.partial(_prep_kernel, rope=rope),
        grid=(nt,),
        in_specs=in_specs,
        out_specs=out_specs,
        out_shape=out_shape,
        compiler_params=_params(("arbitrary",)),
        name="prep_lat" if rope else "prep_ctx",
    )(*args)


def _softmax_pv(scores, values, sink):
    m = scores[0].max(axis=-1, keepdims=True)
    for s in scores[1:]:
        m = jnp.maximum(m, s.max(axis=-1, keepdims=True))
    if sink is not None:
        m = jnp.maximum(m, sink)
    den = jnp.zeros_like(m) if sink is None else jnp.exp(sink - m)
    acc = None
    for s, v in zip(scores, values):
        p = jnp.exp(s - m)
        den = den + p.sum(axis=-1, keepdims=True)
        pv = _dot(p.astype(BF16), v)
        acc = pv if acc is None else acc + pv
    return acc / den


def _hs(x, h):
    return x[:, HEAD_DIM * h:HEAD_DIM * (h + 1)]


GQA = WIN_HEADS // WIN_KV


def _stack_group(q, sink_ref, kv, rows):
    qs = jnp.concatenate([_hs(q, GQA * kv + j) for j in range(GQA)], axis=0)
    part = lax.broadcasted_iota(jnp.int32, (GQA * rows, 1), 0) // rows
    sink = jnp.zeros((GQA * rows, 1), F32)
    for j in range(GQA):
        sink = jnp.where(part == j, sink_ref[0, GQA * kv + j], sink)
    return qs, sink


def _ctx_attn_kernel(sink_ref, q_ref, k_ref, v_ref, qd_ref, kd_ref, vd_ref, ob_ref, od_ref):
    q, k, v = q_ref[...], k_ref[...], v_ref[...]
    for kv in range(WIN_KV):
        qs, sink = _stack_group(q, sink_ref, kv, SEQ)
        s = _dot_nt(qs, _hs(k, kv))
        o = _softmax_pv([s], [_hs(v, kv)], sink)
        for j in range(GQA):
            h = GQA * kv + j
            ob_ref[:, HEAD_DIM * h:HEAD_DIM * (h + 1)] = o[SEQ * j:SEQ * (j + 1)].astype(BF16)
    qd, kd, vd = qd_ref[...], kd_ref[...], vd_ref[...]
    for h in range(NA_HEADS):
        s = _dot_nt(_hs(qd, h), _hs(kd, h))
        o = _softmax_pv([s], [_hs(vd, h)], None)
        od_ref[:, HEAD_DIM * h:HEAD_DIM * (h + 1)] = o.astype(BF16)


def _ctx_attn(sink, q, k, v, qd, kd, vd):
    blk = lambda w: pl.BlockSpec((SEQ, w), lambda b: (b, 0))
    return pl.pallas_call(
        _ctx_attn_kernel,
        grid=(BATCH,),
        in_specs=[pl.BlockSpec(memory_space=pltpu.SMEM),
                  blk(WQ_B), blk(WK_B), blk(WK_B), blk(WD), blk(WD), blk(WD)],
        out_specs=[blk(WQ_B), blk(WD)],
        out_shape=[jax.ShapeDtypeStruct((N_CTX, WQ_B), BF16), jax.ShapeDtypeStruct((N_CTX, WD), BF16)],
        compiler_params=_params(("arbitrary",)),
        name="ctx_attn",
    )(sink, q, k, v, qd, kd, vd)


WIN_SPAN = 3 * WIN


def _win_attn_kernel(sink_ref, q_ref, k_ref, v_ref, kc_ref, vc_ref, o_ref):
    qi = pl.program_id(1)
    start = pl.multiple_of(jnp.clip(qi * WIN - WIN, 0, DEC_SEQ - WIN_SPAN), WIN)
    q = q_ref[...]
    kw = k_ref[pl.ds(start, WIN_SPAN), :]
    vw = v_ref[pl.ds(start, WIN_SPAN), :]
    kc, vc = kc_ref[...], vc_ref[...]
    rows = GQA * WIN
    qpos = qi * WIN + lax.broadcasted_iota(jnp.int32, (rows, WIN_SPAN), 0) % WIN
    kpos = start + lax.broadcasted_iota(jnp.int32, (rows, WIN_SPAN), 1)
    valid = jnp.abs(qpos - kpos) <= WIN
    for kv in range(WIN_KV):
        qs, sink = _stack_group(q, sink_ref, kv, WIN)
        s_loc = jnp.where(valid, _dot_nt(qs, _hs(kw, kv)), NEG)
        s_ctx = _dot_nt(qs, _hs(kc, kv))
        o = _softmax_pv([s_loc, s_ctx], [_hs(vw, kv), _hs(vc, kv)], sink)
        for j in range(GQA):
            h = GQA * kv + j
            o_ref[:, HEAD_DIM * h:HEAD_DIM * (h + 1)] = o[WIN * j:WIN * (j + 1)].astype(BF16)


def _win_attn(sink, q, k, v, kc, vc):
    nq = DEC_SEQ // WIN
    return pl.pallas_call(
        _win_attn_kernel,
        grid=(DEC_BATCH, nq),
        in_specs=[
            pl.BlockSpec(memory_space=pltpu.SMEM),
            pl.BlockSpec((WIN, WQ_B), lambda b, i: (b * nq + i, 0)),
            pl.BlockSpec((None, DEC_SEQ, WK_B), lambda b, i: (b, 0, 0)),
            pl.BlockSpec((None, DEC_SEQ, WK_B), lambda b, i: (b, 0, 0)),
            pl.BlockSpec((None, PAST_LEN, WK_B), lambda b, i: (b, 0, 0)),
            pl.BlockSpec((None, PAST_LEN, WK_B), lambda b, i: (b, 0, 0)),
        ],
        out_specs=pl.BlockSpec((WIN, WQ_B), lambda b, i: (b * nq + i, 0)),
        out_shape=jax.ShapeDtypeStruct((N_LAT, WQ_B), BF16),
        compiler_params=_params(("arbitrary", "arbitrary")),
        name="win_attn",
    )(sink, q, k.reshape(DEC_BATCH, DEC_SEQ, WK_B), v.reshape(DEC_BATCH, DEC_SEQ, WK_B), kc, vc)


NA_ROWS = DEC_SEQ // GRID_W
NA_G = 4
NA_NG = NA_ROWS // NA_G
NA_UROWS = NA_KH + NA_G - 1
NA_UKEYS = NA_UROWS * GRID_W
NA_QROWS = NA_G * GRID_W


def _na_union_start(g):
    return jnp.clip(g * NA_G - NA_KH // 2, 0, NA_ROWS - NA_UROWS)


def _natten_kernel(q_ref, k_ref, v_ref, kc_ref, vc_ref, bias_ref, o_ref):
    start = pl.multiple_of(_na_union_start(pl.program_id(1)) * GRID_W, GRID_W)
    q = q_ref[...]
    kw = k_ref[pl.ds(start, NA_UKEYS), :]
    vw = v_ref[pl.ds(start, NA_UKEYS), :]
    kc, vc = kc_ref[...], vc_ref[...]
    for h in range(NA_HEADS):
        qh = _hs(q, h)
        s_loc = _dot_nt(qh, _hs(kw, h)) + bias_ref[h]
        s_ctx = _dot_nt(qh, _hs(kc, h))
        o = _softmax_pv([s_loc, s_ctx], [_hs(vw, h), _hs(vc, h)], None)
        o_ref[:, HEAD_DIM * h:HEAD_DIM * (h + 1)] = o.astype(BF16)


def _natten(q, k, v, kc, vc, bias):
    variant = lambda g: jnp.where(g == 0, 0, jnp.where(g == NA_NG - 1, 2, 1))
    return pl.pallas_call(
        _natten_kernel,
        grid=(DEC_BATCH, NA_NG),
        in_specs=[
            pl.BlockSpec((NA_QROWS, WD), lambda b, g: (b * NA_NG + g, 0)),
            pl.BlockSpec((None, DEC_SEQ, WD), lambda b, g: (b, 0, 0)),
            pl.BlockSpec((None, DEC_SEQ, WD), lambda b, g: (b, 0, 0)),
            pl.BlockSpec((None, PAST_LEN, WD), lambda b, g: (b, 0, 0)),
            pl.BlockSpec((None, PAST_LEN, WD), lambda b, g: (b, 0, 0)),
            pl.BlockSpec((None, NA_HEADS, NA_QROWS, NA_UKEYS), lambda b, g: (variant(g), 0, 0, 0)),
        ],
        out_specs=pl.BlockSpec((NA_QROWS, WD), lambda b, g: (b * NA_NG + g, 0)),
        out_shape=jax.ShapeDtypeStruct((N_LAT, WD), BF16),
        compiler_params=_params(("arbitrary", "arbitrary")),
        name="natten",
    )(q, k.reshape(DEC_BATCH, DEC_SEQ, WD), v.reshape(DEC_BATCH, DEC_SEQ, WD), kc, vc, bias)


def _natten_bias(rpb):
    c = np.arange(GRID_W)
    cstart = np.clip(c - NA_KW // 2, 0, GRID_W - NA_KW)
    inwin = (c[None, :] >= cstart[:, None]) & (c[None, :] < cstart[:, None] + NA_KW)
    dc = np.clip(c[None, :] - c[:, None] + NA_KW - 1, 0, 2 * NA_KW - 2)
    pick = (dc[None] == np.arange(2 * NA_KW - 1)[:, None, None]).astype(np.float32)
    toep = jnp.einsum('hdj,jck->hdck', rpb.astype(F32), jnp.asarray(pick),
                      precision=lax.Precision.HIGHEST)
    toep = jnp.where(inwin[None, None], toep, NEG)
    masked = jnp.full((NA_HEADS, GRID_W, GRID_W), NEG, F32)
    variants = []
    for g in (0, 1, NA_NG - 1):
        u0 = int(np.clip(g * NA_G - NA_KH // 2, 0, NA_ROWS - NA_UROWS))
        rows = []
        for j in range(NA_G):
            r = g * NA_G + j
            w0 = int(np.clip(r - NA_KH // 2, 0, NA_ROWS - NA_KH))
            blocks = []
            for i in range(NA_UROWS):
                kr = u0 + i
                blocks.append(toep[:, kr - r + NA_KH - 1] if w0 <= kr < w0 + NA_KH else masked)
            rows.append(jnp.concatenate(blocks, axis=-1))
        variants.append(jnp.concatenate(rows, axis=-2))
    return jnp.stack(variants, axis=0)


def _merge_kernel(xc_ref, xl_ref, m_ref, g_ref, of_ref, ob_ref, ag_ref, on_ref, bd_ref,
                  ybc_ref, ybl_ref, ycc_ref, ycl_ref, ydc_ref, ydl_ref,
                  wg_ref, bg_ref, wa_ref, wb_ref, wc_ref, wd_ref, wo_ref, o_ref):
    x = _pick_group(TM, xc_ref, xl_ref)
    h = _prenorm(x, g_ref[...], m_ref[0:1, :], m_ref[1:2, :]).astype(BF16)
    o = of_ref[...] + ob_ref[...]
    ya = _head_rms(o, bd_ref[...], on_ref[...]) * _silu(ag_ref[...])
    yb = _pick_group(TM, ybc_ref, ybl_ref)
    yc = _pick_group(TM, ycc_ref, ycl_ref)
    yd = _pick_group(TM, ydc_ref, ydl_ref)
    d = D_MODEL
    merged = jnp.zeros((TM, d), F32)
    branches = ((ya.astype(BF16), wa_ref), (yb, wb_ref), (yc, wc_ref), (yd, wd_ref))
    for b, (y, w_ref) in enumerate(branches):
        gate = jax.nn.sigmoid(_dot(h, wg_ref[:, d * b:d * (b + 1)]) + bg_ref[:, d * b:d * (b + 1)])
        merged = merged + gate * _dot(y, w_ref[...])
    o_ref[...] = x + m_ref[2:3, :] * _dot(merged.astype(BF16), wo_ref[...])


def _merge(x_c, x_l, mods, norm_g, o_f, o_b, z, onorm, bd, yb_c, yb_l, yc_c, yc_l, yd_c, yd_l,
           w_gate, b_gate, wa, wb, wc, wd, wo):
    full = lambda a: pl.BlockSpec(a.shape, lambda i: (0, 0))
    return pl.pallas_call(
        _merge_kernel,
        grid=(NTOK // TM,),
        in_specs=_two_group_specs(TM, D_MODEL) + [
            pl.BlockSpec((None, 6, D_MODEL), lambda i: (_seg_of_tile(i, TM), 0, 0)),
            full(norm_g),
            pl.BlockSpec((TM, HG_W), lambda i: (i, 0)),
            pl.BlockSpec((TM, HG_W), lambda i: (i, 0)),
            pl.BlockSpec((TM, HG_W), lambda i: (i, COL_AG // HG_W)),
            full(onorm), full(bd),
        ] + _two_group_specs(TM, WQ_B) + _two_group_specs(TM, POOL_W) + _two_group_specs(TM, WD) + [
            full(w_gate), full(b_gate), full(wa), full(wb), full(wc), full(wd), full(wo),
        ],
        out_specs=pl.BlockSpec((TM, D_MODEL), lambda i: (i, 0)),
        out_shape=jax.ShapeDtypeStruct((NTOK, D_MODEL), F32),
        compiler_params=_params(("arbitrary",)),
        name="merge",
    )(x_c, x_l, mods, norm_g, o_f, o_b, z, onorm, bd, yb_c, yb_l, yc_c, yc_l, yd_c, yd_l,
      w_gate, b_gate, wa, wb, wc, wd, wo)


E_PAD = 128
TR = 256
RT = 512
SLAB = 16
NT_R = NTOK // TR
S_LOC = 1536
S_MAX = NTOK * TOP_K + NT_R * N_EXPERTS * (SLAB - 1) + N_EXPERTS * (RT - SLAB)
N_XT = S_MAX // RT
XW = D_MODEL + 2 * E_PAD
assert S_LOC >= TR * TOP_K + N_EXPERTS * (SLAB - 1) and S_LOC % 128 == 0 and S_MAX % RT == 0


def _router_kernel(x_ref, m_ref, g_ref, wr_ref, br_ref, wsg_ref, wsu_ref, wsd_ref, tri_ref, ut_ref,
                   xa_ref, sl_ref, slt_ref, cnt_ref, xs_ref):
    lane = lax.broadcasted_iota(jnp.int32, (TR, E_PAD), 1)
    x = x_ref[...]
    ms = jnp.mean(x * x, axis=-1, keepdims=True)
    y = x * lax.rsqrt(ms + EPS) * g_ref[...]
    h = y * (1.0 + m_ref[4:5, :]) + m_ref[3:4, :]
    h_hi = h.astype(BF16)
    h_lo = (h - h_hi.astype(F32)).astype(BF16)
    w = wr_ref[...]
    w_hi = w.astype(BF16)
    w_lo = (w - w_hi.astype(F32)).astype(BF16)
    logits = _dot(h_hi, w_hi) + _dot(h_hi, w_lo) + _dot(h_lo, w_hi)
    scores = jax.nn.sigmoid(logits)
    sel = jnp.where(lane < N_EXPERTS, scores + br_ref[...], -jnp.inf)
    picked = jnp.zeros((TR, E_PAD), F32)
    hot = jnp.zeros((TR, E_PAD), F32)
    idxs = []
    lane_f = lane.astype(F32)
    for _ in range(TOP_K):
        mx = sel.max(axis=-1, keepdims=True)
        idx = jnp.min(jnp.where(sel == mx, lane_f, float(E_PAD)), axis=-1, keepdims=True)
        hit = lane_f == idx
        picked = jnp.where(hit, scores, picked)
        hot = jnp.where(hit, 1.0, hot)
        sel = jnp.where(hit, -jnp.inf, sel)
        idxs.append(idx)
    wts = ROUTED_SCALE * picked / picked.sum(axis=-1, keepdims=True)

    cnt = hot.sum(axis=0, keepdims=True)
    pad = jnp.floor((cnt + (SLAB - 1.0)) * (1.0 / SLAB)) * SLAB
    loc = _dot(jnp.broadcast_to(pad, (8, E_PAD)).astype(BF16), ut_ref[...])[0:1, :]
    rank = _dot(tri_ref[...], hot.astype(BF16))
    slotmat = loc + rank
    sl = jnp.zeros((TR, E_PAD), F32)
    for k in range(TOP_K):
        s_k = jnp.sum(jnp.where(lane_f == idxs[k], slotmat, 0.0), axis=-1, keepdims=True)
        sl = jnp.where(lane == k, s_k, sl)
    sl_ref[...] = sl
    slt_ref[...] = sl.T[0:8, :]
    cnt_ref[...] = pad

    w16 = wts.astype(BF16)
    xa_ref[:, 0:D_MODEL] = h_hi
    xa_ref[:, D_MODEL:D_MODEL + E_PAD] = w16
    xa_ref[:, D_MODEL + E_PAD:XW] = (wts - w16.astype(F32)).astype(BF16)

    a = _silu(_dot(h_hi, wsg_ref[...])) * _dot(h_hi, wsu_ref[...])
    xs_ref[...] = x + m_ref[5:6, :] * _dot(a.astype(BF16), wsd_ref[...])


def _router(x, mods, norm_g, w_router, b_router, wsg, wsu, wsd, tri, ut):
    full = lambda a: pl.BlockSpec(a.shape, lambda i: (0,) * a.ndim)
    return pl.pallas_call(
        _router_kernel,
        grid=(NT_R,),
        in_specs=[
            pl.BlockSpec((TR, D_MODEL), lambda i: (i, 0)),
            pl.BlockSpec((None, 6, D_MODEL), lambda i: (_seg_of_tile(i, TR), 0, 0)),
            full(norm_g), full(w_router), full(b_router), full(wsg), full(wsu), full(wsd), full(tri), full(ut),
        ],
        out_specs=[
            pl.BlockSpec((TR, XW), lambda i: (i, 0)),
            pl.BlockSpec((TR, E_PAD), lambda i: (i, 0)),
            pl.BlockSpec((None, 8, TR), lambda i: (i, 0, 0)),
            pl.BlockSpec((None, 1, E_PAD), lambda i: (i, 0, 0)),
            pl.BlockSpec((TR, D_MODEL), lambda i: (i, 0)),
        ],
        out_shape=[
            jax.ShapeDtypeStruct((NTOK, XW), BF16),
            jax.ShapeDtypeStruct((NTOK, E_PAD), F32),
            jax.ShapeDtypeStruct((NT_R, 8, TR), F32),
            jax.ShapeDtypeStruct((NT_R, 1, E_PAD), F32),
            jax.ShapeDtypeStruct((NTOK, D_MODEL), F32),
        ],
        compiler_params=_params(("arbitrary",)),
        name="router",
    )(x, mods, norm_g, w_router, b_router, wsg, wsu, wsd, tri, ut)


def _slab_copy(src, src_row, dst, dst_row, sem):
    hint = lambda r: r if isinstance(r, int) else pl.multiple_of(r, SLAB)
    return pltpu.make_async_copy(src.at[pl.ds(hint(src_row), SLAB), :],
                                 dst.at[pl.ds(hint(dst_row), SLAB), :], sem)


NSL = S_LOC // SLAB


def _for_each_slab(grow_ref, ns_ref, t, fn):
    def per_slab(n, c):
        fn(n * SLAB, grow_ref[t * NSL + n])
        return c

    lax.fori_loop(0, ns_ref[t], per_slab, 0)


def _compact_kernel(grow_ref, ns_ref, nd_ref, toff_ref, tn_ref, nu_ref,
                    xa_ref, slt_ref, xs_hbm, xc_scr, zero_scr, sem):
    t = pl.program_id(0)
    slot = t % 2

    def wait_n(n, s):
        def body(_, c):
            _slab_copy(xc_scr.at[s], 0, xs_hbm, 0, sem.at[s]).wait()
            return c
        lax.fori_loop(0, n, body, 0)

    def unused_tile_copy(j):
        row = pl.multiple_of((nu_ref[0] + j) * RT, RT)
        return pltpu.make_async_copy(zero_scr, xs_hbm.at[pl.ds(row, RT), :], sem.at[2])

    @pl.when(t == 0)
    def _():
        zero_scr[...] = jnp.zeros_like(zero_scr)

        def body(j, c):
            unused_tile_copy(j).start()
            return c
        lax.fori_loop(0, N_XT - nu_ref[0], body, 0)

    @pl.when(t >= 2)
    def _():
        wait_n(nd_ref[t - 2], slot)

    row = lax.broadcasted_iota(jnp.int32, (S_LOC, TR), 0)
    slt = slt_ref[...].astype(jnp.int32)
    hit = row == slt[0:1, :]
    for k in range(1, TOP_K):
        hit = hit | (row == slt[k:k + 1, :])
    onehot = jnp.where(hit, 1.0, 0.0).astype(BF16)
    xc_scr[slot] = _dot(onehot, xa_ref[...]).astype(BF16)

    _for_each_slab(grow_ref, ns_ref, t,
                   lambda lr, gr: _slab_copy(xc_scr.at[slot], lr, xs_hbm, gr, sem.at[slot]).start())

    @pl.when(t < N_EXPERTS)
    def _():
        def body(s, c):
            _slab_copy(zero_scr, 0, xs_hbm, toff_ref[t] + s * SLAB, sem.at[slot]).start()
            return c
        lax.fori_loop(0, tn_ref[t], body, 0)

    @pl.when(t == NT_R - 1)
    def _():
        wait_n(nd_ref[t], slot)
        wait_n(nd_ref[t - 1], 1 - slot)

        def body(j, c):
            unused_tile_copy(j).wait()
            return c
        lax.fori_loop(0, N_XT - nu_ref[0], body, 0)


def _compact(meta, xa, slt):
    grid_spec = pltpu.PrefetchScalarGridSpec(
        num_scalar_prefetch=6,
        grid=(NT_R,),
        in_specs=[
            pl.BlockSpec((TR, XW), lambda i, *_: (i, 0)),
            pl.BlockSpec((None, 8, TR), lambda i, *_: (i, 0, 0)),
        ],
        out_specs=pl.BlockSpec(memory_space=pl.ANY),
        scratch_shapes=[pltpu.VMEM((2, S_LOC, XW), BF16), pltpu.VMEM((RT, XW), BF16),
                        pltpu.SemaphoreType.DMA((3,))],
    )
    return pl.pallas_call(
        _compact_kernel,
        grid_spec=grid_spec,
        out_shape=jax.ShapeDtypeStruct((S_MAX, XW), BF16),
        compiler_params=_params(("arbitrary",)),
        name="compact",
    )(meta['grow'], meta['ns'], meta['nd'], meta['toff'], meta['tn'], meta['nu'], xa, slt)


def _expert_kernel(te_ref, ti_ref, nu_ref, nx_ref, par_ref, xs_ref, wg_hbm, wu_hbm, wd_hbm, y_ref,
                   wg_f, wu_f, wd_f, wgu_s, wd_s, sem, *, layer):
    i = pl.program_id(0)
    e = te_ref[i]
    slot = par_ref[i]
    used = i < nu_ref[0]
    first = used & ((i == 0) | (e != te_ref[jnp.maximum(i - 1, 0)]))

    def weight_copies(expert, s):
        return (pltpu.make_async_copy(wg_hbm.at[layer, expert], wg_f.at[s], sem.at[s]),
                pltpu.make_async_copy(wu_hbm.at[layer, expert], wu_f.at[s], sem.at[s]),
                pltpu.make_async_copy(wd_hbm.at[layer, expert], wd_f.at[s], sem.at[s]))

    @pl.when(i == 0)
    def _():
        for c in weight_copies(e, slot):
            c.start()

    @pl.when(first)
    def _():
        for c in weight_copies(e, slot):
            c.wait()
        wgu_s[:, 0:D_EXPERT] = wg_f[slot].astype(BF16)
        wgu_s[:, D_EXPERT:2 * D_EXPERT] = wu_f[slot].astype(BF16)
        wd_s[...] = wd_f[slot].astype(BF16)

    @pl.when(first & (nx_ref[i] >= 0))
    def _():
        for c in weight_copies(nx_ref[i], 1 - slot):
            c.start()

    @pl.when(used)
    def _():
        x = xs_ref[:, 0:D_MODEL]
        gw = xs_ref[:, D_MODEL:D_MODEL + E_PAD].astype(F32) + xs_ref[:, D_MODEL + E_PAD:XW].astype(F32)
        lane = lax.broadcasted_iota(jnp.int32, (RT, E_PAD), 1)
        ge = jnp.sum(jnp.where(lane == e, gw, 0.0), axis=-1, keepdims=True)
        gu = _dot(x, wgu_s[...])
        a = _silu(gu[:, 0:D_EXPERT]) * gu[:, D_EXPERT:2 * D_EXPERT]
        y_ref[...] = _dot((a * ge).astype(BF16), wd_s[...]).astype(BF16)

    @pl.when(i >= nu_ref[0])
    def _():
        y_ref[...] = jnp.zeros_like(y_ref)


def _experts(meta, layer, xs, w_eg, w_eu, w_ed):
    any_spec = pl.BlockSpec(memory_space=pl.ANY)
    grid_spec = pltpu.PrefetchScalarGridSpec(
        num_scalar_prefetch=5,
        grid=(N_XT,),
        in_specs=[pl.BlockSpec((RT, XW), lambda i, te, ti, *_: (ti[i], 0)), any_spec, any_spec, any_spec],
        out_specs=pl.BlockSpec((RT, D_MODEL), lambda i, *_: (i, 0)),
        scratch_shapes=[pltpu.VMEM((2, D_MODEL, D_EXPERT), F32), pltpu.VMEM((2, D_MODEL, D_EXPERT), F32),
                        pltpu.VMEM((2, D_EXPERT, D_MODEL), F32),
                        pltpu.VMEM((D_MODEL, 2 * D_EXPERT), BF16), pltpu.VMEM((D_EXPERT, D_MODEL), BF16),
                        pltpu.SemaphoreType.DMA((2,))],
    )
    return pl.pallas_call(
        functools.partial(_expert_kernel, layer=layer),
        grid_spec=grid_spec,
        out_shape=jax.ShapeDtypeStruct((S_MAX, D_MODEL), BF16),
        compiler_params=_params(("arbitrary",)),
        name="experts",
    )(meta['te'], meta['ti'], meta['nu'], meta['nx'], meta['par'], xs, w_eg, w_eu, w_ed)


def _combine_kernel(grow_ref, ns_ref, xs_ref, m_ref, sl_ref, y_hbm, oc_ref, ol_ref, yc_scr, sem):
    t = pl.program_id(0)
    slot = t % 2

    def issue(tt, s):
        _for_each_slab(grow_ref, ns_ref, tt,
                       lambda lr, gr: _slab_copy(y_hbm, gr, yc_scr.at[s], lr, sem.at[s]).start())

    @pl.when(t == 0)
    def _():
        yc_scr[...] = jnp.zeros_like(yc_scr)
        issue(0, 0)

    @pl.when(t + 1 < NT_R)
    def _():
        issue(t + 1, 1 - slot)

    def wait_body(_, c):
        _slab_copy(y_hbm, 0, yc_scr.at[slot], 0, sem.at[slot]).wait()
        return c
    lax.fori_loop(0, ns_ref[t], wait_body, 0)

    col = lax.broadcasted_iota(jnp.int32, (TR, S_LOC), 1)
    sl = sl_ref[...].astype(jnp.int32)
    hit = col == sl[:, 0:1]
    for k in range(1, TOP_K):
        hit = hit | (col == sl[:, k:k + 1])
    onehot = jnp.where(hit, 1.0, 0.0).astype(BF16)
    out = xs_ref[...] + m_ref[5:6, :] * _dot(onehot, yc_scr[slot])

    @pl.when(t < N_CTX // TR)
    def _():
        oc_ref[...] = out

    @pl.when(t >= N_CTX // TR)
    def _():
        ol_ref[...] = out


def _combine(meta, xsh, mods, sl, y):
    nct = N_CTX // TR
    out_specs = [pl.BlockSpec((TR, D_MODEL), lambda i, *_: (jnp.minimum(i, nct - 1), 0)),
                 pl.BlockSpec((TR, D_MODEL), lambda i, *_: (jnp.maximum(i - nct, 0), 0))]
    out_shape = [jax.ShapeDtypeStruct((N_CTX, D_MODEL), F32), jax.ShapeDtypeStruct((N_LAT, D_MODEL), F32)]
    grid_spec = pltpu.PrefetchScalarGridSpec(
        num_scalar_prefetch=2,
        grid=(NT_R,),
        in_specs=[
            pl.BlockSpec((TR, D_MODEL), lambda i, *_: (i, 0)),
            pl.BlockSpec((None, 6, D_MODEL), lambda i, *_: (_seg_of_tile(i, TR), 0, 0)),
            pl.BlockSpec((TR, E_PAD), lambda i, *_: (i, 0)),
            pl.BlockSpec(memory_space=pl.ANY),
        ],
        out_specs=out_specs,
        scratch_shapes=[pltpu.VMEM((2, S_LOC, D_MODEL), BF16), pltpu.SemaphoreType.DMA((2,))],
    )
    return pl.pallas_call(
        _combine_kernel,
        grid_spec=grid_spec,
        out_shape=out_shape,
        compiler_params=_params(("arbitrary",)),
        name="combine",
    )(meta['grow'], meta['ns'], xsh, mods, sl, y)


def _route_meta(cnt):
    pc = cnt[:, 0, :N_EXPERTS].astype(jnp.int32)
    tot = pc.sum(axis=0)
    tot_pad = ((tot + RT - 1) // RT) * RT
    ends = jnp.cumsum(tot_pad)
    base = ends - tot_pad
    dst = base[None, :] + jnp.cumsum(pc, axis=0) - pc
    cum = jnp.cumsum(pc, axis=1)
    ns = cum[:, -1] // SLAB
    lrow = jnp.arange(NSL, dtype=jnp.int32)[None, :, None] * SLAB
    owner = jnp.minimum(jnp.sum(cum[:, None, :] <= lrow, axis=2), N_EXPERTS - 1)
    mine = owner[:, :, None] == jnp.arange(N_EXPERTS, dtype=jnp.int32)[None, None, :]
    grow = jnp.sum(jnp.where(mine, (dst - cum + pc)[:, None, :], 0), axis=2) + lrow[:, :, 0]
    tn = (tot_pad - tot) // SLAB
    nd = ns + jnp.pad(tn, (0, NT_R - N_EXPERTS))
    n_used = ends[-1] // RT
    ti = jnp.minimum(jnp.arange(N_XT, dtype=jnp.int32), n_used - 1)
    te = jnp.minimum(jnp.sum(ends[None, :] <= (ti * RT)[:, None], axis=1), N_EXPERTS - 1)
    eidx = jnp.arange(N_EXPERTS, dtype=jnp.int32)
    has = tot_pad > 0
    later = has[None, :] & (eidx[None, :] > eidx[:, None])
    nxt_e = jnp.min(jnp.where(later, eidx[None, :], N_EXPERTS), axis=1)
    nxt_e = jnp.where(nxt_e == N_EXPERTS, -1, nxt_e)
    ordinal = jnp.cumsum(has.astype(jnp.int32)) - 1
    is_e = te[:, None] == eidx[None, :]
    nx = jnp.sum(jnp.where(is_e, nxt_e[None, :], 0), axis=1)
    par = jnp.sum(jnp.where(is_e, ordinal[None, :], 0), axis=1) & 1
    i32 = lambda a: a.astype(jnp.int32)
    return dict(grow=i32(grow.reshape(-1)), ns=i32(ns), nd=i32(nd), toff=i32(base + tot), tn=i32(tn),
                te=i32(te), ti=i32(ti), nu=i32(n_used.reshape(1)), nx=i32(nx), par=i32(par))


def _moe(layer, x, mods, norm_g, w_router, b_router, w_eg, w_eu, w_ed, wsg, wsu, wsd, tri, ut):
    xa, sl, slt, cnt, xsh = _router(x, mods, norm_g, w_router, b_router, wsg, wsu, wsd, tri, ut)
    meta = _route_meta(cnt)
    xs = _compact(meta, xa, slt)
    y = _experts(meta, layer, xs, w_eg, w_eu, w_ed)
    return _combine(meta, xsh, mods, sl, y)


def _rope_tables(width):
    t = np.arange(DEC_SEQ)
    quarter = HEAD_DIM // 4
    inv = (ROPE_BASE ** (-np.arange(quarter) / quarter)).astype(np.float32)
    ang_r = (t // GRID_W).astype(np.float32)[:, None] * inv[None]
    ang_c = (t % GRID_W).astype(np.float32)[:, None] * inv[None]
    cos = np.concatenate([np.cos(ang_r), np.cos(ang_r), np.cos(ang_c), np.cos(ang_c)], axis=1)
    sin = np.concatenate([-np.sin(ang_r), np.sin(ang_r), -np.sin(ang_c), np.sin(ang_c)], axis=1)
    reps = width // HEAD_DIM
    return (jnp.asarray(np.tile(cos, (1, reps)), F32), jnp.asarray(np.tile(sin, (1, reps)), F32))


def _permute_w_in(w):
    a = w[:, 0:1280]
    bq, bk, bv = w[:, 1280:1664], w[:, 1664:1792], w[:, 1792:1920]
    cu = w[:, 1920:2176]
    d = w[:, 2176:2944]
    pad = jnp.zeros((w.shape[0], Z_W - 2944), w.dtype)
    return jnp.concatenate([a, cu, d, bq, bk, bv, pad], axis=1)


def _block_diag(blocks):
    g = blocks.shape[0]
    eye = jnp.eye(g, dtype=blocks.dtype)
    return jnp.einsum('gh,gij->gihj', eye, blocks).reshape(g * HEAD_DIM, g * HEAD_DIM)


def kernel(x_prompt, x_sample, cache_win_k, cache_win_v, cache_na_k, cache_na_v, state_hgrn, c, c_ctx, w_mod, b_mod, norm1_g, norm2_g, w_in, w_mgate, b_mgate, hg_lb, hg_onorm, win_qn, win_kn, win_sink, pool_w, pool_scale, na_qn, na_kn, na_rpb, w_branch, w_out, w_router, b_router, w_eg, w_eu, w_ed, w_sg, w_su, w_sd):
    lbp = jax.nn.softmax(hg_lb.astype(F32), axis=0)
    lbs = jnp.cumsum(lbp, axis=0) - lbp[0:1]

    cvec8 = jnp.concatenate([c_ctx[None], c, jnp.zeros((3, D_MODEL), F32)], axis=0)
    mods_all = _modulation(cvec8, w_mod, b_mod).reshape(DEPTH, 8, 6, D_MODEL)

    bd384 = jnp.asarray(_bd_ones(WQ_B), BF16)
    bd256 = bd384[:HG_W, :HG_W]
    bd256_f32 = jnp.asarray(_bd_ones(HG_W), F32)
    rope_q = _rope_tables(WQ_B)
    rope_k = _rope_tables(WK_B)
    tile = lambda g, reps: jnp.tile(g, reps)[None, :]
    tri = jnp.asarray(np.tril(np.ones((TR, TR), np.float32), -1), BF16)
    ut = jnp.asarray(np.triu(np.ones((E_PAD, E_PAD), np.float32), 1), BF16)

    x_c, x_l = x_prompt.reshape(N_CTX, D_MODEL), x_sample.reshape(N_LAT, D_MODEL)
    new_k, new_v, new_kd, new_vd, new_s = [], [], [], [], []
    for l in range(DEPTH):
        mods = mods_all[l]
        z = _projection(x_c, x_l, mods, norm1_g[l][None], _permute_w_in(w_in[l]).astype(BF16))

        o_f, o_b, sfin_f, sfin_b = _hgrn(z, lbs[l], state_hgrn.astype(F32), l, bd256_f32)
        new_s.append(jnp.stack([sfin_f, sfin_b], axis=1))

        w_pool = _block_diag(pool_w[l]).astype(BF16)
        yc_c = _pool(z, 0, BATCH, SEQ, w_pool, pool_scale[l][None])
        yc_l = _pool(z, N_CTX, DEC_BATCH, DEC_SEQ, w_pool, pool_scale[l][None])

        gains = (tile(win_qn[l], WIN_HEADS), tile(win_kn[l], WIN_KV), tile(na_qn[l], NA_HEADS), tile(na_kn[l], NA_HEADS))
        qb_c, kb_c, vb_c, qd_c, kd_c, vd_c, kb32, kd32, vb32, vd32 = _prep(z, 0, N_CTX, gains, bd384, None)
        qb_l, kb_l, vb_l, qd_l, kd_l, vd_l = _prep(z, N_CTX, N_LAT, gains, bd384, rope_q + rope_k)
        sink = win_sink[l][None]
        yb_c, yd_c = _ctx_attn(sink, qb_c, kb_c, vb_c, qd_c, kd_c, vd_c)
        kc = cache_win_k[:, l].reshape(DEC_BATCH, PAST_LEN, WK_B).astype(BF16)
        vc = cache_win_v[:, l].reshape(DEC_BATCH, PAST_LEN, WK_B).astype(BF16)
        yb_l = _win_attn(sink, qb_l, kb_l, vb_l, kc, vc)
        kcd = cache_na_k[:, l].reshape(DEC_BATCH, PAST_LEN, WD).astype(BF16)
        vcd = cache_na_v[:, l].reshape(DEC_BATCH, PAST_LEN, WD).astype(BF16)
        yd_l = _natten(qd_l, kd_l, vd_l, kcd, vcd, _natten_bias(na_rpb[l]))

        new_k.append(kb32.reshape(BATCH, SEQ, WIN_KV, HEAD_DIM))
        new_v.append(vb32.reshape(BATCH, SEQ, WIN_KV, HEAD_DIM))
        new_kd.append(kd32.reshape(BATCH, SEQ, NA_HEADS, HEAD_DIM))
        new_vd.append(vd32.reshape(BATCH, SEQ, NA_HEADS, HEAD_DIM))

        wbr = w_branch[l].astype(BF16)
        x1 = _merge(x_c, x_l, mods, norm1_g[l][None], o_f, o_b, z, tile(hg_onorm[l], HG_HEADS), bd256,
                    yb_c, yb_l, yc_c, yc_l, yd_c, yd_l, w_mgate[l].astype(BF16), b_mgate[l][None],
                    wbr[0:256], wbr[256:640], wbr[640:896], wbr[896:1152], w_out[l].astype(BF16))

        wr = jnp.pad(w_router[l], ((0, 0), (0, E_PAD - N_EXPERTS)))
        br = jnp.pad(b_router[l], (0, E_PAD - N_EXPERTS))[None]
        x_c, x_l = _moe(l, x1, mods, norm2_g[l][None], wr, br, w_eg, w_eu, w_ed,
                        w_sg[l].astype(BF16), w_su[l].astype(BF16), w_sd[l].astype(BF16), tri, ut)

    y_p = x_c.reshape(BATCH, SEQ, D_MODEL)
    y_s = x_l.reshape(DEC_BATCH, DEC_SEQ, D_MODEL)
    return (y_p, y_s, jnp.stack(new_k, axis=1), jnp.stack(new_v, axis=1), jnp.stack(new_kd, axis=1),
            jnp.stack(new_vd, axis=1), jnp.stack(new_s, axis=1))
```

```python
import functools

import numpy as np
import jax
import jax.numpy as jnp
from jax import lax
from jax.experimental import pallas as pl
from jax.experimental.pallas import tpu as pltpu

F32 = jnp.float32
BF16 = jnp.bfloat16

D_MODEL = 1024
BATCH = 16
SEQ = 256
DEPTH = 2
DEC_BATCH = 4
DEC_SEQ = 2048
PAST_LEN = 256
NEG = -1e30
GRID_W = 64
HEAD_DIM = 64
SCALE = HEAD_DIM ** -0.5
ROPE_BASE = 10000.0
EPS = 1e-6
HG_HEADS = 4
HG_W = 256
WIN_HEADS = 6
WIN_KV = 2
WIN = 128
POOL_SIZES = (2, 4, 8, 16)
POOL_W = 256
NA_HEADS = 4
NA_KH = 8
NA_KW = 16
N_EXPERTS = 32
TOP_K = 4
D_EXPERT = 256
ROUTED_SCALE = 2.5

N_CTX = BATCH * SEQ
N_LAT = DEC_BATCH * DEC_SEQ
NTOK = N_CTX + N_LAT
WQ_B = WIN_HEADS * HEAD_DIM
WK_B = WIN_KV * HEAD_DIM
WD = NA_HEADS * HEAD_DIM

Z_W = 3072
COL_AQ, COL_AFF, COL_AFB, COL_AI, COL_AG, COL_CU, COL_DQ, COL_DK, COL_DV = (
    0, 256, 512, 768, 1024, 1280, 1536, 1792, 2048)
COL_BQ, COL_BK, COL_BV = 2304, 2688, 2816

TM = 512
TMP = 2048
TN = 512
HB = 128
VMEM_LIMIT = 56 * 1024 * 1024


def _params(sem, vmem=VMEM_LIMIT):
    return pltpu.CompilerParams(dimension_semantics=sem, vmem_limit_bytes=vmem)


def _seg_of_tile(i, tile):
    nct = N_CTX // tile
    per = DEC_SEQ // tile
    return jnp.where(i < nct, 0, 1 + (i - nct) // per)


def _bd_ones(w):
    idx = np.arange(w) // HEAD_DIM
    return (idx[:, None] == idx[None, :]).astype(np.float32)


def _dot(a, b):
    return jnp.dot(a, b, preferred_element_type=F32)


def _dot_nt(a, b):
    return lax.dot_general(a, b, (((1,), (1,)), ((), ())), preferred_element_type=F32)


def _split_dot(x, w_bf16):
    hi = x.astype(BF16)
    lo = (x - hi.astype(F32)).astype(BF16)
    return _dot(hi, w_bf16) + _dot(lo, w_bf16)


def _head_rms(x, bd, gain):
    ms = _split_dot(x * x, bd) * (1.0 / HEAD_DIM)
    return x * lax.rsqrt(ms + EPS) * gain


def _silu(x):
    return x * jax.nn.sigmoid(x)


def _mod_kernel(c_ref, w_ref, b_ref, o_ref):
    c = c_ref[...]
    a = _silu(c).astype(BF16)
    o_ref[...] = _dot(a, w_ref[...].astype(BF16)) + b_ref[...]


def _modulation(cvec8, w_mod, b_mod):
    n = 6 * D_MODEL
    tn = 1536
    return pl.pallas_call(
        _mod_kernel,
        grid=(DEPTH, n // tn),
        in_specs=[
            pl.BlockSpec((8, D_MODEL), lambda l, j: (0, 0)),
            pl.BlockSpec((None, D_MODEL, tn), lambda l, j: (l, 0, j)),
            pl.BlockSpec((None, 1, tn), lambda l, j: (l, 0, j)),
        ],
        out_specs=pl.BlockSpec((None, 8, tn), lambda l, j: (l, 0, j)),
        out_shape=jax.ShapeDtypeStruct((DEPTH, 8, n), F32),
        compiler_params=_params(("arbitrary", "arbitrary")),
        name="modulation",
    )(cvec8, w_mod, b_mod.reshape(DEPTH, 1, n))


def _prenorm(x, gain, shift, scale):
    ms = jnp.mean(x * x, axis=-1, keepdims=True)
    return x * lax.rsqrt(ms + EPS) * gain * (1.0 + scale) + shift


def _two_group_specs(tile, width, nargs=1):
    nct = N_CTX // tile
    if nargs == 1:
        return [pl.BlockSpec((tile, width), lambda i: (jnp.minimum(i, nct - 1), 0)),
                pl.BlockSpec((tile, width), lambda i: (jnp.maximum(i - nct, 0), 0))]
    return [pl.BlockSpec((tile, width), lambda i, j: (jnp.minimum(i, nct - 1), 0)),
            pl.BlockSpec((tile, width), lambda i, j: (jnp.maximum(i - nct, 0), 0))]


def _pick_group(tile, c_ref, l_ref):
    return jnp.where(pl.program_id(0) < N_CTX // tile, c_ref[...], l_ref[...])


def _proj_kernel(xc_ref, xl_ref, m_ref, g_ref, win_ref, z_ref, h_scr):
    @pl.when(pl.program_id(1) == 0)
    def _():
        x = _pick_group(TMP, xc_ref, xl_ref)
        h_scr[...] = _prenorm(x, g_ref[...], m_ref[0:1, :], m_ref[1:2, :]).astype(BF16)

    z_ref[...] = _dot(h_scr[...], win_ref[...])


def _projection(x_c, x_l, mods, norm_g, w_in_p):
    return pl.pallas_call(
        _proj_kernel,
        grid=(NTOK // TMP, Z_W // TN),
        in_specs=_two_group_specs(TMP, D_MODEL, nargs=2) + [
            pl.BlockSpec((None, 6, D_MODEL), lambda i, j: (_seg_of_tile(i, TMP), 0, 0)),
            pl.BlockSpec((1, D_MODEL), lambda i, j: (0, 0)),
            pl.BlockSpec((D_MODEL, TN), lambda i, j: (0, j)),
        ],
        out_specs=pl.BlockSpec((TMP, TN), lambda i, j: (i, j)),
        out_shape=jax.ShapeDtypeStruct((NTOK, Z_W), F32),
        scratch_shapes=[pltpu.VMEM((TMP, D_MODEL), BF16)],
        compiler_params=_params(("arbitrary", "arbitrary")),
        name="projection",
    )(x_c, x_l, mods, norm_g, w_in_p)


def _hgrn_kernel(qf_ref, ff_ref, vf_ref, qb_ref, fb_ref, vb_ref, lb_ref, s0f_ref, s0b_ref, bd_ref,
                 of_ref, ob_ref, sff_ref, sfb_ref, sf_scr, sb_scr):
    _hgrn_direction(False, qf_ref, ff_ref, vf_ref, lb_ref[0], s0f_ref, bd_ref, of_ref, sff_ref, sf_scr)
    _hgrn_direction(True, qb_ref, fb_ref, vb_ref, lb_ref[1], s0b_ref, bd_ref, ob_ref, sfb_ref, sb_scr)


HSB = 128


def _hgrn_subblock(rev, q, zf, v, lb, s_t, bd):
    n = HSB
    f = lb + (1.0 - lb) * jax.nn.sigmoid(zf)
    lf = jnp.log2(f)
    kk = 1.0 - f

    row = lax.broadcasted_iota(jnp.int32, (n, HG_W), 0)
    tq = lax.broadcasted_iota(jnp.int32, (n, n), 0)
    tk = lax.broadcasted_iota(jnp.int32, (n, n), 1)

    def before(x, m):
        return pltpu.roll(x, (n - m) if rev else m, 0)

    def after(x, m):
        return pltpu.roll(x, m if rev else (n - m), 0)

    q16 = q.astype(BF16)
    k16 = kk.astype(BF16)
    att = [jnp.where(tq == tk, _dot_nt(_hs(q16, h), _hs(k16, h)), 0.0) for h in range(HG_HEADS)]

    tot = lf
    pin = lf
    sex = jnp.zeros_like(lf)
    m = 1
    while m < n:
        late = ((row & (2 * m - 1)) < m) if rev else ((row & (2 * m - 1)) >= m)
        qm = jnp.where(late, q * jnp.exp2(pin), 0.0).astype(BF16)
        km = jnp.where(late, 0.0, kk * jnp.exp2(sex)).astype(BF16)
        shift = (2 * m).bit_length() - 1
        same = (tq >> shift) == (tk >> shift)
        for h in range(HG_HEADS):
            sc = _dot_nt(_hs(qm, h), _hs(km, h))
            att[h] = att[h] + (sc if 2 * m == n else jnp.where(same, sc, 0.0))
        tb = before(tot, m)
        ta = after(tot, m)
        pin = pin + jnp.where(late, tb, 0.0)
        sex = sex + jnp.where(late, 0.0, ta)
        tot = tot + jnp.where(late, tb, ta)
        m *= 2

    lane_head = lax.broadcasted_iota(jnp.int32, (n, HG_W), 1) // HEAD_DIM
    o = _dot_nt((q * jnp.exp2(pin)).astype(BF16), s_t.astype(BF16))
    for h in range(HG_HEADS):
        vh = jnp.where(lane_head == h, v, 0.0).astype(BF16)
        o = o + _dot(att[h].astype(BF16), vh)

    kt = (kk * jnp.exp2(sex)).astype(BF16)
    dec = jnp.exp2(tot[0:1, :])
    s_new = s_t * dec + _dot(v.T.astype(BF16), kt) * bd
    return o, s_new


def _hgrn_direction(rev, q_ref, f_ref, v_ref, lb, s0_ref, bd_ref, o_ref, sfin_ref, s_scr):
    i = pl.program_id(0)
    blk = (pl.num_programs(0) - 1 - i) if rev else i
    nct = N_CTX // HB
    per_c = SEQ // HB
    per_l = DEC_SEQ // HB
    is_ctx = blk < nct
    pos = jnp.where(is_ctx, blk % per_c, (blk - nct) % per_l)
    last = jnp.where(is_ctx, per_c - 1, per_l - 1)
    first_pos = last if rev else 0
    final_pos = 0 if rev else last

    @pl.when(pos == first_pos)
    def _():
        s_scr[...] = jnp.zeros_like(s_scr)

    @pl.when((pos == first_pos) & jnp.logical_not(is_ctx))
    def _():
        for h in range(HG_HEADS):
            hs = slice(HEAD_DIM * h, HEAD_DIM * (h + 1))
            s_scr[hs, hs] = s0_ref[h].T

    bd = bd_ref[...]
    s_new = s_scr[...]
    subs = range(HB // HSB)
    for sb in (reversed(subs) if rev else subs):
        rs = slice(HSB * sb, HSB * (sb + 1))
        o, s_new = _hgrn_subblock(rev, q_ref[rs, :], f_ref[rs, :], v_ref[rs, :], lb, s_new, bd)
        o_ref[rs, :] = o
    s_scr[...] = s_new

    @pl.when((pos == final_pos) & is_ctx)
    def _():
        for h in range(HG_HEADS):
            hs = slice(HEAD_DIM * h, HEAD_DIM * (h + 1))
            sfin_ref[h] = s_new[hs, hs].T


def _hgrn(z, lbs_l, state_hgrn, layer, bd):
    nb = NTOK // HB
    nct = N_CTX // HB
    rblk = lambda i: nb - 1 - i

    def seq_of(b):
        return jnp.where(b < nct, b // (SEQ // HB), BATCH + (b - nct) // (DEC_SEQ // HB))

    col = lambda blk, c: pl.BlockSpec((HB, HG_W), lambda i: (blk(i), c // HG_W))
    state = lambda blk, d: pl.BlockSpec(
        (None, None, None, HG_HEADS, HEAD_DIM, HEAD_DIM),
        lambda i: (jnp.maximum(seq_of(blk(i)) - BATCH, 0), layer, d, 0, 0, 0))
    final = lambda blk: pl.BlockSpec((None, HG_HEADS, HEAD_DIM, HEAD_DIM),
                                     lambda i: (jnp.minimum(seq_of(blk(i)), BATCH - 1), 0, 0, 0))
    fwd = lambda i: i
    return pl.pallas_call(
        _hgrn_kernel,
        grid=(nb,),
        in_specs=[
            col(fwd, COL_AQ), col(fwd, COL_AFF), col(fwd, COL_AI),
            col(rblk, COL_AQ), col(rblk, COL_AFB), col(rblk, COL_AI),
            pl.BlockSpec((2, 1, HG_W), lambda i: (0, 0, 0)),
            state(fwd, 0), state(rblk, 1),
            pl.BlockSpec((HG_W, HG_W), lambda i: (0, 0)),
        ],
        out_specs=[col(fwd, 0), col(rblk, 0), final(fwd), final(rblk)],
        out_shape=[
            jax.ShapeDtypeStruct((NTOK, HG_W), F32),
            jax.ShapeDtypeStruct((NTOK, HG_W), F32),
            jax.ShapeDtypeStruct((BATCH, HG_HEADS, HEAD_DIM, HEAD_DIM), F32),
            jax.ShapeDtypeStruct((BATCH, HG_HEADS, HEAD_DIM, HEAD_DIM), F32),
        ],
        scratch_shapes=[pltpu.VMEM((HG_W, HG_W), F32), pltpu.VMEM((HG_W, HG_W), F32)],
        compiler_params=_params(("arbitrary",)),
        name="hgrn",
    )(z, z, z, z, z, z, lbs_l.reshape(2, 1, HG_W), state_hgrn, state_hgrn, bd)


def _pool_kernel(u_ref, w_ref, sc_ref, o_ref, *, t_len):
    u = u_ref[...]
    row = lax.broadcasted_iota(jnp.int32, (t_len, POOL_W), 0)
    grp = lax.broadcasted_iota(jnp.int32, (t_len, POOL_W), 1) // HEAD_DIM
    half = jnp.left_shift(1, grp)
    acc = jnp.zeros_like(u)
    for j in range(-8, 8):
        src = row + j
        ok = (j >= -half) & (j < half) & (src >= 0) & (src < t_len)
        shifted = u if j == 0 else pltpu.roll(u, (-j) % t_len, 0)
        acc = acc + jnp.where(ok, shifted, 0.0)
    cnt = (jnp.minimum(row + half, t_len) - jnp.maximum(row - half, 0)).astype(F32)
    y = _dot((acc / cnt - u).astype(BF16), w_ref[...]) * sc_ref[...]
    o_ref[...] = y.astype(BF16)


def _pool(z, row0, nseq, t_len, w_bd, scale):
    return pl.pallas_call(
        functools.partial(_pool_kernel, t_len=t_len),
        grid=(nseq,),
        in_specs=[
            pl.BlockSpec((t_len, POOL_W), lambda b: (row0 // t_len + b, COL_CU // POOL_W)),
            pl.BlockSpec((POOL_W, POOL_W), lambda b: (0, 0)),
            pl.BlockSpec((1, POOL_W), lambda b: (0, 0)),
        ],
        out_specs=pl.BlockSpec((t_len, POOL_W), lambda b: (b, 0)),
        out_shape=jax.ShapeDtypeStruct((nseq * t_len, POOL_W), BF16),
        compiler_params=_params(("arbitrary",)),
        name="pool",
    )(z, w_bd, scale)


def _rope(x, cos, sin):
    w = x.shape[-1]
    lane = lax.broadcasted_iota(jnp.int32, x.shape, 1)
    up = pltpu.roll(x, w - 16, 1)
    dn = pltpu.roll(x, 16, 1)
    return x * cos + jnp.where((lane & 31) < 16, up, dn) * sin


def _prep_kernel(*refs, rope):
    if rope:
        (bq, bk, bv, dq, dk, dv, gq, gk, gdq, gdk, bd, cq, sq, ck, sk,
         oq, ok_, ov, odq, odk, odv) = refs
    else:
        (bq, bk, bv, dq, dk, dv, gq, gk, gdq, gdk, bd,
         oq, ok_, ov, odq, odk, odv, ok32, odk32, ov32, odv32) = refs
    bdm = bd[...]
    q = _head_rms(bq[...], bdm, gq[...])
    k = _head_rms(bk[...], bdm[:WK_B, :WK_B], gk[...])
    qd = _head_rms(dq[...], bdm[:WD, :WD], gdq[...])
    kd = _head_rms(dk[...], bdm[:WD, :WD], gdk[...])
    if rope:
        q = _rope(q, cq[...], sq[...])
        k = _rope(k, ck[...], sk[...])
    else:
        ok32[...] = k
        odk32[...] = kd
        ov32[...] = bv[...]
        odv32[...] = dv[...]
    oq[...] = (q * SCALE).astype(BF16)
    ok_[...] = k.astype(BF16)
    ov[...] = bv[...].astype(BF16)
    odq[...] = (qd * SCALE).astype(BF16)
    odk[...] = kd.astype(BF16)
    odv[...] = dv[...].astype(BF16)


def _prep(z, row0, nrows, gains, bd, rope_tabs):
    tm = 512
    nt = nrows // tm
    r0 = row0 // tm
    rope = rope_tabs is not None
    col = lambda c, w: (lambda i: (r0 + i, c // w))
    in_specs = [
        pl.BlockSpec((tm, WQ_B), col(COL_BQ, WQ_B)),
        pl.BlockSpec((tm, WK_B), col(COL_BK, WK_B)),
        pl.BlockSpec((tm, WK_B), col(COL_BV, WK_B)),
        pl.BlockSpec((tm, WD), col(COL_DQ, WD)),
        pl.BlockSpec((tm, WD), col(COL_DK, WD)),
        pl.BlockSpec((tm, WD), col(COL_DV, WD)),
        pl.BlockSpec((1, WQ_B), lambda i: (0, 0)),
        pl.BlockSpec((1, WK_B), lambda i: (0, 0)),
        pl.BlockSpec((1, WD), lambda i: (0, 0)),
        pl.BlockSpec((1, WD), lambda i: (0, 0)),
        pl.BlockSpec((WQ_B, WQ_B), lambda i: (0, 0)),
    ]
    args = [z, z, z, z, z, z, *gains, bd]
    per = DEC_SEQ // tm
    if rope:
        in_specs += [
            pl.BlockSpec((tm, WQ_B), lambda i: (i % per, 0)),
            pl.BlockSpec((tm, WQ_B), lambda i: (i % per, 0)),
            pl.BlockSpec((tm, WK_B), lambda i: (i % per, 0)),
            pl.BlockSpec((tm, WK_B), lambda i: (i % per, 0)),
        ]
        args += list(rope_tabs)
    widths = [WQ_B, WK_B, WK_B, WD, WD, WD]
    out_specs = [pl.BlockSpec((tm, w), lambda i: (i, 0)) for w in widths]
    out_shape = [jax.ShapeDtypeStruct((nrows, w), BF16) for w in widths]
    if not rope:
        out_specs += [pl.BlockSpec((tm, w), lambda i: (i, 0)) for w in (WK_B, WD, WK_B, WD)]
        out_shape += [jax.ShapeDtypeStruct((nrows, w), F32) for w in (WK_B, WD, WK_B, WD)]
    return pl.pallas_call(
        functools.partial(_prep_kernel, rope=rope),
        grid=(nt,),
        in_specs=in_specs,
        out_specs=out_specs,
        out_shape=out_shape,
        compiler_params=_params(("arbitrary",)),
        name="prep_lat" if rope else "prep_ctx",
    )(*args)


def _softmax_pv(scores, values, sink):
    m = scores[0].max(axis=-1, keepdims=True)
    for s in scores[1:]:
        m = jnp.maximum(m, s.max(axis=-1, keepdims=True))
    if sink is not None:
        m = jnp.maximum(m, sink)
    den = jnp.zeros_like(m) if sink is None else jnp.exp(sink - m)
    acc = None
    for s, v in zip(scores, values):
        p = jnp.exp(s - m)
        den = den + p.sum(axis=-1, keepdims=True)
        pv = _dot(p.astype(BF16), v)
        acc = pv if acc is None else acc + pv
    return acc / den


def _hs(x, h):
    return x[:, HEAD_DIM * h:HEAD_DIM * (h + 1)]


GQA = WIN_HEADS // WIN_KV


def _stack_group(q, sink_ref, kv, rows):
    qs = jnp.concatenate([_hs(q, GQA * kv + j) for j in range(GQA)], axis=0)
    part = lax.broadcasted_iota(jnp.int32, (GQA * rows, 1), 0) // rows
    sink = jnp.zeros((GQA * rows, 1), F32)
    for j in range(GQA):
        sink = jnp.where(part == j, sink_ref[0, GQA * kv + j], sink)
    return qs, sink


def _ctx_attn_kernel(sink_ref, q_ref, k_ref, v_ref, qd_ref, kd_ref, vd_ref, ob_ref, od_ref):
    q, k, v = q_ref[...], k_ref[...], v_ref[...]
    for kv in range(WIN_KV):
        qs, sink = _stack_group(q, sink_ref, kv, SEQ)
        s = _dot_nt(qs, _hs(k, kv))
        o = _softmax_pv([s], [_hs(v, kv)], sink)
        for j in range(GQA):
            h = GQA * kv + j
            ob_ref[:, HEAD_DIM * h:HEAD_DIM * (h + 1)] = o[SEQ * j:SEQ * (j + 1)].astype(BF16)
    qd, kd, vd = qd_ref[...], kd_ref[...], vd_ref[...]
    for h in range(NA_HEADS):
        s = _dot_nt(_hs(qd, h), _hs(kd, h))
        o = _softmax_pv([s], [_hs(vd, h)], None)
        od_ref[:, HEAD_DIM * h:HEAD_DIM * (h + 1)] = o.astype(BF16)


def _ctx_attn(sink, q, k, v, qd, kd, vd):
    blk = lambda w: pl.BlockSpec((SEQ, w), lambda b: (b, 0))
    return pl.pallas_call(
        _ctx_attn_kernel,
        grid=(BATCH,),
        in_specs=[pl.BlockSpec(memory_space=pltpu.SMEM),
                  blk(WQ_B), blk(WK_B), blk(WK_B), blk(WD), blk(WD), blk(WD)],
        out_specs=[blk(WQ_B), blk(WD)],
        out_shape=[jax.ShapeDtypeStruct((N_CTX, WQ_B), BF16), jax.ShapeDtypeStruct((N_CTX, WD), BF16)],
        compiler_params=_params(("arbitrary",)),
        name="ctx_attn",
    )(sink, q, k, v, qd, kd, vd)


WIN_SPAN = 3 * WIN


def _win_attn_kernel(sink_ref, q_ref, k_ref, v_ref, kc_ref, vc_ref, o_ref):
    qi = pl.program_id(1)
    start = pl.multiple_of(jnp.clip(qi * WIN - WIN, 0, DEC_SEQ - WIN_SPAN), WIN)
    q = q_ref[...]
    kw = k_ref[pl.ds(start, WIN_SPAN), :]
    vw = v_ref[pl.ds(start, WIN_SPAN), :]
    kc, vc = kc_ref[...], vc_ref[...]
    rows = GQA * WIN
    qpos = qi * WIN + lax.broadcasted_iota(jnp.int32, (rows, WIN_SPAN), 0) % WIN
    kpos = start + lax.broadcasted_iota(jnp.int32, (rows, WIN_SPAN), 1)
    valid = jnp.abs(qpos - kpos) <= WIN
    for kv in range(WIN_KV):
        qs, sink = _stack_group(q, sink_ref, kv, WIN)
        s_loc = jnp.where(valid, _dot_nt(qs, _hs(kw, kv)), NEG)
        s_ctx = _dot_nt(qs, _hs(kc, kv))
        o = _softmax_pv([s_loc, s_ctx], [_hs(vw, kv), _hs(vc, kv)], sink)
        for j in range(GQA):
            h = GQA * kv + j
            o_ref[:, HEAD_DIM * h:HEAD_DIM * (h + 1)] = o[WIN * j:WIN * (j + 1)].astype(BF16)


def _win_attn(sink, q, k, v, kc, vc):
    nq = DEC_SEQ // WIN
    return pl.pallas_call(
        _win_attn_kernel,
        grid=(DEC_BATCH, nq),
        in_specs=[
            pl.BlockSpec(memory_space=pltpu.SMEM),
            pl.BlockSpec((WIN, WQ_B), lambda b, i: (b * nq + i, 0)),
            pl.BlockSpec((None, DEC_SEQ, WK_B), lambda b, i: (b, 0, 0)),
            pl.BlockSpec((None, DEC_SEQ, WK_B), lambda b, i: (b, 0, 0)),
            pl.BlockSpec((None, PAST_LEN, WK_B), lambda b, i: (b, 0, 0)),
            pl.BlockSpec((None, PAST_LEN, WK_B), lambda b, i: (b, 0, 0)),
        ],
        out_specs=pl.BlockSpec((WIN, WQ_B), lambda b, i: (b * nq + i, 0)),
        out_shape=jax.ShapeDtypeStruct((N_LAT, WQ_B), BF16),
        compiler_params=_params(("arbitrary", "arbitrary")),
        name="win_attn",
    )(sink, q, k.reshape(DEC_BATCH, DEC_SEQ, WK_B), v.reshape(DEC_BATCH, DEC_SEQ, WK_B), kc, vc)


NA_ROWS = DEC_SEQ // GRID_W
NA_G = 4
NA_NG = NA_ROWS // NA_G
NA_UROWS = NA_KH + NA_G - 1
NA_UKEYS = NA_UROWS * GRID_W
NA_QROWS = NA_G * GRID_W


def _na_union_start(g):
    return jnp.clip(g * NA_G - NA_KH // 2, 0, NA_ROWS - NA_UROWS)


def _natten_kernel(q_ref, k_ref, v_ref, kc_ref, vc_ref, bias_ref, o_ref):
    start = pl.multiple_of(_na_union_start(pl.program_id(1)) * GRID_W, GRID_W)
    q = q_ref[...]
    kw = k_ref[pl.ds(start, NA_UKEYS), :]
    vw = v_ref[pl.ds(start, NA_UKEYS), :]
    kc, vc = kc_ref[...], vc_ref[...]
    for h in range(NA_HEADS):
        qh = _hs(q, h)
        s_loc = _dot_nt(qh, _hs(kw, h)) + bias_ref[h]
        s_ctx = _dot_nt(qh, _hs(kc, h))
        o = _softmax_pv([s_loc, s_ctx], [_hs(vw, h), _hs(vc, h)], None)
        o_ref[:, HEAD_DIM * h:HEAD_DIM * (h + 1)] = o.astype(BF16)


def _natten(q, k, v, kc, vc, bias):
    variant = lambda g: jnp.where(g == 0, 0, jnp.where(g == NA_NG - 1, 2, 1))
    return pl.pallas_call(
        _natten_kernel,
        grid=(DEC_BATCH, NA_NG),
        in_specs=[
            pl.BlockSpec((NA_QROWS, WD), lambda b, g: (b * NA_NG + g, 0)),
            pl.BlockSpec((None, DEC_SEQ, WD), lambda b, g: (b, 0, 0)),
            pl.BlockSpec((None, DEC_SEQ, WD), lambda b, g: (b, 0, 0)),
            pl.BlockSpec((None, PAST_LEN, WD), lambda b, g: (b, 0, 0)),
            pl.BlockSpec((None, PAST_LEN, WD), lambda b, g: (b, 0, 0)),
            pl.BlockSpec((None, NA_HEADS, NA_QROWS, NA_UKEYS), lambda b, g: (variant(g), 0, 0, 0)),
        ],
        out_specs=pl.BlockSpec((NA_QROWS, WD), lambda b, g: (b * NA_NG + g, 0)),
        out_shape=jax.ShapeDtypeStruct((N_LAT, WD), BF16),
        compiler_params=_params(("arbitrary", "arbitrary")),
        name="natten",
    )(q, k.reshape(DEC_BATCH, DEC_SEQ, WD), v.reshape(DEC_BATCH, DEC_SEQ, WD), kc, vc, bias)


def _natten_bias(rpb):
    c = np.arange(GRID_W)
    cstart = np.clip(c - NA_KW // 2, 0, GRID_W - NA_KW)
    inwin = (c[None, :] >= cstart[:, None]) & (c[None, :] < cstart[:, None] + NA_KW)
    dc = np.clip(c[None, :] - c[:, None] + NA_KW - 1, 0, 2 * NA_KW - 2)
    pick = (dc[None] == np.arange(2 * NA_KW - 1)[:, None, None]).astype(np.float32)
    toep = jnp.einsum('hdj,jck->hdck', rpb.astype(F32), jnp.asarray(pick),
                      precision=lax.Precision.HIGHEST)
    toep = jnp.where(inwin[None, None], toep, NEG)
    nd = 2 * NA_KH - 1
    toep = jnp.concatenate([toep, jnp.full((NA_HEADS, 1, GRID_W, GRID_W), NEG, F32)], axis=1)
    sel = np.zeros((3, NA_G, NA_UROWS, nd + 1), np.float32)
    for vi, g in enumerate((0, 1, NA_NG - 1)):
        u0 = int(np.clip(g * NA_G - NA_KH // 2, 0, NA_ROWS - NA_UROWS))
        for j in range(NA_G):
            r = g * NA_G + j
            w0 = int(np.clip(r - NA_KH // 2, 0, NA_ROWS - NA_KH))
            for i in range(NA_UROWS):
                kr = u0 + i
                sel[vi, j, i, (kr - r + NA_KH - 1) if w0 <= kr < w0 + NA_KH else nd] = 1.0
    bias = jnp.einsum('vjid,hdck->vhjcik', jnp.asarray(sel), toep, precision=lax.Precision.HIGHEST)
    return bias.reshape(3, NA_HEADS, NA_QROWS, NA_UKEYS)


def _merge_kernel(xc_ref, xl_ref, m_ref, g_ref, of_ref, ob_ref, ag_ref, on_ref, bd_ref,
                  ybc_ref, ybl_ref, ycc_ref, ycl_ref, ydc_ref, ydl_ref,
                  wg_ref, bg_ref, wa_ref, wb_ref, wc_ref, wd_ref, wo_ref,
                  g2_ref, wr_ref, br_ref, wsg_ref, wsu_ref, wsd_ref, tri_ref, ut_ref,
                  xa_ref, sl_ref, slt_ref, cnt_ref, xs_ref):
    x = _pick_group(TM, xc_ref, xl_ref)
    h = _prenorm(x, g_ref[...], m_ref[0:1, :], m_ref[1:2, :]).astype(BF16)
    o = of_ref[...] + ob_ref[...]
    ya = _head_rms(o, bd_ref[...], on_ref[...]) * _silu(ag_ref[...])
    yb = _pick_group(TM, ybc_ref, ybl_ref)
    yc = _pick_group(TM, ycc_ref, ycl_ref)
    yd = _pick_group(TM, ydc_ref, ydl_ref)
    d = D_MODEL
    merged = jnp.zeros((TM, d), F32)
    branches = ((ya.astype(BF16), wa_ref), (yb, wb_ref), (yc, wc_ref), (yd, wd_ref))
    for b, (y, w_ref) in enumerate(branches):
        gate = jax.nn.sigmoid(_dot(h, wg_ref[:, d * b:d * (b + 1)]) + bg_ref[:, d * b:d * (b + 1)])
        merged = merged + gate * _dot(y, w_ref[...])
    x1 = x + m_ref[2:3, :] * _dot(merged.astype(BF16), wo_ref[...])
    for t in range(TM // TR):
        _route_rows(x1[TR * t:TR * (t + 1)], t, m_ref, g2_ref, wr_ref, br_ref, wsg_ref, wsu_ref, wsd_ref,
                    tri_ref, ut_ref, xa_ref, sl_ref, slt_ref, cnt_ref, xs_ref)


def _merge(x_c, x_l, mods, norm_g, o_f, o_b, z, onorm, bd, yb_c, yb_l, yc_c, yc_l, yd_c, yd_l,
           w_gate, b_gate, wa, wb, wc, wd, wo, norm2_g, w_router, b_router, wsg, wsu, wsd, tri, ut):
    full = lambda a: pl.BlockSpec(a.shape, lambda i: (0, 0))
    tpm = TM // TR
    return pl.pallas_call(
        _merge_kernel,
        grid=(NTOK // TM,),
        in_specs=_two_group_specs(TM, D_MODEL) + [
            pl.BlockSpec((None, 6, D_MODEL), lambda i: (_seg_of_tile(i, TM), 0, 0)),
            full(norm_g),
            pl.BlockSpec((TM, HG_W), lambda i: (i, 0)),
            pl.BlockSpec((TM, HG_W), lambda i: (i, 0)),
            pl.BlockSpec((TM, HG_W), lambda i: (i, COL_AG // HG_W)),
            full(onorm), full(bd),
        ] + _two_group_specs(TM, WQ_B) + _two_group_specs(TM, POOL_W) + _two_group_specs(TM, WD) + [
            full(w_gate), full(b_gate), full(wa), full(wb), full(wc), full(wd), full(wo),
            full(norm2_g), full(w_router), full(b_router), full(wsg), full(wsu), full(wsd), full(tri), full(ut),
        ],
        out_specs=[
            pl.BlockSpec((TM, XW), lambda i: (i, 0)),
            pl.BlockSpec((TM, E_PAD), lambda i: (i, 0)),
            pl.BlockSpec((tpm, 8, TR), lambda i: (i, 0, 0)),
            pl.BlockSpec((tpm, 1, E_PAD), lambda i: (i, 0, 0)),
            pl.BlockSpec((TM, D_MODEL), lambda i: (i, 0)),
        ],
        out_shape=[
            jax.ShapeDtypeStruct((NTOK, XW), BF16),
            jax.ShapeDtypeStruct((NTOK, E_PAD), F32),
            jax.ShapeDtypeStruct((NT_R, 8, TR), F32),
            jax.ShapeDtypeStruct((NT_R, 1, E_PAD), F32),
            jax.ShapeDtypeStruct((NTOK, D_MODEL), F32),
        ],
        compiler_params=_params(("arbitrary",)),
        name="merge",
    )(x_c, x_l, mods, norm_g, o_f, o_b, z, onorm, bd, yb_c, yb_l, yc_c, yc_l, yd_c, yd_l,
      w_gate, b_gate, wa, wb, wc, wd, wo, norm2_g, w_router, b_router, wsg, wsu, wsd, tri, ut)


E_PAD = 128
TR = 256
RT = 512
SLAB = 16
NT_R = NTOK // TR
S_LOC = 1536
S_MAX = NTOK * TOP_K + NT_R * N_EXPERTS * (SLAB - 1) + N_EXPERTS * (RT - SLAB)
N_XT = S_MAX // RT
XW = D_MODEL + 2 * E_PAD
assert S_LOC >= TR * TOP_K + N_EXPERTS * (SLAB - 1) and S_LOC % 128 == 0 and S_MAX % RT == 0


def _route_rows(x, t, m_ref, g_ref, wr_ref, br_ref, wsg_ref, wsu_ref, wsd_ref, tri_ref, ut_ref,
                xa_ref, sl_ref, slt_ref, cnt_ref, xs_ref):
    rows = slice(TR * t, TR * (t + 1))
    lane = lax.broadcasted_iota(jnp.int32, (TR, E_PAD), 1)
    ms = jnp.mean(x * x, axis=-1, keepdims=True)
    y = x * lax.rsqrt(ms + EPS) * g_ref[...]
    h = y * (1.0 + m_ref[4:5, :]) + m_ref[3:4, :]
    h_hi = h.astype(BF16)
    h_lo = (h - h_hi.astype(F32)).astype(BF16)
    w = wr_ref[...]
    w_hi = w.astype(BF16)
    w_lo = (w - w_hi.astype(F32)).astype(BF16)
    logits = _dot(h_hi, w_hi) + _dot(h_hi, w_lo) + _dot(h_lo, w_hi)
    scores = jax.nn.sigmoid(logits)
    sel = jnp.where(lane < N_EXPERTS, scores + br_ref[...], -jnp.inf)
    picked = jnp.zeros((TR, E_PAD), F32)
    hot = jnp.zeros((TR, E_PAD), F32)
    idxs = []
    lane_f = lane.astype(F32)
    for _ in range(TOP_K):
        mx = sel.max(axis=-1, keepdims=True)
        idx = jnp.min(jnp.where(sel == mx, lane_f, float(E_PAD)), axis=-1, keepdims=True)
        hit = lane_f == idx
        picked = jnp.where(hit, scores, picked)
        hot = jnp.where(hit, 1.0, hot)
        sel = jnp.where(hit, -jnp.inf, sel)
        idxs.append(idx)
    wts = ROUTED_SCALE * picked / picked.sum(axis=-1, keepdims=True)

    cnt = hot.sum(axis=0, keepdims=True)
    pad = jnp.floor((cnt + (SLAB - 1.0)) * (1.0 / SLAB)) * SLAB
    loc = _dot(jnp.broadcast_to(pad, (8, E_PAD)).astype(BF16), ut_ref[...])[0:1, :]
    rank = _dot(tri_ref[...], hot.astype(BF16))
    slotmat = loc + rank
    sl = jnp.zeros((TR, E_PAD), F32)
    for k in range(TOP_K):
        s_k = jnp.sum(jnp.where(lane_f == idxs[k], slotmat, 0.0), axis=-1, keepdims=True)
        sl = jnp.where(lane == k, s_k, sl)
    sl_ref[rows, :] = sl
    slt_ref[t] = sl.T[0:8, :]
    cnt_ref[t] = pad

    w16 = wts.astype(BF16)
    xa_ref[rows, 0:D_MODEL] = h_hi
    xa_ref[rows, D_MODEL:D_MODEL + E_PAD] = w16
    xa_ref[rows, D_MODEL + E_PAD:XW] = (wts - w16.astype(F32)).astype(BF16)

    a = _silu(_dot(h_hi, wsg_ref[...])) * _dot(h_hi, wsu_ref[...])
    xs_ref[rows, :] = x + m_ref[5:6, :] * _dot(a.astype(BF16), wsd_ref[...])


def _slab_copy(src, src_row, dst, dst_row, sem):
    hint = lambda r: r if isinstance(r, int) else pl.multiple_of(r, SLAB)
    return pltpu.make_async_copy(src.at[pl.ds(hint(src_row), SLAB), :],
                                 dst.at[pl.ds(hint(dst_row), SLAB), :], sem)


NSL = S_LOC // SLAB


def _for_each_slab(grow_ref, ns_ref, t, fn):
    def per_slab(n, c):
        fn(n * SLAB, grow_ref[t * NSL + n])
        return c

    lax.fori_loop(0, ns_ref[t], per_slab, 0)


def _compact_kernel(grow_ref, ns_ref, nd_ref, toff_ref, tn_ref, nu_ref,
                    xa_ref, slt_ref, xs_hbm, xc_scr, zero_scr, sem):
    t = pl.program_id(0)
    slot = t % 2

    def wait_n(n, s):
        def body(_, c):
            _slab_copy(xc_scr.at[s], 0, xs_hbm, 0, sem.at[s]).wait()
            return c
        lax.fori_loop(0, n, body, 0)

    def unused_tile_copy(j):
        row = pl.multiple_of((nu_ref[0] + j) * RT, RT)
        return pltpu.make_async_copy(zero_scr, xs_hbm.at[pl.ds(row, RT), :], sem.at[2])

    @pl.when(t == 0)
    def _():
        zero_scr[...] = jnp.zeros_like(zero_scr)

        def body(j, c):
            unused_tile_copy(j).start()
            return c
        lax.fori_loop(0, N_XT - nu_ref[0], body, 0)

    @pl.when(t >= 2)
    def _():
        wait_n(nd_ref[t - 2], slot)

    row = lax.broadcasted_iota(jnp.int32, (S_LOC, TR), 0)
    slt = slt_ref[...].astype(jnp.int32)
    hit = row == slt[0:1, :]
    for k in range(1, TOP_K):
        hit = hit | (row == slt[k:k + 1, :])
    onehot = jnp.where(hit, 1.0, 0.0).astype(BF16)
    xc_scr[slot] = _dot(onehot, xa_ref[...]).astype(BF16)

    _for_each_slab(grow_ref, ns_ref, t,
                   lambda lr, gr: _slab_copy(xc_scr.at[slot], lr, xs_hbm, gr, sem.at[slot]).start())

    @pl.when(t < N_EXPERTS)
    def _():
        def body(s, c):
            _slab_copy(zero_scr, 0, xs_hbm, toff_ref[t] + s * SLAB, sem.at[slot]).start()
            return c
        lax.fori_loop(0, tn_ref[t], body, 0)

    @pl.when(t == NT_R - 1)
    def _():
        wait_n(nd_ref[t], slot)
        wait_n(nd_ref[t - 1], 1 - slot)

        def body(j, c):
            unused_tile_copy(j).wait()
            return c
        lax.fori_loop(0, N_XT - nu_ref[0], body, 0)


def _compact(meta, xa, slt):
    grid_spec = pltpu.PrefetchScalarGridSpec(
        num_scalar_prefetch=6,
        grid=(NT_R,),
        in_specs=[
            pl.BlockSpec((TR, XW), lambda i, *_: (i, 0)),
            pl.BlockSpec((None, 8, TR), lambda i, *_: (i, 0, 0)),
        ],
        out_specs=pl.BlockSpec(memory_space=pl.ANY),
        scratch_shapes=[pltpu.VMEM((2, S_LOC, XW), BF16), pltpu.VMEM((RT, XW), BF16),
                        pltpu.SemaphoreType.DMA((3,))],
    )
    return pl.pallas_call(
        _compact_kernel,
        grid_spec=grid_spec,
        out_shape=jax.ShapeDtypeStruct((S_MAX, XW), BF16),
        compiler_params=_params(("arbitrary",)),
        name="compact",
    )(meta['grow'], meta['ns'], meta['nd'], meta['toff'], meta['tn'], meta['nu'], xa, slt)


def _expert_kernel(te_ref, ti_ref, nu_ref, nx_ref, par_ref, xs_ref, wg_hbm, wu_hbm, wd_hbm, y_ref,
                   wg_f, wu_f, wd_f, wgu_s, wd_s, sem, *, layer):
    i = pl.program_id(0)
    e = te_ref[i]
    slot = par_ref[i]
    used = i < nu_ref[0]
    first = used & ((i == 0) | (e != te_ref[jnp.maximum(i - 1, 0)]))

    def weight_copies(expert, s):
        return (pltpu.make_async_copy(wg_hbm.at[layer, expert], wg_f.at[s], sem.at[s]),
                pltpu.make_async_copy(wu_hbm.at[layer, expert], wu_f.at[s], sem.at[s]),
                pltpu.make_async_copy(wd_hbm.at[layer, expert], wd_f.at[s], sem.at[s]))

    @pl.when(i == 0)
    def _():
        for c in weight_copies(e, slot):
            c.start()

    @pl.when(first)
    def _():
        for c in weight_copies(e, slot):
            c.wait()
        wgu_s[:, 0:D_EXPERT] = wg_f[slot].astype(BF16)
        wgu_s[:, D_EXPERT:2 * D_EXPERT] = wu_f[slot].astype(BF16)
        wd_s[...] = wd_f[slot].astype(BF16)

    @pl.when(first & (nx_ref[i] >= 0))
    def _():
        for c in weight_copies(nx_ref[i], 1 - slot):
            c.start()

    @pl.when(used)
    def _():
        x = xs_ref[:, 0:D_MODEL]
        gw = xs_ref[:, D_MODEL:D_MODEL + E_PAD].astype(F32) + xs_ref[:, D_MODEL + E_PAD:XW].astype(F32)
        lane = lax.broadcasted_iota(jnp.int32, (RT, E_PAD), 1)
        ge = jnp.sum(jnp.where(lane == e, gw, 0.0), axis=-1, keepdims=True)
        gu = _dot(x, wgu_s[...])
        a = _silu(gu[:, 0:D_EXPERT]) * gu[:, D_EXPERT:2 * D_EXPERT]
        y_ref[...] = _dot((a * ge).astype(BF16), wd_s[...]).astype(BF16)

    @pl.when(i >= nu_ref[0])
    def _():
        y_ref[...] = jnp.zeros_like(y_ref)


def _experts(meta, layer, xs, w_eg, w_eu, w_ed):
    any_spec = pl.BlockSpec(memory_space=pl.ANY)
    grid_spec = pltpu.PrefetchScalarGridSpec(
        num_scalar_prefetch=5,
        grid=(N_XT,),
        in_specs=[pl.BlockSpec((RT, XW), lambda i, te, ti, *_: (ti[i], 0)), any_spec, any_spec, any_spec],
        out_specs=pl.BlockSpec((RT, D_MODEL), lambda i, *_: (i, 0)),
        scratch_shapes=[pltpu.VMEM((2, D_MODEL, D_EXPERT), F32), pltpu.VMEM((2, D_MODEL, D_EXPERT), F32),
                        pltpu.VMEM((2, D_EXPERT, D_MODEL), F32),
                        pltpu.VMEM((D_MODEL, 2 * D_EXPERT), BF16), pltpu.VMEM((D_EXPERT, D_MODEL), BF16),
                        pltpu.SemaphoreType.DMA((2,))],
    )
    return pl.pallas_call(
        functools.partial(_expert_kernel, layer=layer),
        grid_spec=grid_spec,
        out_shape=jax.ShapeDtypeStruct((S_MAX, D_MODEL), BF16),
        compiler_params=_params(("arbitrary",)),
        name="experts",
    )(meta['te'], meta['ti'], meta['nu'], meta['nx'], meta['par'], xs, w_eg, w_eu, w_ed)


def _combine_kernel(grow_ref, ns_ref, xs_ref, m_ref, sl_ref, y_hbm, oc_ref, ol_ref, yc_scr, sem):
    t = pl.program_id(0)
    slot = t % 2

    def issue(tt, s):
        _for_each_slab(grow_ref, ns_ref, tt,
                       lambda lr, gr: _slab_copy(y_hbm, gr, yc_scr.at[s], lr, sem.at[s]).start())

    @pl.when(t == 0)
    def _():
        yc_scr[...] = jnp.zeros_like(yc_scr)
        issue(0, 0)

    @pl.when(t + 1 < NT_R)
    def _():
        issue(t + 1, 1 - slot)

    def wait_body(_, c):
        _slab_copy(y_hbm, 0, yc_scr.at[slot], 0, sem.at[slot]).wait()
        return c
    lax.fori_loop(0, ns_ref[t], wait_body, 0)

    col = lax.broadcasted_iota(jnp.int32, (TR, S_LOC), 1)
    sl = sl_ref[...].astype(jnp.int32)
    hit = col == sl[:, 0:1]
    for k in range(1, TOP_K):
        hit = hit | (col == sl[:, k:k + 1])
    onehot = jnp.where(hit, 1.0, 0.0).astype(BF16)
    out = xs_ref[...] + m_ref[5:6, :] * _dot(onehot, yc_scr[slot])

    @pl.when(t < N_CTX // TR)
    def _():
        oc_ref[...] = out

    @pl.when(t >= N_CTX // TR)
    def _():
        ol_ref[...] = out


def _combine(meta, xsh, mods, sl, y):
    nct = N_CTX // TR
    out_specs = [pl.BlockSpec((TR, D_MODEL), lambda i, *_: (jnp.minimum(i, nct - 1), 0)),
                 pl.BlockSpec((TR, D_MODEL), lambda i, *_: (jnp.maximum(i - nct, 0), 0))]
    out_shape = [jax.ShapeDtypeStruct((N_CTX, D_MODEL), F32), jax.ShapeDtypeStruct((N_LAT, D_MODEL), F32)]
    grid_spec = pltpu.PrefetchScalarGridSpec(
        num_scalar_prefetch=2,
        grid=(NT_R,),
        in_specs=[
            pl.BlockSpec((TR, D_MODEL), lambda i, *_: (i, 0)),
            pl.BlockSpec((None, 6, D_MODEL), lambda i, *_: (_seg_of_tile(i, TR), 0, 0)),
            pl.BlockSpec((TR, E_PAD), lambda i, *_: (i, 0)),
            pl.BlockSpec(memory_space=pl.ANY),
        ],
        out_specs=out_specs,
        scratch_shapes=[pltpu.VMEM((2, S_LOC, D_MODEL), BF16), pltpu.SemaphoreType.DMA((2,))],
    )
    return pl.pallas_call(
        _combine_kernel,
        grid_spec=grid_spec,
        out_shape=out_shape,
        compiler_params=_params(("arbitrary",)),
        name="combine",
    )(meta['grow'], meta['ns'], xsh, mods, sl, y)


def _route_meta(cnt):
    pc = cnt[:, 0, :N_EXPERTS].astype(jnp.int32)
    tot = pc.sum(axis=0)
    tot_pad = ((tot + RT - 1) // RT) * RT
    ends = jnp.cumsum(tot_pad)
    base = ends - tot_pad
    dst = base[None, :] + jnp.cumsum(pc, axis=0) - pc
    cum = jnp.cumsum(pc, axis=1)
    ns = cum[:, -1] // SLAB
    lrow = jnp.arange(NSL, dtype=jnp.int32)[None, :, None] * SLAB
    owner = jnp.minimum(jnp.sum(cum[:, None, :] <= lrow, axis=2), N_EXPERTS - 1)
    mine = owner[:, :, None] == jnp.arange(N_EXPERTS, dtype=jnp.int32)[None, None, :]
    grow = jnp.sum(jnp.where(mine, (dst - cum + pc)[:, None, :], 0), axis=2) + lrow[:, :, 0]
    tn = (tot_pad - tot) // SLAB
    nd = ns + jnp.pad(tn, (0, NT_R - N_EXPERTS))
    n_used = ends[-1] // RT
    ti = jnp.minimum(jnp.arange(N_XT, dtype=jnp.int32), n_used - 1)
    te = jnp.minimum(jnp.sum(ends[None, :] <= (ti * RT)[:, None], axis=1), N_EXPERTS - 1)
    eidx = jnp.arange(N_EXPERTS, dtype=jnp.int32)
    has = tot_pad > 0
    later = has[None, :] & (eidx[None, :] > eidx[:, None])
    nxt_e = jnp.min(jnp.where(later, eidx[None, :], N_EXPERTS), axis=1)
    nxt_e = jnp.where(nxt_e == N_EXPERTS, -1, nxt_e)
    ordinal = jnp.cumsum(has.astype(jnp.int32)) - 1
    is_e = te[:, None] == eidx[None, :]
    nx = jnp.sum(jnp.where(is_e, nxt_e[None, :], 0), axis=1)
    par = jnp.sum(jnp.where(is_e, ordinal[None, :], 0), axis=1) & 1
    i32 = lambda a: a.astype(jnp.int32)
    return dict(grow=i32(grow.reshape(-1)), ns=i32(ns), nd=i32(nd), toff=i32(base + tot), tn=i32(tn),
                te=i32(te), ti=i32(ti), nu=i32(n_used.reshape(1)), nx=i32(nx), par=i32(par))


def _moe_routed(layer, routed, mods, w_eg, w_eu, w_ed):
    xa, sl, slt, cnt, xsh = routed
    meta = _route_meta(cnt)
    xs = _compact(meta, xa, slt)
    y = _experts(meta, layer, xs, w_eg, w_eu, w_ed)
    return _combine(meta, xsh, mods, sl, y)


def _rope_tables(width):
    t = np.arange(DEC_SEQ)
    quarter = HEAD_DIM // 4
    inv = (ROPE_BASE ** (-np.arange(quarter) / quarter)).astype(np.float32)
    ang_r = (t // GRID_W).astype(np.float32)[:, None] * inv[None]
    ang_c = (t % GRID_W).astype(np.float32)[:, None] * inv[None]
    cos = np.concatenate([np.cos(ang_r), np.cos(ang_r), np.cos(ang_c), np.cos(ang_c)], axis=1)
    sin = np.concatenate([-np.sin(ang_r), np.sin(ang_r), -np.sin(ang_c), np.sin(ang_c)], axis=1)
    reps = width // HEAD_DIM
    return (jnp.asarray(np.tile(cos, (1, reps)), F32), jnp.asarray(np.tile(sin, (1, reps)), F32))


def _permute_w_in(w):
    a = w[:, 0:1280]
    bq, bk, bv = w[:, 1280:1664], w[:, 1664:1792], w[:, 1792:1920]
    cu = w[:, 1920:2176]
    d = w[:, 2176:2944]
    pad = jnp.zeros((w.shape[0], Z_W - 2944), w.dtype)
    return jnp.concatenate([a, cu, d, bq, bk, bv, pad], axis=1)


def _block_diag(blocks):
    g = blocks.shape[0]
    eye = jnp.eye(g, dtype=blocks.dtype)
    return jnp.einsum('gh,gij->gihj', eye, blocks).reshape(g * HEAD_DIM, g * HEAD_DIM)


def kernel(x_prompt, x_sample, cache_win_k, cache_win_v, cache_na_k, cache_na_v, state_hgrn, c, c_ctx, w_mod, b_mod, norm1_g, norm2_g, w_in, w_mgate, b_mgate, hg_lb, hg_onorm, win_qn, win_kn, win_sink, pool_w, pool_scale, na_qn, na_kn, na_rpb, w_branch, w_out, w_router, b_router, w_eg, w_eu, w_ed, w_sg, w_su, w_sd):
    lbp = jax.nn.softmax(hg_lb.astype(F32), axis=0)
    lbs = jnp.cumsum(lbp, axis=0) - lbp[0:1]

    cvec8 = jnp.concatenate([c_ctx[None], c, jnp.zeros((3, D_MODEL), F32)], axis=0)
    mods_all = _modulation(cvec8, w_mod, b_mod).reshape(DEPTH, 8, 6, D_MODEL)

    bd384 = jnp.asarray(_bd_ones(WQ_B), BF16)
    bd256 = bd384[:HG_W, :HG_W]
    bd256_f32 = jnp.asarray(_bd_ones(HG_W), F32)
    rope_q = _rope_tables(WQ_B)
    rope_k = _rope_tables(WK_B)
    tile = lambda g, reps: jnp.tile(g, reps)[None, :]
    tri = jnp.asarray(np.tril(np.ones((TR, TR), np.float32), -1), BF16)
    ut = jnp.asarray(np.triu(np.ones((E_PAD, E_PAD), np.float32), 1), BF16)

    x_c, x_l = x_prompt.reshape(N_CTX, D_MODEL), x_sample.reshape(N_LAT, D_MODEL)
    new_k, new_v, new_kd, new_vd, new_s = [], [], [], [], []
    for l in range(DEPTH):
        mods = mods_all[l]
        z = _projection(x_c, x_l, mods, norm1_g[l][None], _permute_w_in(w_in[l]).astype(BF16))

        o_f, o_b, sfin_f, sfin_b = _hgrn(z, lbs[l], state_hgrn.astype(F32), l, bd256_f32)
        new_s.append(jnp.stack([sfin_f, sfin_b], axis=1))

        w_pool = _block_diag(pool_w[l]).astype(BF16)
        yc_c = _pool(z, 0, BATCH, SEQ, w_pool, pool_scale[l][None])
        yc_l = _pool(z, N_CTX, DEC_BATCH, DEC_SEQ, w_pool, pool_scale[l][None])

        gains = (tile(win_qn[l], WIN_HEADS), tile(win_kn[l], WIN_KV), tile(na_qn[l], NA_HEADS), tile(na_kn[l], NA_HEADS))
        qb_c, kb_c, vb_c, qd_c, kd_c, vd_c, kb32, kd32, vb32, vd32 = _prep(z, 0, N_CTX, gains, bd384, None)
        qb_l, kb_l, vb_l, qd_l, kd_l, vd_l = _prep(z, N_CTX, N_LAT, gains, bd384, rope_q + rope_k)
        sink = win_sink[l][None]
        yb_c, yd_c = _ctx_attn(sink, qb_c, kb_c, vb_c, qd_c, kd_c, vd_c)
        kc = cache_win_k[:, l].reshape(DEC_BATCH, PAST_LEN, WK_B).astype(BF16)
        vc = cache_win_v[:, l].reshape(DEC_BATCH, PAST_LEN, WK_B).astype(BF16)
        yb_l = _win_attn(sink, qb_l, kb_l, vb_l, kc, vc)
        kcd = cache_na_k[:, l].reshape(DEC_BATCH, PAST_LEN, WD).astype(BF16)
        vcd = cache_na_v[:, l].reshape(DEC_BATCH, PAST_LEN, WD).astype(BF16)
        yd_l = _natten(qd_l, kd_l, vd_l, kcd, vcd, _natten_bias(na_rpb[l]))

        new_k.append(kb32.reshape(BATCH, SEQ, WIN_KV, HEAD_DIM))
        new_v.append(vb32.reshape(BATCH, SEQ, WIN_KV, HEAD_DIM))
        new_kd.append(kd32.reshape(BATCH, SEQ, NA_HEADS, HEAD_DIM))
        new_vd.append(vd32.reshape(BATCH, SEQ, NA_HEADS, HEAD_DIM))

        wbr = w_branch[l].astype(BF16)
        wr = jnp.pad(w_router[l], ((0, 0), (0, E_PAD - N_EXPERTS)))
        br = jnp.pad(b_router[l], (0, E_PAD - N_EXPERTS))[None]
        routed = _merge(x_c, x_l, mods, norm1_g[l][None], o_f, o_b, z, tile(hg_onorm[l], HG_HEADS), bd256,
                        yb_c, yb_l, yc_c, yc_l, yd_c, yd_l, w_mgate[l].astype(BF16), b_mgate[l][None],
                        wbr[0:256], wbr[256:640], wbr[640:896], wbr[896:1152], w_out[l].astype(BF16),
                        norm2_g[l][None], wr, br, w_sg[l].astype(BF16), w_su[l].astype(BF16),
                        w_sd[l].astype(BF16), tri, ut)
        x_c, x_l = _moe_routed(l, routed, mods, w_eg, w_eu, w_ed)

    y_p = x_c.reshape(BATCH, SEQ, D_MODEL)
    y_s = x_l.reshape(DEC_BATCH, DEC_SEQ, D_MODEL)
    return (y_p, y_s, jnp.stack(new_k, axis=1), jnp.stack(new_v, axis=1), jnp.stack(new_kd, axis=1),
            jnp.stack(new_vd, axis=1), jnp.stack(new_s, axis=1))
```

```python
import functools

import numpy as np
import jax
import jax.numpy as jnp
from jax import lax
from jax.experimental import pallas as pl
from jax.experimental.pallas import tpu as pltpu

F32 = jnp.float32
BF16 = jnp.bfloat16

D_MODEL = 1024
BATCH = 16
SEQ = 256
DEPTH = 2
DEC_BATCH = 4
DEC_SEQ = 2048
PAST_LEN = 256
NEG = -1e30
GRID_W = 64
HEAD_DIM = 64
SCALE = HEAD_DIM ** -0.5
ROPE_BASE = 10000.0
EPS = 1e-6
HG_HEADS = 4
HG_W = 256
WIN_HEADS = 6
WIN_KV = 2
WIN = 128
POOL_SIZES = (2, 4, 8, 16)
POOL_W = 256
NA_HEADS = 4
NA_KH = 8
NA_KW = 16
N_EXPERTS = 32
TOP_K = 4
D_EXPERT = 256
ROUTED_SCALE = 2.5

N_CTX = BATCH * SEQ
N_LAT = DEC_BATCH * DEC_SEQ
NTOK = N_CTX + N_LAT
WQ_B = WIN_HEADS * HEAD_DIM
WK_B = WIN_KV * HEAD_DIM
WD = NA_HEADS * HEAD_DIM

Z_W = 3072
COL_AQ, COL_AFF, COL_AFB, COL_AI, COL_AG, COL_CU, COL_DQ, COL_DK, COL_DV = (
    0, 256, 512, 768, 1024, 1280, 1536, 1792, 2048)
COL_BQ, COL_BK, COL_BV = 2304, 2688, 2816

TM = 512
TMP = 2048
TN = 512
HB = 128
VMEM_LIMIT = 56 * 1024 * 1024


def _params(sem, vmem=VMEM_LIMIT):
    return pltpu.CompilerParams(dimension_semantics=sem, vmem_limit_bytes=vmem)


def _seg_of_tile(i, tile):
    nct = N_CTX // tile
    per = DEC_SEQ // tile
    return jnp.where(i < nct, 0, 1 + (i - nct) // per)


def _bd_ones(w):
    idx = np.arange(w) // HEAD_DIM
    return (idx[:, None] == idx[None, :]).astype(np.float32)


def _dot(a, b):
    return jnp.dot(a, b, preferred_element_type=F32)


def _dot_nt(a, b):
    return lax.dot_general(a, b, (((1,), (1,)), ((), ())), preferred_element_type=F32)


def _split_dot(x, w_bf16):
    hi = x.astype(BF16)
    lo = (x - hi.astype(F32)).astype(BF16)
    return _dot(hi, w_bf16) + _dot(lo, w_bf16)


def _head_rms(x, bd, gain):
    ms = _split_dot(x * x, bd) * (1.0 / HEAD_DIM)
    return x * lax.rsqrt(ms + EPS) * gain


def _silu(x):
    return x * jax.nn.sigmoid(x)


def _mod_kernel(c_ref, w_ref, b_ref, o_ref):
    c = c_ref[...]
    a = _silu(c).astype(BF16)
    o_ref[...] = _dot(a, w_ref[...].astype(BF16)) + b_ref[...]


def _modulation(cvec8, w_mod, b_mod):
    n = 6 * D_MODEL
    tn = 1536
    return pl.pallas_call(
        _mod_kernel,
        grid=(DEPTH, n // tn),
        in_specs=[
            pl.BlockSpec((8, D_MODEL), lambda l, j: (0, 0)),
            pl.BlockSpec((None, D_MODEL, tn), lambda l, j: (l, 0, j)),
            pl.BlockSpec((None, 1, tn), lambda l, j: (l, 0, j)),
        ],
        out_specs=pl.BlockSpec((None, 8, tn), lambda l, j: (l, 0, j)),
        out_shape=jax.ShapeDtypeStruct((DEPTH, 8, n), F32),
        compiler_params=_params(("arbitrary", "arbitrary")),
        name="modulation",
    )(cvec8, w_mod, b_mod.reshape(DEPTH, 1, n))


def _prenorm(x, gain, shift, scale):
    ms = jnp.mean(x * x, axis=-1, keepdims=True)
    return x * lax.rsqrt(ms + EPS) * gain * (1.0 + scale) + shift


def _two_group_specs(tile, width, nargs=1):
    nct = N_CTX // tile
    if nargs == 1:
        return [pl.BlockSpec((tile, width), lambda i: (jnp.minimum(i, nct - 1), 0)),
                pl.BlockSpec((tile, width), lambda i: (jnp.maximum(i - nct, 0), 0))]
    return [pl.BlockSpec((tile, width), lambda i, j: (jnp.minimum(i, nct - 1), 0)),
            pl.BlockSpec((tile, width), lambda i, j: (jnp.maximum(i - nct, 0), 0))]


def _pick_group(tile, c_ref, l_ref):
    return jnp.where(pl.program_id(0) < N_CTX // tile, c_ref[...], l_ref[...])


def _proj_kernel(xc_ref, xl_ref, m_ref, g_ref, win_ref, z_ref, h_scr):
    @pl.when(pl.program_id(1) == 0)
    def _():
        x = _pick_group(TMP, xc_ref, xl_ref)
        h_scr[...] = _prenorm(x, g_ref[...], m_ref[0:1, :], m_ref[1:2, :]).astype(BF16)

    z_ref[...] = _dot(h_scr[...], win_ref[...])


def _projection(x_c, x_l, mods, norm_g, w_in_p):
    return pl.pallas_call(
        _proj_kernel,
        grid=(NTOK // TMP, Z_W // TN),
        in_specs=_two_group_specs(TMP, D_MODEL, nargs=2) + [
            pl.BlockSpec((None, 6, D_MODEL), lambda i, j: (_seg_of_tile(i, TMP), 0, 0)),
            pl.BlockSpec((1, D_MODEL), lambda i, j: (0, 0)),
            pl.BlockSpec((D_MODEL, TN), lambda i, j: (0, j)),
        ],
        out_specs=pl.BlockSpec((TMP, TN), lambda i, j: (i, j)),
        out_shape=jax.ShapeDtypeStruct((NTOK, Z_W), F32),
        scratch_shapes=[pltpu.VMEM((TMP, D_MODEL), BF16)],
        compiler_params=_params(("arbitrary", "arbitrary")),
        name="projection",
    )(x_c, x_l, mods, norm_g, w_in_p)


def _hgrn_kernel(qf_ref, ff_ref, vf_ref, qb_ref, fb_ref, vb_ref, lb_ref, s0f_ref, s0b_ref, bd_ref,
                 of_ref, ob_ref, sff_ref, sfb_ref, sf_scr, sb_scr):
    _hgrn_direction(False, qf_ref, ff_ref, vf_ref, lb_ref[0], s0f_ref, bd_ref, of_ref, sff_ref, sf_scr)
    _hgrn_direction(True, qb_ref, fb_ref, vb_ref, lb_ref[1], s0b_ref, bd_ref, ob_ref, sfb_ref, sb_scr)


HSB = 128


def _hgrn_subblock(rev, q, zf, v, lb, s_t, bd):
    n = HSB
    f = lb + (1.0 - lb) * jax.nn.sigmoid(zf)
    lf = jnp.log2(f)
    kk = 1.0 - f

    row = lax.broadcasted_iota(jnp.int32, (n, HG_W), 0)
    tq = lax.broadcasted_iota(jnp.int32, (n, n), 0)
    tk = lax.broadcasted_iota(jnp.int32, (n, n), 1)

    def before(x, m):
        return pltpu.roll(x, (n - m) if rev else m, 0)

    def after(x, m):
        return pltpu.roll(x, m if rev else (n - m), 0)

    q16 = q.astype(BF16)
    k16 = kk.astype(BF16)
    att = [jnp.where(tq == tk, _dot_nt(_hs(q16, h), _hs(k16, h)), 0.0) for h in range(HG_HEADS)]

    tot = lf
    pin = lf
    sex = jnp.zeros_like(lf)
    m = 1
    while m < n:
        late = ((row & (2 * m - 1)) < m) if rev else ((row & (2 * m - 1)) >= m)
        qm = jnp.where(late, q * jnp.exp2(pin), 0.0).astype(BF16)
        km = jnp.where(late, 0.0, kk * jnp.exp2(sex)).astype(BF16)
        shift = (2 * m).bit_length() - 1
        same = (tq >> shift) == (tk >> shift)
        for h in range(HG_HEADS):
            sc = _dot_nt(_hs(qm, h), _hs(km, h))
            att[h] = att[h] + (sc if 2 * m == n else jnp.where(same, sc, 0.0))
        tb = before(tot, m)
        ta = after(tot, m)
        pin = pin + jnp.where(late, tb, 0.0)
        sex = sex + jnp.where(late, 0.0, ta)
        tot = tot + jnp.where(late, tb, ta)
        m *= 2

    lane_head = lax.broadcasted_iota(jnp.int32, (n, HG_W), 1) // HEAD_DIM
    o = _dot_nt((q * jnp.exp2(pin)).astype(BF16), s_t.astype(BF16))
    for h in range(HG_HEADS):
        vh = jnp.where(lane_head == h, v, 0.0).astype(BF16)
        o = o + _dot(att[h].astype(BF16), vh)

    kt = (kk * jnp.exp2(sex)).astype(BF16)
    dec = jnp.exp2(tot[0:1, :])
    s_new = s_t * dec + _dot(v.T.astype(BF16), kt) * bd
    return o, s_new


def _hgrn_direction(rev, q_ref, f_ref, v_ref, lb, s0_ref, bd_ref, o_ref, sfin_ref, s_scr):
    i = pl.program_id(0)
    blk = (pl.num_programs(0) - 1 - i) if rev else i
    nct = N_CTX // HB
    per_c = SEQ // HB
    per_l = DEC_SEQ // HB
    is_ctx = blk < nct
    pos = jnp.where(is_ctx, blk % per_c, (blk - nct) % per_l)
    last = jnp.where(is_ctx, per_c - 1, per_l - 1)
    first_pos = last if rev else 0
    final_pos = 0 if rev else last

    @pl.when(pos == first_pos)
    def _():
        s_scr[...] = jnp.zeros_like(s_scr)

    @pl.when((pos == first_pos) & jnp.logical_not(is_ctx))
    def _():
        for h in range(HG_HEADS):
            hs = slice(HEAD_DIM * h, HEAD_DIM * (h + 1))
            s_scr[hs, hs] = s0_ref[h].T

    bd = bd_ref[...]
    s_new = s_scr[...]
    subs = range(HB // HSB)
    for sb in (reversed(subs) if rev else subs):
        rs = slice(HSB * sb, HSB * (sb + 1))
        o, s_new = _hgrn_subblock(rev, q_ref[rs, :], f_ref[rs, :], v_ref[rs, :], lb, s_new, bd)
        o_ref[rs, :] = o
    s_scr[...] = s_new

    @pl.when((pos == final_pos) & is_ctx)
    def _():
        for h in range(HG_HEADS):
            hs = slice(HEAD_DIM * h, HEAD_DIM * (h + 1))
            sfin_ref[h] = s_new[hs, hs].T


def _hgrn(z, lbs_l, state_hgrn, layer, bd):
    nb = NTOK // HB
    nct = N_CTX // HB
    rblk = lambda i: nb - 1 - i

    def seq_of(b):
        return jnp.where(b < nct, b // (SEQ // HB), BATCH + (b - nct) // (DEC_SEQ // HB))

    col = lambda blk, c: pl.BlockSpec((HB, HG_W), lambda i: (blk(i), c // HG_W))
    state = lambda blk, d: pl.BlockSpec(
        (None, None, None, HG_HEADS, HEAD_DIM, HEAD_DIM),
        lambda i: (jnp.maximum(seq_of(blk(i)) - BATCH, 0), layer, d, 0, 0, 0))
    final = lambda blk: pl.BlockSpec((None, HG_HEADS, HEAD_DIM, HEAD_DIM),
                                     lambda i: (jnp.minimum(seq_of(blk(i)), BATCH - 1), 0, 0, 0))
    fwd = lambda i: i
    return pl.pallas_call(
        _hgrn_kernel,
        grid=(nb,),
        in_specs=[
            col(fwd, COL_AQ), col(fwd, COL_AFF), col(fwd, COL_AI),
            col(rblk, COL_AQ), col(rblk, COL_AFB), col(rblk, COL_AI),
            pl.BlockSpec((2, 1, HG_W), lambda i: (0, 0, 0)),
            state(fwd, 0), state(rblk, 1),
            pl.BlockSpec((HG_W, HG_W), lambda i: (0, 0)),
        ],
        out_specs=[col(fwd, 0), col(rblk, 0), final(fwd), final(rblk)],
        out_shape=[
            jax.ShapeDtypeStruct((NTOK, HG_W), F32),
            jax.ShapeDtypeStruct((NTOK, HG_W), F32),
            jax.ShapeDtypeStruct((BATCH, HG_HEADS, HEAD_DIM, HEAD_DIM), F32),
            jax.ShapeDtypeStruct((BATCH, HG_HEADS, HEAD_DIM, HEAD_DIM), F32),
        ],
        scratch_shapes=[pltpu.VMEM((HG_W, HG_W), F32), pltpu.VMEM((HG_W, HG_W), F32)],
        compiler_params=_params(("arbitrary",)),
        name="hgrn",
    )(z, z, z, z, z, z, lbs_l.reshape(2, 1, HG_W), state_hgrn, state_hgrn, bd)


def _pool_kernel(u_ref, w_ref, sc_ref, o_ref, *, t_len):
    u = u_ref[...]
    row = lax.broadcasted_iota(jnp.int32, (t_len, POOL_W), 0)
    grp = lax.broadcasted_iota(jnp.int32, (t_len, POOL_W), 1) // HEAD_DIM
    half = jnp.left_shift(1, grp)
    acc = jnp.zeros_like(u)
    for j in range(-8, 8):
        src = row + j
        ok = (j >= -half) & (j < half) & (src >= 0) & (src < t_len)
        shifted = u if j == 0 else pltpu.roll(u, (-j) % t_len, 0)
        acc = acc + jnp.where(ok, shifted, 0.0)
    cnt = (jnp.minimum(row + half, t_len) - jnp.maximum(row - half, 0)).astype(F32)
    y = _dot((acc / cnt - u).astype(BF16), w_ref[...]) * sc_ref[...]
    o_ref[...] = y.astype(BF16)


def _pool(z, row0, nseq, t_len, w_bd, scale):
    return pl.pallas_call(
        functools.partial(_pool_kernel, t_len=t_len),
        grid=(nseq,),
        in_specs=[
            pl.BlockSpec((t_len, POOL_W), lambda b: (row0 // t_len + b, COL_CU // POOL_W)),
            pl.BlockSpec((POOL_W, POOL_W), lambda b: (0, 0)),
            pl.BlockSpec((1, POOL_W), lambda b: (0, 0)),
        ],
        out_specs=pl.BlockSpec((t_len, POOL_W), lambda b: (b, 0)),
        out_shape=jax.ShapeDtypeStruct((nseq * t_len, POOL_W), BF16),
        compiler_params=_params(("arbitrary",)),
        name="pool",
    )(z, w_bd, scale)


def _rope(x, cos, sin):
    w = x.shape[-1]
    lane = lax.broadcasted_iota(jnp.int32, x.shape, 1)
    up = pltpu.roll(x, w - 16, 1)
    dn = pltpu.roll(x, 16, 1)
    return x * cos + jnp.where((lane & 31) < 16, up, dn) * sin


def _prep_kernel(*refs, rope):
    if rope:
        (bq, bk, bv, dq, dk, dv, gq, gk, gdq, gdk, bd, cq, sq, ck, sk,
         oq, ok_, ov, odq, odk, odv) = refs
    else:
        (bq, bk, bv, dq, dk, dv, gq, gk, gdq, gdk, bd,
         oq, ok_, ov, odq, odk, odv, ok32, odk32, ov32, odv32) = refs
    bdm = bd[...]
    q = _head_rms(bq[...], bdm, gq[...])
    k = _head_rms(bk[...], bdm[:WK_B, :WK_B], gk[...])
    qd = _head_rms(dq[...], bdm[:WD, :WD], gdq[...])
    kd = _head_rms(dk[...], bdm[:WD, :WD], gdk[...])
    if rope:
        q = _rope(q, cq[...], sq[...])
        k = _rope(k, ck[...], sk[...])
    else:
        ok32[...] = k
        odk32[...] = kd
        ov32[...] = bv[...]
        odv32[...] = dv[...]
    oq[...] = (q * SCALE).astype(BF16)
    ok_[...] = k.astype(BF16)
    ov[...] = bv[...].astype(BF16)
    odq[...] = (qd * SCALE).astype(BF16)
    odk[...] = kd.astype(BF16)
    odv[...] = dv[...].astype(BF16)


def _prep(z, row0, nrows, gains, bd, rope_tabs):
    tm = 512
    nt = nrows // tm
    r0 = row0 // tm
    rope = rope_tabs is not None
    col = lambda c, w: (lambda i: (r0 + i, c // w))
    in_specs = [
        pl.BlockSpec((tm, WQ_B), col(COL_BQ, WQ_B)),
        pl.BlockSpec((tm, WK_B), col(COL_BK, WK_B)),
        pl.BlockSpec((tm, WK_B), col(COL_BV, WK_B)),
        pl.BlockSpec((tm, WD), col(COL_DQ, WD)),
        pl.BlockSpec((tm, WD), col(COL_DK, WD)),
        pl.BlockSpec((tm, WD), col(COL_DV, WD)),
        pl.BlockSpec((1, WQ_B), lambda i: (0, 0)),
        pl.BlockSpec((1, WK_B), lambda i: (0, 0)),
        pl.BlockSpec((1, WD), lambda i: (0, 0)),
        pl.BlockSpec((1, WD), lambda i: (0, 0)),
        pl.BlockSpec((WQ_B, WQ_B), lambda i: (0, 0)),
    ]
    args = [z, z, z, z, z, z, *gains, bd]
    per = DEC_SEQ // tm
    if rope:
        in_specs += [
            pl.BlockSpec((tm, WQ_B), lambda i: (i % per, 0)),
            pl.BlockSpec((tm, WQ_B), lambda i: (i % per, 0)),
            pl.BlockSpec((tm, WK_B), lambda i: (i % per, 0)),
            pl.BlockSpec((tm, WK_B), lambda i: (i % per, 0)),
        ]
        args += list(rope_tabs)
    widths = [WQ_B, WK_B, WK_B, WD, WD, WD]
    out_specs = [pl.BlockSpec((tm, w), lambda i: (i, 0)) for w in widths]
    out_shape = [jax.ShapeDtypeStruct((nrows, w), BF16) for w in widths]
    if not rope:
        out_specs += [pl.BlockSpec((tm, w), lambda i: (i, 0)) for w in (WK_B, WD, WK_B, WD)]
        out_shape += [jax.ShapeDtypeStruct((nrows, w), F32) for w in (WK_B, WD, WK_B, WD)]
    return pl.pallas_call(
        functools.partial(_prep_kernel, rope=rope),
        grid=(nt,),
        in_specs=in_specs,
        out_specs=out_specs,
        out_shape=out_shape,
        compiler_params=_params(("arbitrary",)),
        name="prep_lat" if rope else "prep_ctx",
    )(*args)


def _softmax_pv(scores, values, sink):
    m = scores[0].max(axis=-1, keepdims=True)
    for s in scores[1:]:
        m = jnp.maximum(m, s.max(axis=-1, keepdims=True))
    if sink is not None:
        m = jnp.maximum(m, sink)
    den = jnp.zeros_like(m) if sink is None else jnp.exp(sink - m)
    acc = None
    for s, v in zip(scores, values):
        p = jnp.exp(s - m)
        den = den + p.sum(axis=-1, keepdims=True)
        pv = _dot(p.astype(BF16), v)
        acc = pv if acc is None else acc + pv
    return acc / den


def _hs(x, h):
    return x[:, HEAD_DIM * h:HEAD_DIM * (h + 1)]


GQA = WIN_HEADS // WIN_KV


def _stack_group(q, sink_ref, kv, rows):
    qs = jnp.concatenate([_hs(q, GQA * kv + j) for j in range(GQA)], axis=0)
    part = lax.broadcasted_iota(jnp.int32, (GQA * rows, 1), 0) // rows
    sink = jnp.zeros((GQA * rows, 1), F32)
    for j in range(GQA):
        sink = jnp.where(part == j, sink_ref[0, GQA * kv + j], sink)
    return qs, sink


def _ctx_attn_kernel(sink_ref, q_ref, k_ref, v_ref, qd_ref, kd_ref, vd_ref, ob_ref, od_ref):
    q, k, v = q_ref[...], k_ref[...], v_ref[...]
    for kv in range(WIN_KV):
        qs, sink = _stack_group(q, sink_ref, kv, SEQ)
        s = _dot_nt(qs, _hs(k, kv))
        o = _softmax_pv([s], [_hs(v, kv)], sink)
        for j in range(GQA):
            h = GQA * kv + j
            ob_ref[:, HEAD_DIM * h:HEAD_DIM * (h + 1)] = o[SEQ * j:SEQ * (j + 1)].astype(BF16)
    qd, kd, vd = qd_ref[...], kd_ref[...], vd_ref[...]
    for h in range(NA_HEADS):
        s = _dot_nt(_hs(qd, h), _hs(kd, h))
        o = _softmax_pv([s], [_hs(vd, h)], None)
        od_ref[:, HEAD_DIM * h:HEAD_DIM * (h + 1)] = o.astype(BF16)


def _ctx_attn(sink, q, k, v, qd, kd, vd):
    blk = lambda w: pl.BlockSpec((SEQ, w), lambda b: (b, 0))
    return pl.pallas_call(
        _ctx_attn_kernel,
        grid=(BATCH,),
        in_specs=[pl.BlockSpec(memory_space=pltpu.SMEM),
                  blk(WQ_B), blk(WK_B), blk(WK_B), blk(WD), blk(WD), blk(WD)],
        out_specs=[blk(WQ_B), blk(WD)],
        out_shape=[jax.ShapeDtypeStruct((N_CTX, WQ_B), BF16), jax.ShapeDtypeStruct((N_CTX, WD), BF16)],
        compiler_params=_params(("arbitrary",)),
        name="ctx_attn",
    )(sink, q, k, v, qd, kd, vd)


WIN_SPAN = 3 * WIN


def _win_attn_kernel(sink_ref, q_ref, k_ref, v_ref, kc_ref, vc_ref, o_ref):
    qi = pl.program_id(1)
    start = pl.multiple_of(jnp.clip(qi * WIN - WIN, 0, DEC_SEQ - WIN_SPAN), WIN)
    q = q_ref[...]
    kw = k_ref[pl.ds(start, WIN_SPAN), :]
    vw = v_ref[pl.ds(start, WIN_SPAN), :]
    kc, vc = kc_ref[...], vc_ref[...]
    rows = GQA * WIN
    qpos = qi * WIN + lax.broadcasted_iota(jnp.int32, (rows, WIN_SPAN), 0) % WIN
    kpos = start + lax.broadcasted_iota(jnp.int32, (rows, WIN_SPAN), 1)
    valid = jnp.abs(qpos - kpos) <= WIN
    for kv in range(WIN_KV):
        qs, sink = _stack_group(q, sink_ref, kv, WIN)
        s_loc = jnp.where(valid, _dot_nt(qs, _hs(kw, kv)), NEG)
        s_ctx = _dot_nt(qs, _hs(kc, kv))
        o = _softmax_pv([s_loc, s_ctx], [_hs(vw, kv), _hs(vc, kv)], sink)
        for j in range(GQA):
            h = GQA * kv + j
            o_ref[:, HEAD_DIM * h:HEAD_DIM * (h + 1)] = o[WIN * j:WIN * (j + 1)].astype(BF16)


def _win_attn(sink, q, k, v, kc, vc):
    nq = DEC_SEQ // WIN
    return pl.pallas_call(
        _win_attn_kernel,
        grid=(DEC_BATCH, nq),
        in_specs=[
            pl.BlockSpec(memory_space=pltpu.SMEM),
            pl.BlockSpec((WIN, WQ_B), lambda b, i: (b * nq + i, 0)),
            pl.BlockSpec((None, DEC_SEQ, WK_B), lambda b, i: (b, 0, 0)),
            pl.BlockSpec((None, DEC_SEQ, WK_B), lambda b, i: (b, 0, 0)),
            pl.BlockSpec((None, PAST_LEN, WK_B), lambda b, i: (b, 0, 0)),
            pl.BlockSpec((None, PAST_LEN, WK_B), lambda b, i: (b, 0, 0)),
        ],
        out_specs=pl.BlockSpec((WIN, WQ_B), lambda b, i: (b * nq + i, 0)),
        out_shape=jax.ShapeDtypeStruct((N_LAT, WQ_B), BF16),
        compiler_params=_params(("arbitrary", "arbitrary")),
        name="win_attn",
    )(sink, q, k.reshape(DEC_BATCH, DEC_SEQ, WK_B), v.reshape(DEC_BATCH, DEC_SEQ, WK_B), kc, vc)


NA_ROWS = DEC_SEQ // GRID_W
NA_G = 4
NA_NG = NA_ROWS // NA_G
NA_UROWS = NA_KH + NA_G - 1
NA_UKEYS = NA_UROWS * GRID_W
NA_QROWS = NA_G * GRID_W


def _na_union_start(g):
    return jnp.clip(g * NA_G - NA_KH // 2, 0, NA_ROWS - NA_UROWS)


def _natten_kernel(q_ref, k_ref, v_ref, kc_ref, vc_ref, bias_ref, o_ref):
    start = pl.multiple_of(_na_union_start(pl.program_id(1)) * GRID_W, GRID_W)
    q = q_ref[...]
    kw = k_ref[pl.ds(start, NA_UKEYS), :]
    vw = v_ref[pl.ds(start, NA_UKEYS), :]
    kc, vc = kc_ref[...], vc_ref[...]
    for h in range(NA_HEADS):
        qh = _hs(q, h)
        s_loc = _dot_nt(qh, _hs(kw, h)) + bias_ref[h]
        s_ctx = _dot_nt(qh, _hs(kc, h))
        o = _softmax_pv([s_loc, s_ctx], [_hs(vw, h), _hs(vc, h)], None)
        o_ref[:, HEAD_DIM * h:HEAD_DIM * (h + 1)] = o.astype(BF16)


def _natten(q, k, v, kc, vc, bias):
    variant = lambda g: jnp.where(g == 0, 0, jnp.where(g == NA_NG - 1, 2, 1))
    return pl.pallas_call(
        _natten_kernel,
        grid=(DEC_BATCH, NA_NG),
        in_specs=[
            pl.BlockSpec((NA_QROWS, WD), lambda b, g: (b * NA_NG + g, 0)),
            pl.BlockSpec((None, DEC_SEQ, WD), lambda b, g: (b, 0, 0)),
            pl.BlockSpec((None, DEC_SEQ, WD), lambda b, g: (b, 0, 0)),
            pl.BlockSpec((None, PAST_LEN, WD), lambda b, g: (b, 0, 0)),
            pl.BlockSpec((None, PAST_LEN, WD), lambda b, g: (b, 0, 0)),
            pl.BlockSpec((None, NA_HEADS, NA_QROWS, NA_UKEYS), lambda b, g: (variant(g), 0, 0, 0)),
        ],
        out_specs=pl.BlockSpec((NA_QROWS, WD), lambda b, g: (b * NA_NG + g, 0)),
        out_shape=jax.ShapeDtypeStruct((N_LAT, WD), BF16),
        compiler_params=_params(("arbitrary", "arbitrary")),
        name="natten",
    )(q, k.reshape(DEC_BATCH, DEC_SEQ, WD), v.reshape(DEC_BATCH, DEC_SEQ, WD), kc, vc, bias)


def _natten_bias(rpb):
    c = np.arange(GRID_W)
    cstart = np.clip(c - NA_KW // 2, 0, GRID_W - NA_KW)
    inwin = (c[None, :] >= cstart[:, None]) & (c[None, :] < cstart[:, None] + NA_KW)
    dc = np.clip(c[None, :] - c[:, None] + NA_KW - 1, 0, 2 * NA_KW - 2)
    pick = (dc[None] == np.arange(2 * NA_KW - 1)[:, None, None]).astype(np.float32)
    toep = jnp.einsum('hdj,jck->hdck', rpb.astype(F32), jnp.asarray(pick),
                      precision=lax.Precision.HIGHEST)
    toep = jnp.where(inwin[None, None], toep, NEG)
    masked = jnp.full((NA_HEADS, GRID_W, GRID_W), NEG, F32)
    variants = []
    for g in (0, 1, NA_NG - 1):
        u0 = int(np.clip(g * NA_G - NA_KH // 2, 0, NA_ROWS - NA_UROWS))
        rows = []
        for j in range(NA_G):
            r = g * NA_G + j
            w0 = int(np.clip(r - NA_KH // 2, 0, NA_ROWS - NA_KH))
            blocks = []
            for i in range(NA_UROWS):
                kr = u0 + i
                blocks.append(toep[:, kr - r + NA_KH - 1] if w0 <= kr < w0 + NA_KH else masked)
            rows.append(jnp.concatenate(blocks, axis=-1))
        variants.append(jnp.concatenate(rows, axis=-2))
    return jnp.stack(variants, axis=0)


def _merge_kernel(xc_ref, xl_ref, m_ref, g_ref, of_ref, ob_ref, ag_ref, on_ref, bd_ref,
                  ybc_ref, ybl_ref, ycc_ref, ycl_ref, ydc_ref, ydl_ref,
                  wg_ref, bg_ref, wa_ref, wb_ref, wc_ref, wd_ref, wo_ref,
                  g2_ref, wr_ref, br_ref, wsg_ref, wsu_ref, wsd_ref, tri_ref, ut_ref,
                  xa_ref, sl_ref, slt_ref, cnt_ref, xs_ref):
    x = _pick_group(TM, xc_ref, xl_ref)
    h = _prenorm(x, g_ref[...], m_ref[0:1, :], m_ref[1:2, :]).astype(BF16)
    o = of_ref[...] + ob_ref[...]
    ya = _head_rms(o, bd_ref[...], on_ref[...]) * _silu(ag_ref[...])
    yb = _pick_group(TM, ybc_ref, ybl_ref)
    yc = _pick_group(TM, ycc_ref, ycl_ref)
    yd = _pick_group(TM, ydc_ref, ydl_ref)
    d = D_MODEL
    merged = jnp.zeros((TM, d), F32)
    branches = ((ya.astype(BF16), wa_ref), (yb, wb_ref), (yc, wc_ref), (yd, wd_ref))
    for b, (y, w_ref) in enumerate(branches):
        gate = jax.nn.sigmoid(_dot(h, wg_ref[:, d * b:d * (b + 1)]) + bg_ref[:, d * b:d * (b + 1)])
        merged = merged + gate * _dot(y, w_ref[...])
    x1 = x + m_ref[2:3, :] * _dot(merged.astype(BF16), wo_ref[...])
    for t in range(TM // TR):
        _route_rows(x1[TR * t:TR * (t + 1)], t, m_ref, g2_ref, wr_ref, br_ref, wsg_ref, wsu_ref, wsd_ref,
                    tri_ref, ut_ref, xa_ref, sl_ref, slt_ref, cnt_ref, xs_ref)


def _merge(x_c, x_l, mods, norm_g, o_f, o_b, z, onorm, bd, yb_c, yb_l, yc_c, yc_l, yd_c, yd_l,
           w_gate, b_gate, wa, wb, wc, wd, wo, norm2_g, w_router, b_router, wsg, wsu, wsd, tri, ut):
    full = lambda a: pl.BlockSpec(a.shape, lambda i: (0, 0))
    tpm = TM // TR
    return pl.pallas_call(
        _merge_kernel,
        grid=(NTOK // TM,),
        in_specs=_two_group_specs(TM, D_MODEL) + [
            pl.BlockSpec((None, 6, D_MODEL), lambda i: (_seg_of_tile(i, TM), 0, 0)),
            full(norm_g),
            pl.BlockSpec((TM, HG_W), lambda i: (i, 0)),
            pl.BlockSpec((TM, HG_W), lambda i: (i, 0)),
            pl.BlockSpec((TM, HG_W), lambda i: (i, COL_AG // HG_W)),
            full(onorm), full(bd),
        ] + _two_group_specs(TM, WQ_B) + _two_group_specs(TM, POOL_W) + _two_group_specs(TM, WD) + [
            full(w_gate), full(b_gate), full(wa), full(wb), full(wc), full(wd), full(wo),
            full(norm2_g), full(w_router), full(b_router), full(wsg), full(wsu), full(wsd), full(tri), full(ut),
        ],
        out_specs=[
            pl.BlockSpec((TM, XW), lambda i: (i, 0)),
            pl.BlockSpec((TM, E_PAD), lambda i: (i, 0)),
            pl.BlockSpec((tpm, 8, TR), lambda i: (i, 0, 0)),
            pl.BlockSpec((tpm, 1, E_PAD), lambda i: (i, 0, 0)),
            pl.BlockSpec((TM, D_MODEL), lambda i: (i, 0)),
        ],
        out_shape=[
            jax.ShapeDtypeStruct((NTOK, XW), BF16),
            jax.ShapeDtypeStruct((NTOK, E_PAD), F32),
            jax.ShapeDtypeStruct((NT_R, 8, TR), F32),
            jax.ShapeDtypeStruct((NT_R, 1, E_PAD), F32),
            jax.ShapeDtypeStruct((NTOK, D_MODEL), F32),
        ],
        compiler_params=_params(("arbitrary",)),
        name="merge",
    )(x_c, x_l, mods, norm_g, o_f, o_b, z, onorm, bd, yb_c, yb_l, yc_c, yc_l, yd_c, yd_l,
      w_gate, b_gate, wa, wb, wc, wd, wo, norm2_g, w_router, b_router, wsg, wsu, wsd, tri, ut)


E_PAD = 128
TR = 256
RT = 512
SLAB = 16
NT_R = NTOK // TR
S_LOC = 1536
S_MAX = NTOK * TOP_K + NT_R * N_EXPERTS * (SLAB - 1) + N_EXPERTS * (RT - SLAB)
N_XT = S_MAX // RT
XW = D_MODEL + 2 * E_PAD
assert S_LOC >= TR * TOP_K + N_EXPERTS * (SLAB - 1) and S_LOC % 128 == 0 and S_MAX % RT == 0


def _route_rows(x, t, m_ref, g_ref, wr_ref, br_ref, wsg_ref, wsu_ref, wsd_ref, tri_ref, ut_ref,
                xa_ref, sl_ref, slt_ref, cnt_ref, xs_ref):
    rows = slice(TR * t, TR * (t + 1))
    lane = lax.broadcasted_iota(jnp.int32, (TR, E_PAD), 1)
    ms = jnp.mean(x * x, axis=-1, keepdims=True)
    y = x * lax.rsqrt(ms + EPS) * g_ref[...]
    h = y * (1.0 + m_ref[4:5, :]) + m_ref[3:4, :]
    h_hi = h.astype(BF16)
    h_lo = (h - h_hi.astype(F32)).astype(BF16)
    w = wr_ref[...]
    w_hi = w.astype(BF16)
    w_lo = (w - w_hi.astype(F32)).astype(BF16)
    logits = _dot(h_hi, w_hi) + _dot(h_hi, w_lo) + _dot(h_lo, w_hi)
    scores = jax.nn.sigmoid(logits)
    sel = jnp.where(lane < N_EXPERTS, scores + br_ref[...], -jnp.inf)
    picked = jnp.zeros((TR, E_PAD), F32)
    hot = jnp.zeros((TR, E_PAD), F32)
    idxs = []
    lane_f = lane.astype(F32)
    for _ in range(TOP_K):
        mx = sel.max(axis=-1, keepdims=True)
        idx = jnp.min(jnp.where(sel == mx, lane_f, float(E_PAD)), axis=-1, keepdims=True)
        hit = lane_f == idx
        picked = jnp.where(hit, scores, picked)
        hot = jnp.where(hit, 1.0, hot)
        sel = jnp.where(hit, -jnp.inf, sel)
        idxs.append(idx)
    wts = ROUTED_SCALE * picked / picked.sum(axis=-1, keepdims=True)

    cnt = hot.sum(axis=0, keepdims=True)
    pad = jnp.floor((cnt + (SLAB - 1.0)) * (1.0 / SLAB)) * SLAB
    loc = _dot(jnp.broadcast_to(pad, (8, E_PAD)).astype(BF16), ut_ref[...])[0:1, :]
    rank = _dot(tri_ref[...], hot.astype(BF16))
    slotmat = loc + rank
    sl = jnp.zeros((TR, E_PAD), F32)
    for k in range(TOP_K):
        s_k = jnp.sum(jnp.where(lane_f == idxs[k], slotmat, 0.0), axis=-1, keepdims=True)
        sl = jnp.where(lane == k, s_k, sl)
    sl_ref[rows, :] = sl
    slt_ref[t] = sl.T[0:8, :]
    cnt_ref[t] = pad

    w16 = wts.astype(BF16)
    xa_ref[rows, 0:D_MODEL] = h_hi
    xa_ref[rows, D_MODEL:D_MODEL + E_PAD] = w16
    xa_ref[rows, D_MODEL + E_PAD:XW] = (wts - w16.astype(F32)).astype(BF16)

    a = _silu(_dot(h_hi, wsg_ref[...])) * _dot(h_hi, wsu_ref[...])
    xs_ref[rows, :] = x + m_ref[5:6, :] * _dot(a.astype(BF16), wsd_ref[...])


def _slab_copy(src, src_row, dst, dst_row, sem):
    hint = lambda r: r if isinstance(r, int) else pl.multiple_of(r, SLAB)
    return pltpu.make_async_copy(src.at[pl.ds(hint(src_row), SLAB), :],
                                 dst.at[pl.ds(hint(dst_row), SLAB), :], sem)


NSL = S_LOC // SLAB
CB = 256


def _for_each_slab(grow_ref, ns_ref, t, fn):
    def per_slab(n, c):
        fn(n * SLAB, grow_ref[t * NSL + n])
        return c

    lax.fori_loop(0, ns_ref[t], per_slab, 0)


def _compact_kernel(grow_ref, ns_ref, nd_ref, toff_ref, tn_ref, nu_ref,
                    xa_ref, slt_ref, xs_hbm, xc_scr, zero_scr, sem):
    t = pl.program_id(0)
    slot = t % 2

    def wait_n(n, s):
        def body(_, c):
            _slab_copy(xc_scr.at[s], 0, xs_hbm, 0, sem.at[s]).wait()
            return c
        lax.fori_loop(0, n, body, 0)

    def unused_tile_copy(j):
        row = pl.multiple_of((nu_ref[0] + j) * RT, RT)
        return pltpu.make_async_copy(zero_scr, xs_hbm.at[pl.ds(row, RT), :], sem.at[2])

    @pl.when(t == 0)
    def _():
        zero_scr[...] = jnp.zeros_like(zero_scr)

        def body(j, c):
            unused_tile_copy(j).start()
            return c
        lax.fori_loop(0, N_XT - nu_ref[0], body, 0)

    @pl.when(t >= 2)
    def _():
        wait_n(nd_ref[t - 2], slot)

    slt = slt_ref[...].astype(jnp.int32)
    for cb in range(S_LOC // CB):
        @pl.when(cb * CB < ns_ref[t] * SLAB)
        def _():
            row = cb * CB + lax.broadcasted_iota(jnp.int32, (CB, TR), 0)
            hit = row == slt[0:1, :]
            for k in range(1, TOP_K):
                hit = hit | (row == slt[k:k + 1, :])
            onehot = jnp.where(hit, 1.0, 0.0).astype(BF16)
            xc_scr[slot, CB * cb:CB * (cb + 1), :] = _dot(onehot, xa_ref[...]).astype(BF16)

    _for_each_slab(grow_ref, ns_ref, t,
                   lambda lr, gr: _slab_copy(xc_scr.at[slot], lr, xs_hbm, gr, sem.at[slot]).start())

    @pl.when(t < N_EXPERTS)
    def _():
        def body(s, c):
            _slab_copy(zero_scr, 0, xs_hbm, toff_ref[t] + s * SLAB, sem.at[slot]).start()
            return c
        lax.fori_loop(0, tn_ref[t], body, 0)

    @pl.when(t == NT_R - 1)
    def _():
        wait_n(nd_ref[t], slot)
        wait_n(nd_ref[t - 1], 1 - slot)

        def body(j, c):
            unused_tile_copy(j).wait()
            return c
        lax.fori_loop(0, N_XT - nu_ref[0], body, 0)


def _compact(meta, xa, slt):
    grid_spec = pltpu.PrefetchScalarGridSpec(
        num_scalar_prefetch=6,
        grid=(NT_R,),
        in_specs=[
            pl.BlockSpec((TR, XW), lambda i, *_: (i, 0)),
            pl.BlockSpec((None, 8, TR), lambda i, *_: (i, 0, 0)),
        ],
        out_specs=pl.BlockSpec(memory_space=pl.ANY),
        scratch_shapes=[pltpu.VMEM((2, S_LOC, XW), BF16), pltpu.VMEM((RT, XW), BF16),
                        pltpu.SemaphoreType.DMA((3,))],
    )
    return pl.pallas_call(
        _compact_kernel,
        grid_spec=grid_spec,
        out_shape=jax.ShapeDtypeStruct((S_MAX, XW), BF16),
        compiler_params=_params(("arbitrary",)),
        name="compact",
    )(meta['grow'], meta['ns'], meta['nd'], meta['toff'], meta['tn'], meta['nu'], xa, slt)


def _expert_kernel(te_ref, ti_ref, nu_ref, nx_ref, par_ref, xs_ref, wg_hbm, wu_hbm, wd_hbm, y_ref,
                   wg_f, wu_f, wd_f, wgu_s, wd_s, sem, *, layer):
    i = pl.program_id(0)
    e = te_ref[i]
    slot = par_ref[i]
    used = i < nu_ref[0]
    first = used & ((i == 0) | (e != te_ref[jnp.maximum(i - 1, 0)]))

    def weight_copies(expert, s):
        return (pltpu.make_async_copy(wg_hbm.at[layer, expert], wg_f.at[s], sem.at[s]),
                pltpu.make_async_copy(wu_hbm.at[layer, expert], wu_f.at[s], sem.at[s]),
                pltpu.make_async_copy(wd_hbm.at[layer, expert], wd_f.at[s], sem.at[s]))

    @pl.when(i == 0)
    def _():
        for c in weight_copies(e, slot):
            c.start()

    @pl.when(first)
    def _():
        for c in weight_copies(e, slot):
            c.wait()
        wgu_s[:, 0:D_EXPERT] = wg_f[slot].astype(BF16)
        wgu_s[:, D_EXPERT:2 * D_EXPERT] = wu_f[slot].astype(BF16)
        wd_s[...] = wd_f[slot].astype(BF16)

    @pl.when(first & (nx_ref[i] >= 0))
    def _():
        for c in weight_copies(nx_ref[i], 1 - slot):
            c.start()

    @pl.when(used)
    def _():
        x = xs_ref[:, 0:D_MODEL]
        gw = xs_ref[:, D_MODEL:D_MODEL + E_PAD].astype(F32) + xs_ref[:, D_MODEL + E_PAD:XW].astype(F32)
        lane = lax.broadcasted_iota(jnp.int32, (RT, E_PAD), 1)
        ge = jnp.sum(jnp.where(lane == e, gw, 0.0), axis=-1, keepdims=True)
        gu = _dot(x, wgu_s[...])
        a = _silu(gu[:, 0:D_EXPERT]) * gu[:, D_EXPERT:2 * D_EXPERT]
        y_ref[...] = _dot((a * ge).astype(BF16), wd_s[...]).astype(BF16)

    @pl.when(i >= nu_ref[0])
    def _():
        y_ref[...] = jnp.zeros_like(y_ref)


def _experts(meta, layer, xs, w_eg, w_eu, w_ed):
    any_spec = pl.BlockSpec(memory_space=pl.ANY)
    grid_spec = pltpu.PrefetchScalarGridSpec(
        num_scalar_prefetch=5,
        grid=(N_XT,),
        in_specs=[pl.BlockSpec((RT, XW), lambda i, te, ti, *_: (ti[i], 0)), any_spec, any_spec, any_spec],
        out_specs=pl.BlockSpec((RT, D_MODEL), lambda i, *_: (i, 0)),
        scratch_shapes=[pltpu.VMEM((2, D_MODEL, D_EXPERT), F32), pltpu.VMEM((2, D_MODEL, D_EXPERT), F32),
                        pltpu.VMEM((2, D_EXPERT, D_MODEL), F32),
                        pltpu.VMEM((D_MODEL, 2 * D_EXPERT), BF16), pltpu.VMEM((D_EXPERT, D_MODEL), BF16),
                        pltpu.SemaphoreType.DMA((2,))],
    )
    return pl.pallas_call(
        functools.partial(_expert_kernel, layer=layer),
        grid_spec=grid_spec,
        out_shape=jax.ShapeDtypeStruct((S_MAX, D_MODEL), BF16),
        compiler_params=_params(("arbitrary",)),
        name="experts",
    )(meta['te'], meta['ti'], meta['nu'], meta['nx'], meta['par'], xs, w_eg, w_eu, w_ed)


def _combine_kernel(grow_ref, ns_ref, xs_ref, m_ref, sl_ref, y_hbm, oc_ref, ol_ref, yc_scr, sem):
    t = pl.program_id(0)
    slot = t % 2

    def issue(tt, s):
        _for_each_slab(grow_ref, ns_ref, tt,
                       lambda lr, gr: _slab_copy(y_hbm, gr, yc_scr.at[s], lr, sem.at[s]).start())

    @pl.when(t == 0)
    def _():
        yc_scr[...] = jnp.zeros_like(yc_scr)
        issue(0, 0)

    @pl.when(t + 1 < NT_R)
    def _():
        issue(t + 1, 1 - slot)

    def wait_body(_, c):
        _slab_copy(y_hbm, 0, yc_scr.at[slot], 0, sem.at[slot]).wait()
        return c
    lax.fori_loop(0, ns_ref[t], wait_body, 0)

    col = lax.broadcasted_iota(jnp.int32, (TR, S_LOC), 1)
    sl = sl_ref[...].astype(jnp.int32)
    hit = col == sl[:, 0:1]
    for k in range(1, TOP_K):
        hit = hit | (col == sl[:, k:k + 1])
    onehot = jnp.where(hit, 1.0, 0.0).astype(BF16)
    out = xs_ref[...] + m_ref[5:6, :] * _dot(onehot, yc_scr[slot])

    @pl.when(t < N_CTX // TR)
    def _():
        oc_ref[...] = out

    @pl.when(t >= N_CTX // TR)
    def _():
        ol_ref[...] = out


def _combine(meta, xsh, mods, sl, y):
    nct = N_CTX // TR
    out_specs = [pl.BlockSpec((TR, D_MODEL), lambda i, *_: (jnp.minimum(i, nct - 1), 0)),
                 pl.BlockSpec((TR, D_MODEL), lambda i, *_: (jnp.maximum(i - nct, 0), 0))]
    out_shape = [jax.ShapeDtypeStruct((N_CTX, D_MODEL), F32), jax.ShapeDtypeStruct((N_LAT, D_MODEL), F32)]
    grid_spec = pltpu.PrefetchScalarGridSpec(
        num_scalar_prefetch=2,
        grid=(NT_R,),
        in_specs=[
            pl.BlockSpec((TR, D_MODEL), lambda i, *_: (i, 0)),
            pl.BlockSpec((None, 6, D_MODEL), lambda i, *_: (_seg_of_tile(i, TR), 0, 0)),
            pl.BlockSpec((TR, E_PAD), lambda i, *_: (i, 0)),
            pl.BlockSpec(memory_space=pl.ANY),
        ],
        out_specs=out_specs,
        scratch_shapes=[pltpu.VMEM((2, S_LOC, D_MODEL), BF16), pltpu.SemaphoreType.DMA((2,))],
    )
    return pl.pallas_call(
        _combine_kernel,
        grid_spec=grid_spec,
        out_shape=out_shape,
        compiler_params=_params(("arbitrary",)),
        name="combine",
    )(meta['grow'], meta['ns'], xsh, mods, sl, y)


def _route_meta(cnt):
    pc = cnt[:, 0, :N_EXPERTS].astype(jnp.int32)
    tot = pc.sum(axis=0)
    tot_pad = ((tot + RT - 1) // RT) * RT
    ends = jnp.cumsum(tot_pad)
    base = ends - tot_pad
    dst = base[None, :] + jnp.cumsum(pc, axis=0) - pc
    cum = jnp.cumsum(pc, axis=1)
    ns = cum[:, -1] // SLAB
    lrow = jnp.arange(NSL, dtype=jnp.int32)[None, :, None] * SLAB
    owner = jnp.minimum(jnp.sum(cum[:, None, :] <= lrow, axis=2), N_EXPERTS - 1)
    mine = owner[:, :, None] == jnp.arange(N_EXPERTS, dtype=jnp.int32)[None, None, :]
    grow = jnp.sum(jnp.where(mine, (dst - cum + pc)[:, None, :], 0), axis=2) + lrow[:, :, 0]
    tn = (tot_pad - tot) // SLAB
    nd = ns + jnp.pad(tn, (0, NT_R - N_EXPERTS))
    n_used = ends[-1] // RT
    ti = jnp.minimum(jnp.arange(N_XT, dtype=jnp.int32), n_used - 1)
    te = jnp.minimum(jnp.sum(ends[None, :] <= (ti * RT)[:, None], axis=1), N_EXPERTS - 1)
    eidx = jnp.arange(N_EXPERTS, dtype=jnp.int32)
    has = tot_pad > 0
    later = has[None, :] & (eidx[None, :] > eidx[:, None])
    nxt_e = jnp.min(jnp.where(later, eidx[None, :], N_EXPERTS), axis=1)
    nxt_e = jnp.where(nxt_e == N_EXPERTS, -1, nxt_e)
    ordinal = jnp.cumsum(has.astype(jnp.int32)) - 1
    is_e = te[:, None] == eidx[None, :]
    nx = jnp.sum(jnp.where(is_e, nxt_e[None, :], 0), axis=1)
    par = jnp.sum(jnp.where(is_e, ordinal[None, :], 0), axis=1) & 1
    i32 = lambda a: a.astype(jnp.int32)
    return dict(grow=i32(grow.reshape(-1)), ns=i32(ns), nd=i32(nd), toff=i32(base + tot), tn=i32(tn),
                te=i32(te), ti=i32(ti), nu=i32(n_used.reshape(1)), nx=i32(nx), par=i32(par))


def _moe_routed(layer, routed, mods, w_eg, w_eu, w_ed):
    xa, sl, slt, cnt, xsh = routed
    meta = _route_meta(cnt)
    xs = _compact(meta, xa, slt)
    y = _experts(meta, layer, xs, w_eg, w_eu, w_ed)
    return _combine(meta, xsh, mods, sl, y)


def _rope_tables(width):
    t = np.arange(DEC_SEQ)
    quarter = HEAD_DIM // 4
    inv = (ROPE_BASE ** (-np.arange(quarter) / quarter)).astype(np.float32)
    ang_r = (t // GRID_W).astype(np.float32)[:, None] * inv[None]
    ang_c = (t % GRID_W).astype(np.float32)[:, None] * inv[None]
    cos = np.concatenate([np.cos(ang_r), np.cos(ang_r), np.cos(ang_c), np.cos(ang_c)], axis=1)
    sin = np.concatenate([-np.sin(ang_r), np.sin(ang_r), -np.sin(ang_c), np.sin(ang_c)], axis=1)
    reps = width // HEAD_DIM
    return (jnp.asarray(np.tile(cos, (1, reps)), F32), jnp.asarray(np.tile(sin, (1, reps)), F32))


def _permute_w_in(w):
    a = w[:, 0:1280]
    bq, bk, bv = w[:, 1280:1664], w[:, 1664:1792], w[:, 1792:1920]
    cu = w[:, 1920:2176]
    d = w[:, 2176:2944]
    pad = jnp.zeros((w.shape[0], Z_W - 2944), w.dtype)
    return jnp.concatenate([a, cu, d, bq, bk, bv, pad], axis=1)


def _block_diag(blocks):
    g = blocks.shape[0]
    eye = jnp.eye(g, dtype=blocks.dtype)
    return jnp.einsum('gh,gij->gihj', eye, blocks).reshape(g * HEAD_DIM, g * HEAD_DIM)


def kernel(x_prompt, x_sample, cache_win_k, cache_win_v, cache_na_k, cache_na_v, state_hgrn, c, c_ctx, w_mod, b_mod, norm1_g, norm2_g, w_in, w_mgate, b_mgate, hg_lb, hg_onorm, win_qn, win_kn, win_sink, pool_w, pool_scale, na_qn, na_kn, na_rpb, w_branch, w_out, w_router, b_router, w_eg, w_eu, w_ed, w_sg, w_su, w_sd):
    lbp = jax.nn.softmax(hg_lb.astype(F32), axis=0)
    lbs = jnp.cumsum(lbp, axis=0) - lbp[0:1]

    cvec8 = jnp.concatenate([c_ctx[None], c, jnp.zeros((3, D_MODEL), F32)], axis=0)
    mods_all = _modulation(cvec8, w_mod, b_mod).reshape(DEPTH, 8, 6, D_MODEL)

    bd384 = jnp.asarray(_bd_ones(WQ_B), BF16)
    bd256 = bd384[:HG_W, :HG_W]
    bd256_f32 = jnp.asarray(_bd_ones(HG_W), F32)
    rope_q = _rope_tables(WQ_B)
    rope_k = _rope_tables(WK_B)
    tile = lambda g, reps: jnp.tile(g, reps)[None, :]
    tri = jnp.asarray(np.tril(np.ones((TR, TR), np.float32), -1), BF16)
    ut = jnp.asarray(np.triu(np.ones((E_PAD, E_PAD), np.float32), 1), BF16)

    x_c, x_l = x_prompt.reshape(N_CTX, D_MODEL), x_sample.reshape(N_LAT, D_MODEL)
    new_k, new_v, new_kd, new_vd, new_s = [], [], [], [], []
    for l in range(DEPTH):
        mods = mods_all[l]
        z = _projection(x_c, x_l, mods, norm1_g[l][None], _permute_w_in(w_in[l]).astype(BF16))

        o_f, o_b, sfin_f, sfin_b = _hgrn(z, lbs[l], state_hgrn.astype(F32), l, bd256_f32)
        new_s.append(jnp.stack([sfin_f, sfin_b], axis=1))

        w_pool = _block_diag(pool_w[l]).astype(BF16)
        yc_c = _pool(z, 0, BATCH, SEQ, w_pool, pool_scale[l][None])
        yc_l = _pool(z, N_CTX, DEC_BATCH, DEC_SEQ, w_pool, pool_scale[l][None])

        gains = (tile(win_qn[l], WIN_HEADS), tile(win_kn[l], WIN_KV), tile(na_qn[l], NA_HEADS), tile(na_kn[l], NA_HEADS))
        qb_c, kb_c, vb_c, qd_c, kd_c, vd_c, kb32, kd32, vb32, vd32 = _prep(z, 0, N_CTX, gains, bd384, None)
        qb_l, kb_l, vb_l, qd_l, kd_l, vd_l = _prep(z, N_CTX, N_LAT, gains, bd384, rope_q + rope_k)
        sink = win_sink[l][None]
        yb_c, yd_c = _ctx_attn(sink, qb_c, kb_c, vb_c, qd_c, kd_c, vd_c)
        kc = cache_win_k[:, l].reshape(DEC_BATCH, PAST_LEN, WK_B).astype(BF16)
        vc = cache_win_v[:, l].reshape(DEC_BATCH, PAST_LEN, WK_B).astype(BF16)
        yb_l = _win_attn(sink, qb_l, kb_l, vb_l, kc, vc)
        kcd = cache_na_k[:, l].reshape(DEC_BATCH, PAST_LEN, WD).astype(BF16)
        vcd = cache_na_v[:, l].reshape(DEC_BATCH, PAST_LEN, WD).astype(BF16)
        yd_l = _natten(qd_l, kd_l, vd_l, kcd, vcd, _natten_bias(na_rpb[l]))

        new_k.append(kb32.reshape(BATCH, SEQ, WIN_KV, HEAD_DIM))
        new_v.append(vb32.reshape(BATCH, SEQ, WIN_KV, HEAD_DIM))
        new_kd.append(kd32.reshape(BATCH, SEQ, NA_HEADS, HEAD_DIM))
        new_vd.append(vd32.reshape(BATCH, SEQ, NA_HEADS, HEAD_DIM))

        wbr = w_branch[l].astype(BF16)
        wr = jnp.pad(w_router[l], ((0, 0), (0, E_PAD - N_EXPERTS)))
        br = jnp.pad(b_router[l], (0, E_PAD - N_EXPERTS))[None]
        routed = _merge(x_c, x_l, mods, norm1_g[l][None], o_f, o_b, z, tile(hg_onorm[l], HG_HEADS), bd256,
                        yb_c, yb_l, yc_c, yc_l, yd_c, yd_l, w_mgate[l].astype(BF16), b_mgate[l][None],
                        wbr[0:256], wbr[256:640], wbr[640:896], wbr[896:1152], w_out[l].astype(BF16),
                        norm2_g[l][None], wr, br, w_sg[l].astype(BF16), w_su[l].astype(BF16),
                        w_sd[l].astype(BF16), tri, ut)
        x_c, x_l = _moe_routed(l, routed, mods, w_eg, w_eu, w_ed)

    y_p = x_c.reshape(BATCH, SEQ, D_MODEL)
    y_s = x_l.reshape(DEC_BATCH, DEC_SEQ, D_MODEL)
    return (y_p, y_s, jnp.stack(new_k, axis=1), jnp.stack(new_v, axis=1), jnp.stack(new_kd, axis=1),
            jnp.stack(new_vd, axis=1), jnp.stack(new_s, axis=1))
```

```python
import functools

import numpy as np
import jax
import jax.numpy as jnp
from jax import lax
from jax.experimental import pallas as pl
from jax.experimental.pallas import tpu as pltpu

F32 = jnp.float32
BF16 = jnp.bfloat16

D_MODEL = 1024
BATCH = 16
SEQ = 256
DEPTH = 2
DEC_BATCH = 4
DEC_SEQ = 2048
PAST_LEN = 256
NEG = -1e30
GRID_W = 64
HEAD_DIM = 64
SCALE = HEAD_DIM ** -0.5
ROPE_BASE = 10000.0
EPS = 1e-6
HG_HEADS = 4
HG_W = 256
WIN_HEADS = 6
WIN_KV = 2
WIN = 128
POOL_SIZES = (2, 4, 8, 16)
POOL_W = 256
NA_HEADS = 4
NA_KH = 8
NA_KW = 16
N_EXPERTS = 32
TOP_K = 4
D_EXPERT = 256
ROUTED_SCALE = 2.5

N_CTX = BATCH * SEQ
N_LAT = DEC_BATCH * DEC_SEQ
NTOK = N_CTX + N_LAT
WQ_B = WIN_HEADS * HEAD_DIM
WK_B = WIN_KV * HEAD_DIM
WD = NA_HEADS * HEAD_DIM

Z_W = 3072
COL_AQ, COL_AFF, COL_AFB, COL_AI, COL_AG, COL_CU, COL_DQ, COL_DK, COL_DV = (
    0, 256, 512, 768, 1024, 1280, 1536, 1792, 2048)
COL_BQ, COL_BK, COL_BV = 2304, 2688, 2816

TM = 512
TMP = 2048
TN = 512
HB = 256
VMEM_LIMIT = 56 * 1024 * 1024


def _params(sem, vmem=VMEM_LIMIT):
    return pltpu.CompilerParams(dimension_semantics=sem, vmem_limit_bytes=vmem)


def _seg_of_tile(i, tile):
    nct = N_CTX // tile
    per = DEC_SEQ // tile
    return jnp.where(i < nct, 0, 1 + (i - nct) // per)


def _bd_ones(w):
    idx = np.arange(w) // HEAD_DIM
    return (idx[:, None] == idx[None, :]).astype(np.float32)


def _dot(a, b):
    return jnp.dot(a, b, preferred_element_type=F32)


def _dot_nt(a, b):
    return lax.dot_general(a, b, (((1,), (1,)), ((), ())), preferred_element_type=F32)


def _split_dot(x, w_bf16):
    hi = x.astype(BF16)
    lo = (x - hi.astype(F32)).astype(BF16)
    return _dot(hi, w_bf16) + _dot(lo, w_bf16)


def _head_rms(x, bd, gain):
    ms = _split_dot(x * x, bd) * (1.0 / HEAD_DIM)
    return x * lax.rsqrt(ms + EPS) * gain


def _silu(x):
    return x * jax.nn.sigmoid(x)


def _mod_kernel(c_ref, w_ref, b_ref, o_ref):
    c = c_ref[...]
    a = _silu(c).astype(BF16)
    o_ref[...] = _dot(a, w_ref[...].astype(BF16)) + b_ref[...]


def _modulation(cvec8, w_mod, b_mod):
    n = 6 * D_MODEL
    tn = 1536
    return pl.pallas_call(
        _mod_kernel,
        grid=(DEPTH, n // tn),
        in_specs=[
            pl.BlockSpec((8, D_MODEL), lambda l, j: (0, 0)),
            pl.BlockSpec((None, D_MODEL, tn), lambda l, j: (l, 0, j)),
            pl.BlockSpec((None, 1, tn), lambda l, j: (l, 0, j)),
        ],
        out_specs=pl.BlockSpec((None, 8, tn), lambda l, j: (l, 0, j)),
        out_shape=jax.ShapeDtypeStruct((DEPTH, 8, n), F32),
        compiler_params=_params(("arbitrary", "arbitrary")),
        name="modulation",
    )(cvec8, w_mod, b_mod.reshape(DEPTH, 1, n))


def _prenorm(x, gain, shift, scale):
    ms = jnp.mean(x * x, axis=-1, keepdims=True)
    return x * lax.rsqrt(ms + EPS) * gain * (1.0 + scale) + shift


def _two_group_specs(tile, width, nargs=1):
    nct = N_CTX // tile
    if nargs == 1:
        return [pl.BlockSpec((tile, width), lambda i: (jnp.minimum(i, nct - 1), 0)),
                pl.BlockSpec((tile, width), lambda i: (jnp.maximum(i - nct, 0), 0))]
    return [pl.BlockSpec((tile, width), lambda i, j: (jnp.minimum(i, nct - 1), 0)),
            pl.BlockSpec((tile, width), lambda i, j: (jnp.maximum(i - nct, 0), 0))]


def _pick_group(tile, c_ref, l_ref):
    return jnp.where(pl.program_id(0) < N_CTX // tile, c_ref[...], l_ref[...])


def _proj_kernel(xc_ref, xl_ref, m_ref, g_ref, win_ref, z_ref, h_scr):
    @pl.when(pl.program_id(1) == 0)
    def _():
        x = _pick_group(TMP, xc_ref, xl_ref)
        h_scr[...] = _prenorm(x, g_ref[...], m_ref[0:1, :], m_ref[1:2, :]).astype(BF16)

    z_ref[...] = _dot(h_scr[...], win_ref[...])


def _projection(x_c, x_l, mods, norm_g, w_in_p):
    return pl.pallas_call(
        _proj_kernel,
        grid=(NTOK // TMP, Z_W // TN),
        in_specs=_two_group_specs(TMP, D_MODEL, nargs=2) + [
            pl.BlockSpec((None, 6, D_MODEL), lambda i, j: (_seg_of_tile(i, TMP), 0, 0)),
            pl.BlockSpec((1, D_MODEL), lambda i, j: (0, 0)),
            pl.BlockSpec((D_MODEL, TN), lambda i, j: (0, j)),
        ],
        out_specs=pl.BlockSpec((TMP, TN), lambda i, j: (i, j)),
        out_shape=jax.ShapeDtypeStruct((NTOK, Z_W), F32),
        scratch_shapes=[pltpu.VMEM((TMP, D_MODEL), BF16)],
        compiler_params=_params(("arbitrary", "arbitrary")),
        name="projection",
    )(x_c, x_l, mods, norm_g, w_in_p)


def _hgrn_kernel(qf_ref, ff_ref, vf_ref, qb_ref, fb_ref, vb_ref, lb_ref, s0f_ref, s0b_ref, bd_ref,
                 of_ref, ob_ref, sff_ref, sfb_ref, sf_scr, sb_scr):
    _hgrn_direction(False, qf_ref, ff_ref, vf_ref, lb_ref[0], s0f_ref, bd_ref, of_ref, sff_ref, sf_scr)
    _hgrn_direction(True, qb_ref, fb_ref, vb_ref, lb_ref[1], s0b_ref, bd_ref, ob_ref, sfb_ref, sb_scr)


HSB = 128


def _hgrn_subblock(rev, q, zf, v, lb, s_t, bd):
    n = HSB
    f = lb + (1.0 - lb) * jax.nn.sigmoid(zf)
    lf = jnp.log2(f)
    kk = 1.0 - f

    row = lax.broadcasted_iota(jnp.int32, (n, HG_W), 0)
    tq = lax.broadcasted_iota(jnp.int32, (n, n), 0)
    tk = lax.broadcasted_iota(jnp.int32, (n, n), 1)

    def before(x, m):
        return pltpu.roll(x, (n - m) if rev else m, 0)

    def after(x, m):
        return pltpu.roll(x, m if rev else (n - m), 0)

    q16 = q.astype(BF16)
    k16 = kk.astype(BF16)
    att = [jnp.where(tq == tk, _dot_nt(_hs(q16, h), _hs(k16, h)), 0.0) for h in range(HG_HEADS)]

    tot = lf
    pin = lf
    sex = jnp.zeros_like(lf)
    m = 1
    while m < n:
        late = ((row & (2 * m - 1)) < m) if rev else ((row & (2 * m - 1)) >= m)
        qm = jnp.where(late, q * jnp.exp2(pin), 0.0).astype(BF16)
        km = jnp.where(late, 0.0, kk * jnp.exp2(sex)).astype(BF16)
        shift = (2 * m).bit_length() - 1
        same = (tq >> shift) == (tk >> shift)
        for h in range(HG_HEADS):
            sc = _dot_nt(_hs(qm, h), _hs(km, h))
            att[h] = att[h] + (sc if 2 * m == n else jnp.where(same, sc, 0.0))
        tb = before(tot, m)
        ta = after(tot, m)
        pin = pin + jnp.where(late, tb, 0.0)
        sex = sex + jnp.where(late, 0.0, ta)
        tot = tot + jnp.where(late, tb, ta)
        m *= 2

    lane_head = lax.broadcasted_iota(jnp.int32, (n, HG_W), 1) // HEAD_DIM
    o = _dot_nt((q * jnp.exp2(pin)).astype(BF16), s_t.astype(BF16))
    for h in range(HG_HEADS):
        vh = jnp.where(lane_head == h, v, 0.0).astype(BF16)
        o = o + _dot(att[h].astype(BF16), vh)

    kt = (kk * jnp.exp2(sex)).astype(BF16)
    dec = jnp.exp2(tot[0:1, :])
    s_new = s_t * dec + _dot(v.T.astype(BF16), kt) * bd
    return o, s_new


def _hgrn_direction(rev, q_ref, f_ref, v_ref, lb, s0_ref, bd_ref, o_ref, sfin_ref, s_scr):
    i = pl.program_id(0)
    blk = (pl.num_programs(0) - 1 - i) if rev else i
    nct = N_CTX // HB
    per_c = SEQ // HB
    per_l = DEC_SEQ // HB
    is_ctx = blk < nct
    pos = jnp.where(is_ctx, blk % per_c, (blk - nct) % per_l)
    last = jnp.where(is_ctx, per_c - 1, per_l - 1)
    first_pos = last if rev else 0
    final_pos = 0 if rev else last

    @pl.when(pos == first_pos)
    def _():
        s_scr[...] = jnp.zeros_like(s_scr)

    @pl.when((pos == first_pos) & jnp.logical_not(is_ctx))
    def _():
        for h in range(HG_HEADS):
            hs = slice(HEAD_DIM * h, HEAD_DIM * (h + 1))
            s_scr[hs, hs] = s0_ref[h].T

    bd = bd_ref[...]
    s_new = s_scr[...]
    subs = range(HB // HSB)
    for sb in (reversed(subs) if rev else subs):
        rs = slice(HSB * sb, HSB * (sb + 1))
        o, s_new = _hgrn_subblock(rev, q_ref[rs, :], f_ref[rs, :], v_ref[rs, :], lb, s_new, bd)
        o_ref[rs, :] = o
    s_scr[...] = s_new

    @pl.when((pos == final_pos) & is_ctx)
    def _():
        for h in range(HG_HEADS):
            hs = slice(HEAD_DIM * h, HEAD_DIM * (h + 1))
            sfin_ref[h] = s_new[hs, hs].T


def _hgrn(z, lbs_l, state_hgrn, layer, bd):
    nb = NTOK // HB
    nct = N_CTX // HB
    rblk = lambda i: nb - 1 - i

    def seq_of(b):
        return jnp.where(b < nct, b // (SEQ // HB), BATCH + (b - nct) // (DEC_SEQ // HB))

    col = lambda blk, c: pl.BlockSpec((HB, HG_W), lambda i: (blk(i), c // HG_W))
    state = lambda blk, d: pl.BlockSpec(
        (None, None, None, HG_HEADS, HEAD_DIM, HEAD_DIM),
        lambda i: (jnp.maximum(seq_of(blk(i)) - BATCH, 0), layer, d, 0, 0, 0))
    final = lambda blk: pl.BlockSpec((None, HG_HEADS, HEAD_DIM, HEAD_DIM),
                                     lambda i: (jnp.minimum(seq_of(blk(i)), BATCH - 1), 0, 0, 0))
    fwd = lambda i: i
    return pl.pallas_call(
        _hgrn_kernel,
        grid=(nb,),
        in_specs=[
            col(fwd, COL_AQ), col(fwd, COL_AFF), col(fwd, COL_AI),
            col(rblk, COL_AQ), col(rblk, COL_AFB), col(rblk, COL_AI),
            pl.BlockSpec((2, 1, HG_W), lambda i: (0, 0, 0)),
            state(fwd, 0), state(rblk, 1),
            pl.BlockSpec((HG_W, HG_W), lambda i: (0, 0)),
        ],
        out_specs=[col(fwd, 0), col(rblk, 0), final(fwd), final(rblk)],
        out_shape=[
            jax.ShapeDtypeStruct((NTOK, HG_W), F32),
            jax.ShapeDtypeStruct((NTOK, HG_W), F32),
            jax.ShapeDtypeStruct((BATCH, HG_HEADS, HEAD_DIM, HEAD_DIM), F32),
            jax.ShapeDtypeStruct((BATCH, HG_HEADS, HEAD_DIM, HEAD_DIM), F32),
        ],
        scratch_shapes=[pltpu.VMEM((HG_W, HG_W), F32), pltpu.VMEM((HG_W, HG_W), F32)],
        compiler_params=_params(("arbitrary",)),
        name="hgrn",
    )(z, z, z, z, z, z, lbs_l.reshape(2, 1, HG_W), state_hgrn, state_hgrn, bd)


def _pool_kernel(u_ref, w_ref, sc_ref, o_ref, *, t_len):
    u = u_ref[...]
    row = lax.broadcasted_iota(jnp.int32, (t_len, POOL_W), 0)
    grp = lax.broadcasted_iota(jnp.int32, (t_len, POOL_W), 1) // HEAD_DIM
    half = jnp.left_shift(1, grp)
    acc = jnp.zeros_like(u)
    for j in range(-8, 8):
        src = row + j
        ok = (j >= -half) & (j < half) & (src >= 0) & (src < t_len)
        shifted = u if j == 0 else pltpu.roll(u, (-j) % t_len, 0)
        acc = acc + jnp.where(ok, shifted, 0.0)
    cnt = (jnp.minimum(row + half, t_len) - jnp.maximum(row - half, 0)).astype(F32)
    y = _dot((acc / cnt - u).astype(BF16), w_ref[...]) * sc_ref[...]
    o_ref[...] = y.astype(BF16)


def _pool(z, row0, nseq, t_len, w_bd, scale):
    return pl.pallas_call(
        functools.partial(_pool_kernel, t_len=t_len),
        grid=(nseq,),
        in_specs=[
            pl.BlockSpec((t_len, POOL_W), lambda b: (row0 // t_len + b, COL_CU // POOL_W)),
            pl.BlockSpec((POOL_W, POOL_W), lambda b: (0, 0)),
            pl.BlockSpec((1, POOL_W), lambda b: (0, 0)),
        ],
        out_specs=pl.BlockSpec((t_len, POOL_W), lambda b: (b, 0)),
        out_shape=jax.ShapeDtypeStruct((nseq * t_len, POOL_W), BF16),
        compiler_params=_params(("arbitrary",)),
        name="pool",
    )(z, w_bd, scale)


def _rope(x, cos, sin):
    w = x.shape[-1]
    lane = lax.broadcasted_iota(jnp.int32, x.shape, 1)
    up = pltpu.roll(x, w - 16, 1)
    dn = pltpu.roll(x, 16, 1)
    return x * cos + jnp.where((lane & 31) < 16, up, dn) * sin


def _prep_kernel(*refs, rope):
    if rope:
        (bq, bk, bv, dq, dk, dv, gq, gk, gdq, gdk, bd, cq, sq, ck, sk,
         oq, ok_, ov, odq, odk, odv) = refs
    else:
        (bq, bk, bv, dq, dk, dv, gq, gk, gdq, gdk, bd,
         oq, ok_, ov, odq, odk, odv, ok32, odk32, ov32, odv32) = refs
    bdm = bd[...]
    q = _head_rms(bq[...], bdm, gq[...])
    k = _head_rms(bk[...], bdm[:WK_B, :WK_B], gk[...])
    qd = _head_rms(dq[...], bdm[:WD, :WD], gdq[...])
    kd = _head_rms(dk[...], bdm[:WD, :WD], gdk[...])
    if rope:
        q = _rope(q, cq[...], sq[...])
        k = _rope(k, ck[...], sk[...])
    else:
        ok32[...] = k
        odk32[...] = kd
        ov32[...] = bv[...]
        odv32[...] = dv[...]
    oq[...] = (q * SCALE).astype(BF16)
    ok_[...] = k.astype(BF16)
    ov[...] = bv[...].astype(BF16)
    odq[...] = (qd * SCALE).astype(BF16)
    odk[...] = kd.astype(BF16)
    odv[...] = dv[...].astype(BF16)


def _prep(z, row0, nrows, gains, bd, rope_tabs):
    tm = 512
    nt = nrows // tm
    r0 = row0 // tm
    rope = rope_tabs is not None
    col = lambda c, w: (lambda i: (r0 + i, c // w))
    in_specs = [
        pl.BlockSpec((tm, WQ_B), col(COL_BQ, WQ_B)),
        pl.BlockSpec((tm, WK_B), col(COL_BK, WK_B)),
        pl.BlockSpec((tm, WK_B), col(COL_BV, WK_B)),
        pl.BlockSpec((tm, WD), col(COL_DQ, WD)),
        pl.BlockSpec((tm, WD), col(COL_DK, WD)),
        pl.BlockSpec((tm, WD), col(COL_DV, WD)),
        pl.BlockSpec((1, WQ_B), lambda i: (0, 0)),
        pl.BlockSpec((1, WK_B), lambda i: (0, 0)),
        pl.BlockSpec((1, WD), lambda i: (0, 0)),
        pl.BlockSpec((1, WD), lambda i: (0, 0)),
        pl.BlockSpec((WQ_B, WQ_B), lambda i: (0, 0)),
    ]
    args = [z, z, z, z, z, z, *gains, bd]
    per = DEC_SEQ // tm
    if rope:
        in_specs += [
            pl.BlockSpec((tm, WQ_B), lambda i: (i % per, 0)),
            pl.BlockSpec((tm, WQ_B), lambda i: (i % per, 0)),
            pl.BlockSpec((tm, WK_B), lambda i: (i % per, 0)),
            pl.BlockSpec((tm, WK_B), lambda i: (i % per, 0)),
        ]
        args += list(rope_tabs)
    widths = [WQ_B, WK_B, WK_B, WD, WD, WD]
    out_specs = [pl.BlockSpec((tm, w), lambda i: (i, 0)) for w in widths]
    out_shape = [jax.ShapeDtypeStruct((nrows, w), BF16) for w in widths]
    if not rope:
        out_specs += [pl.BlockSpec((tm, w), lambda i: (i, 0)) for w in (WK_B, WD, WK_B, WD)]
        out_shape += [jax.ShapeDtypeStruct((nrows, w), F32) for w in (WK_B, WD, WK_B, WD)]
    return pl.pallas_call(
        functools.partial(_prep_kernel, rope=rope),
        grid=(nt,),
        in_specs=in_specs,
        out_specs=out_specs,
        out_shape=out_shape,
        compiler_params=_params(("arbitrary",)),
        name="prep_lat" if rope else "prep_ctx",
    )(*args)


def _softmax_pv(scores, values, sink):
    m = scores[0].max(axis=-1, keepdims=True)
    for s in scores[1:]:
        m = jnp.maximum(m, s.max(axis=-1, keepdims=True))
    if sink is not None:
        m = jnp.maximum(m, sink)
    den = jnp.zeros_like(m) if sink is None else jnp.exp(sink - m)
    acc = None
    for s, v in zip(scores, values):
        p = jnp.exp(s - m)
        den = den + p.sum(axis=-1, keepdims=True)
        pv = _dot(p.astype(BF16), v)
        acc = pv if acc is None else acc + pv
    return acc / den


def _hs(x, h):
    return x[:, HEAD_DIM * h:HEAD_DIM * (h + 1)]


GQA = WIN_HEADS // WIN_KV


def _stack_group(q, sink_ref, kv, rows):
    qs = jnp.concatenate([_hs(q, GQA * kv + j) for j in range(GQA)], axis=0)
    part = lax.broadcasted_iota(jnp.int32, (GQA * rows, 1), 0) // rows
    sink = jnp.zeros((GQA * rows, 1), F32)
    for j in range(GQA):
        sink = jnp.where(part == j, sink_ref[0, GQA * kv + j], sink)
    return qs, sink


def _ctx_attn_kernel(sink_ref, q_ref, k_ref, v_ref, qd_ref, kd_ref, vd_ref, ob_ref, od_ref):
    q, k, v = q_ref[...], k_ref[...], v_ref[...]
    for kv in range(WIN_KV):
        qs, sink = _stack_group(q, sink_ref, kv, SEQ)
        s = _dot_nt(qs, _hs(k, kv))
        o = _softmax_pv([s], [_hs(v, kv)], sink)
        for j in range(GQA):
            h = GQA * kv + j
            ob_ref[:, HEAD_DIM * h:HEAD_DIM * (h + 1)] = o[SEQ * j:SEQ * (j + 1)].astype(BF16)
    qd, kd, vd = qd_ref[...], kd_ref[...], vd_ref[...]
    for h in range(NA_HEADS):
        s = _dot_nt(_hs(qd, h), _hs(kd, h))
        o = _softmax_pv([s], [_hs(vd, h)], None)
        od_ref[:, HEAD_DIM * h:HEAD_DIM * (h + 1)] = o.astype(BF16)


def _ctx_attn(sink, q, k, v, qd, kd, vd):
    blk = lambda w: pl.BlockSpec((SEQ, w), lambda b: (b, 0))
    return pl.pallas_call(
        _ctx_attn_kernel,
        grid=(BATCH,),
        in_specs=[pl.BlockSpec(memory_space=pltpu.SMEM),
                  blk(WQ_B), blk(WK_B), blk(WK_B), blk(WD), blk(WD), blk(WD)],
        out_specs=[blk(WQ_B), blk(WD)],
        out_shape=[jax.ShapeDtypeStruct((N_CTX, WQ_B), BF16), jax.ShapeDtypeStruct((N_CTX, WD), BF16)],
        compiler_params=_params(("arbitrary",)),
        name="ctx_attn",
    )(sink, q, k, v, qd, kd, vd)


WIN_SPAN = 3 * WIN


def _win_attn_kernel(sink_ref, q_ref, k_ref, v_ref, kc_ref, vc_ref, o_ref):
    qi = pl.program_id(1)
    start = pl.multiple_of(jnp.clip(qi * WIN - WIN, 0, DEC_SEQ - WIN_SPAN), WIN)
    q = q_ref[...]
    kw = k_ref[pl.ds(start, WIN_SPAN), :]
    vw = v_ref[pl.ds(start, WIN_SPAN), :]
    kc, vc = kc_ref[...], vc_ref[...]
    rows = GQA * WIN
    qpos = qi * WIN + lax.broadcasted_iota(jnp.int32, (rows, WIN_SPAN), 0) % WIN
    kpos = start + lax.broadcasted_iota(jnp.int32, (rows, WIN_SPAN), 1)
    valid = jnp.abs(qpos - kpos) <= WIN
    for kv in range(WIN_KV):
        qs, sink = _stack_group(q, sink_ref, kv, WIN)
        s_loc = jnp.where(valid, _dot_nt(qs, _hs(kw, kv)), NEG)
        s_ctx = _dot_nt(qs, _hs(kc, kv))
        o = _softmax_pv([s_loc, s_ctx], [_hs(vw, kv), _hs(vc, kv)], sink)
        for j in range(GQA):
            h = GQA * kv + j
            o_ref[:, HEAD_DIM * h:HEAD_DIM * (h + 1)] = o[WIN * j:WIN * (j + 1)].astype(BF16)


def _win_attn(sink, q, k, v, kc, vc):
    nq = DEC_SEQ // WIN
    return pl.pallas_call(
        _win_attn_kernel,
        grid=(DEC_BATCH, nq),
        in_specs=[
            pl.BlockSpec(memory_space=pltpu.SMEM),
            pl.BlockSpec((WIN, WQ_B), lambda b, i: (b * nq + i, 0)),
            pl.BlockSpec((None, DEC_SEQ, WK_B), lambda b, i: (b, 0, 0)),
            pl.BlockSpec((None, DEC_SEQ, WK_B), lambda b, i: (b, 0, 0)),
            pl.BlockSpec((None, PAST_LEN, WK_B), lambda b, i: (b, 0, 0)),
            pl.BlockSpec((None, PAST_LEN, WK_B), lambda b, i: (b, 0, 0)),
        ],
        out_specs=pl.BlockSpec((WIN, WQ_B), lambda b, i: (b * nq + i, 0)),
        out_shape=jax.ShapeDtypeStruct((N_LAT, WQ_B), BF16),
        compiler_params=_params(("arbitrary", "arbitrary")),
        name="win_attn",
    )(sink, q, k.reshape(DEC_BATCH, DEC_SEQ, WK_B), v.reshape(DEC_BATCH, DEC_SEQ, WK_B), kc, vc)


NA_ROWS = DEC_SEQ // GRID_W
NA_G = 4
NA_NG = NA_ROWS // NA_G
NA_UROWS = NA_KH + NA_G - 1
NA_UKEYS = NA_UROWS * GRID_W
NA_QROWS = NA_G * GRID_W


def _na_union_start(g):
    return jnp.clip(g * NA_G - NA_KH // 2, 0, NA_ROWS - NA_UROWS)


def _natten_kernel(q_ref, k_ref, v_ref, kc_ref, vc_ref, bias_ref, o_ref):
    start = pl.multiple_of(_na_union_start(pl.program_id(1)) * GRID_W, GRID_W)
    q = q_ref[...]
    kw = k_ref[pl.ds(start, NA_UKEYS), :]
    vw = v_ref[pl.ds(start, NA_UKEYS), :]
    kc, vc = kc_ref[...], vc_ref[...]
    for h in range(NA_HEADS):
        qh = _hs(q, h)
        s_loc = _dot_nt(qh, _hs(kw, h)) + bias_ref[h]
        s_ctx = _dot_nt(qh, _hs(kc, h))
        o = _softmax_pv([s_loc, s_ctx], [_hs(vw, h), _hs(vc, h)], None)
        o_ref[:, HEAD_DIM * h:HEAD_DIM * (h + 1)] = o.astype(BF16)


def _natten(q, k, v, kc, vc, bias):
    variant = lambda g: jnp.where(g == 0, 0, jnp.where(g == NA_NG - 1, 2, 1))
    return pl.pallas_call(
        _natten_kernel,
        grid=(DEC_BATCH, NA_NG),
        in_specs=[
            pl.BlockSpec((NA_QROWS, WD), lambda b, g: (b * NA_NG + g, 0)),
            pl.BlockSpec((None, DEC_SEQ, WD), lambda b, g: (b, 0, 0)),
            pl.BlockSpec((None, DEC_SEQ, WD), lambda b, g: (b, 0, 0)),
            pl.BlockSpec((None, PAST_LEN, WD), lambda b, g: (b, 0, 0)),
            pl.BlockSpec((None, PAST_LEN, WD), lambda b, g: (b, 0, 0)),
            pl.BlockSpec((None, NA_HEADS, NA_QROWS, NA_UKEYS), lambda b, g: (variant(g), 0, 0, 0)),
        ],
        out_specs=pl.BlockSpec((NA_QROWS, WD), lambda b, g: (b * NA_NG + g, 0)),
        out_shape=jax.ShapeDtypeStruct((N_LAT, WD), BF16),
        compiler_params=_params(("arbitrary", "arbitrary")),
        name="natten",
    )(q, k.reshape(DEC_BATCH, DEC_SEQ, WD), v.reshape(DEC_BATCH, DEC_SEQ, WD), kc, vc, bias)


def _natten_bias(rpb):
    c = np.arange(GRID_W)
    cstart = np.clip(c - NA_KW // 2, 0, GRID_W - NA_KW)
    inwin = (c[None, :] >= cstart[:, None]) & (c[None, :] < cstart[:, None] + NA_KW)
    dc = np.clip(c[None, :] - c[:, None] + NA_KW - 1, 0, 2 * NA_KW - 2)
    pick = (dc[None] == np.arange(2 * NA_KW - 1)[:, None, None]).astype(np.float32)
    toep = jnp.einsum('hdj,jck->hdck', rpb.astype(F32), jnp.asarray(pick),
                      precision=lax.Precision.HIGHEST)
    toep = jnp.where(inwin[None, None], toep, NEG)
    masked = jnp.full((NA_HEADS, GRID_W, GRID_W), NEG, F32)
    variants = []
    for g in (0, 1, NA_NG - 1):
        u0 = int(np.clip(g * NA_G - NA_KH // 2, 0, NA_ROWS - NA_UROWS))
        rows = []
        for j in range(NA_G):
            r = g * NA_G + j
            w0 = int(np.clip(r - NA_KH // 2, 0, NA_ROWS - NA_KH))
            blocks = []
            for i in range(NA_UROWS):
                kr = u0 + i
                blocks.append(toep[:, kr - r + NA_KH - 1] if w0 <= kr < w0 + NA_KH else masked)
            rows.append(jnp.concatenate(blocks, axis=-1))
        variants.append(jnp.concatenate(rows, axis=-2))
    return jnp.stack(variants, axis=0)


def _merge_kernel(xc_ref, xl_ref, m_ref, g_ref, of_ref, ob_ref, ag_ref, on_ref, bd_ref,
                  ybc_ref, ybl_ref, ycc_ref, ycl_ref, ydc_ref, ydl_ref,
                  wg_ref, bg_ref, wa_ref, wb_ref, wc_ref, wd_ref, wo_ref,
                  g2_ref, wr_ref, br_ref, wsg_ref, wsu_ref, wsd_ref, tri_ref, ut_ref,
                  xa_ref, sl_ref, slt_ref, cnt_ref, xs_ref):
    x = _pick_group(TM, xc_ref, xl_ref)
    h = _prenorm(x, g_ref[...], m_ref[0:1, :], m_ref[1:2, :]).astype(BF16)
    o = of_ref[...] + ob_ref[...]
    ya = _head_rms(o, bd_ref[...], on_ref[...]) * _silu(ag_ref[...])
    yb = _pick_group(TM, ybc_ref, ybl_ref)
    yc = _pick_group(TM, ycc_ref, ycl_ref)
    yd = _pick_group(TM, ydc_ref, ydl_ref)
    d = D_MODEL
    merged = jnp.zeros((TM, d), F32)
    branches = ((ya.astype(BF16), wa_ref), (yb, wb_ref), (yc, wc_ref), (yd, wd_ref))
    for b, (y, w_ref) in enumerate(branches):
        gate = jax.nn.sigmoid(_dot(h, wg_ref[:, d * b:d * (b + 1)]) + bg_ref[:, d * b:d * (b + 1)])
        merged = merged + gate * _dot(y, w_ref[...])
    x1 = x + m_ref[2:3, :] * _dot(merged.astype(BF16), wo_ref[...])
    for t in range(TM // TR):
        _route_rows(x1[TR * t:TR * (t + 1)], t, m_ref, g2_ref, wr_ref, br_ref, wsg_ref, wsu_ref, wsd_ref,
                    tri_ref, ut_ref, xa_ref, sl_ref, slt_ref, cnt_ref, xs_ref)


def _merge(x_c, x_l, mods, norm_g, o_f, o_b, z, onorm, bd, yb_c, yb_l, yc_c, yc_l, yd_c, yd_l,
           w_gate, b_gate, wa, wb, wc, wd, wo, norm2_g, w_router, b_router, wsg, wsu, wsd, tri, ut):
    full = lambda a: pl.BlockSpec(a.shape, lambda i: (0, 0))
    tpm = TM // TR
    return pl.pallas_call(
        _merge_kernel,
        grid=(NTOK // TM,),
        in_specs=_two_group_specs(TM, D_MODEL) + [
            pl.BlockSpec((None, 6, D_MODEL), lambda i: (_seg_of_tile(i, TM), 0, 0)),
            full(norm_g),
            pl.BlockSpec((TM, HG_W), lambda i: (i, 0)),
            pl.BlockSpec((TM, HG_W), lambda i: (i, 0)),
            pl.BlockSpec((TM, HG_W), lambda i: (i, COL_AG // HG_W)),
            full(onorm), full(bd),
        ] + _two_group_specs(TM, WQ_B) + _two_group_specs(TM, POOL_W) + _two_group_specs(TM, WD) + [
            full(w_gate), full(b_gate), full(wa), full(wb), full(wc), full(wd), full(wo),
            full(norm2_g), full(w_router), full(b_router), full(wsg), full(wsu), full(wsd), full(tri), full(ut),
        ],
        out_specs=[
            pl.BlockSpec((TM, XW), lambda i: (i, 0)),
            pl.BlockSpec((TM, E_PAD), lambda i: (i, 0)),
            pl.BlockSpec((tpm, 8, TR), lambda i: (i, 0, 0)),
            pl.BlockSpec((tpm, 1, E_PAD), lambda i: (i, 0, 0)),
            pl.BlockSpec((TM, D_MODEL), lambda i: (i, 0)),
        ],
        out_shape=[
            jax.ShapeDtypeStruct((NTOK, XW), BF16),
            jax.ShapeDtypeStruct((NTOK, E_PAD), F32),
            jax.ShapeDtypeStruct((NT_R, 8, TR), F32),
            jax.ShapeDtypeStruct((NT_R, 1, E_PAD), F32),
            jax.ShapeDtypeStruct((NTOK, D_MODEL), F32),
        ],
        compiler_params=_params(("arbitrary",)),
        name="merge",
    )(x_c, x_l, mods, norm_g, o_f, o_b, z, onorm, bd, yb_c, yb_l, yc_c, yc_l, yd_c, yd_l,
      w_gate, b_gate, wa, wb, wc, wd, wo, norm2_g, w_router, b_router, wsg, wsu, wsd, tri, ut)


E_PAD = 128
TR = 256
RT = 512
SLAB = 16
NT_R = NTOK // TR
S_LOC = 1536
S_MAX = NTOK * TOP_K + NT_R * N_EXPERTS * (SLAB - 1) + N_EXPERTS * (RT - SLAB)
N_XT = S_MAX // RT
XW = D_MODEL + 2 * E_PAD
assert S_LOC >= TR * TOP_K + N_EXPERTS * (SLAB - 1) and S_LOC % 128 == 0 and S_MAX % RT == 0


def _route_rows(x, t, m_ref, g_ref, wr_ref, br_ref, wsg_ref, wsu_ref, wsd_ref, tri_ref, ut_ref,
                xa_ref, sl_ref, slt_ref, cnt_ref, xs_ref):
    rows = slice(TR * t, TR * (t + 1))
    lane = lax.broadcasted_iota(jnp.int32, (TR, E_PAD), 1)
    ms = jnp.mean(x * x, axis=-1, keepdims=True)
    y = x * lax.rsqrt(ms + EPS) * g_ref[...]
    h = y * (1.0 + m_ref[4:5, :]) + m_ref[3:4, :]
    h_hi = h.astype(BF16)
    h_lo = (h - h_hi.astype(F32)).astype(BF16)
    w = wr_ref[...]
    w_hi = w.astype(BF16)
    w_lo = (w - w_hi.astype(F32)).astype(BF16)
    logits = _dot(h_hi, w_hi) + _dot(h_hi, w_lo) + _dot(h_lo, w_hi)
    scores = jax.nn.sigmoid(logits)
    sel = jnp.where(lane < N_EXPERTS, scores + br_ref[...], -jnp.inf)
    picked = jnp.zeros((TR, E_PAD), F32)
    hot = jnp.zeros((TR, E_PAD), F32)
    idxs = []
    lane_f = lane.astype(F32)
    for _ in range(TOP_K):
        mx = sel.max(axis=-1, keepdims=True)
        idx = jnp.min(jnp.where(sel == mx, lane_f, float(E_PAD)), axis=-1, keepdims=True)
        hit = lane_f == idx
        picked = jnp.where(hit, scores, picked)
        hot = jnp.where(hit, 1.0, hot)
        sel = jnp.where(hit, -jnp.inf, sel)
        idxs.append(idx)
    wts = ROUTED_SCALE * picked / picked.sum(axis=-1, keepdims=True)

    cnt = hot.sum(axis=0, keepdims=True)
    pad = jnp.floor((cnt + (SLAB - 1.0)) * (1.0 / SLAB)) * SLAB
    loc = _dot(jnp.broadcast_to(pad, (8, E_PAD)).astype(BF16), ut_ref[...])[0:1, :]
    rank = _dot(tri_ref[...], hot.astype(BF16))
    slotmat = loc + rank
    sl = jnp.zeros((TR, E_PAD), F32)
    for k in range(TOP_K):
        s_k = jnp.sum(jnp.where(lane_f == idxs[k], slotmat, 0.0), axis=-1, keepdims=True)
        sl = jnp.where(lane == k, s_k, sl)
    sl_ref[rows, :] = sl
    slt_ref[t] = sl.T[0:8, :]
    cnt_ref[t] = pad

    w16 = wts.astype(BF16)
    xa_ref[rows, 0:D_MODEL] = h_hi
    xa_ref[rows, D_MODEL:D_MODEL + E_PAD] = w16
    xa_ref[rows, D_MODEL + E_PAD:XW] = (wts - w16.astype(F32)).astype(BF16)

    a = _silu(_dot(h_hi, wsg_ref[...])) * _dot(h_hi, wsu_ref[...])
    xs_ref[rows, :] = x + m_ref[5:6, :] * _dot(a.astype(BF16), wsd_ref[...])


def _slab_copy(src, src_row, dst, dst_row, sem):
    hint = lambda r: r if isinstance(r, int) else pl.multiple_of(r, SLAB)
    return pltpu.make_async_copy(src.at[pl.ds(hint(src_row), SLAB), :],
                                 dst.at[pl.ds(hint(dst_row), SLAB), :], sem)


NSL = S_LOC // SLAB


def _for_each_slab(grow_ref, ns_ref, t, fn):
    def per_slab(n, c):
        fn(n * SLAB, grow_ref[t * NSL + n])
        return c

    lax.fori_loop(0, ns_ref[t], per_slab, 0)


def _compact_kernel(grow_ref, ns_ref, nd_ref, toff_ref, tn_ref, nu_ref,
                    xa_ref, slt_ref, xs_hbm, xc_scr, zero_scr, sem):
    t = pl.program_id(0)
    slot = t % 2

    def wait_n(n, s):
        def body(_, c):
            _slab_copy(xc_scr.at[s], 0, xs_hbm, 0, sem.at[s]).wait()
            return c
        lax.fori_loop(0, n, body, 0)

    def unused_tile_copy(j):
        row = pl.multiple_of((nu_ref[0] + j) * RT, RT)
        return pltpu.make_async_copy(zero_scr, xs_hbm.at[pl.ds(row, RT), :], sem.at[2])

    @pl.when(t == 0)
    def _():
        zero_scr[...] = jnp.zeros_like(zero_scr)

        def body(j, c):
            unused_tile_copy(j).start()
            return c
        lax.fori_loop(0, N_XT - nu_ref[0], body, 0)

    @pl.when(t >= 2)
    def _():
        wait_n(nd_ref[t - 2], slot)

    row = lax.broadcasted_iota(jnp.int32, (S_LOC, TR), 0)
    slt = slt_ref[...].astype(jnp.int32)
    hit = row == slt[0:1, :]
    for k in range(1, TOP_K):
        hit = hit | (row == slt[k:k + 1, :])
    onehot = jnp.where(hit, 1.0, 0.0).astype(BF16)
    xc_scr[slot] = _dot(onehot, xa_ref[...]).astype(BF16)

    _for_each_slab(grow_ref, ns_ref, t,
                   lambda lr, gr: _slab_copy(xc_scr.at[slot], lr, xs_hbm, gr, sem.at[slot]).start())

    @pl.when(t < N_EXPERTS)
    def _():
        def body(s, c):
            _slab_copy(zero_scr, 0, xs_hbm, toff_ref[t] + s * SLAB, sem.at[slot]).start()
            return c
        lax.fori_loop(0, tn_ref[t], body, 0)

    @pl.when(t == NT_R - 1)
    def _():
        wait_n(nd_ref[t], slot)
        wait_n(nd_ref[t - 1], 1 - slot)

        def body(j, c):
            unused_tile_copy(j).wait()
            return c
        lax.fori_loop(0, N_XT - nu_ref[0], body, 0)


def _compact(meta, xa, slt):
    grid_spec = pltpu.PrefetchScalarGridSpec(
        num_scalar_prefetch=6,
        grid=(NT_R,),
        in_specs=[
            pl.BlockSpec((TR, XW), lambda i, *_: (i, 0)),
            pl.BlockSpec((None, 8, TR), lambda i, *_: (i, 0, 0)),
        ],
        out_specs=pl.BlockSpec(memory_space=pl.ANY),
        scratch_shapes=[pltpu.VMEM((2, S_LOC, XW), BF16), pltpu.VMEM((RT, XW), BF16),
                        pltpu.SemaphoreType.DMA((3,))],
    )
    return pl.pallas_call(
        _compact_kernel,
        grid_spec=grid_spec,
        out_shape=jax.ShapeDtypeStruct((S_MAX, XW), BF16),
        compiler_params=_params(("arbitrary",)),
        name="compact",
    )(meta['grow'], meta['ns'], meta['nd'], meta['toff'], meta['tn'], meta['nu'], xa, slt)


def _expert_kernel(te_ref, ti_ref, nu_ref, nx_ref, par_ref, xs_ref, wg_hbm, wu_hbm, wd_hbm, y_ref,
                   wg_f, wu_f, wd_f, wgu_s, wd_s, sem, *, layer):
    i = pl.program_id(0)
    e = te_ref[i]
    slot = par_ref[i]
    used = i < nu_ref[0]
    first = used & ((i == 0) | (e != te_ref[jnp.maximum(i - 1, 0)]))

    def weight_copies(expert, s):
        return (pltpu.make_async_copy(wg_hbm.at[layer, expert], wg_f.at[s], sem.at[s]),
                pltpu.make_async_copy(wu_hbm.at[layer, expert], wu_f.at[s], sem.at[s]),
                pltpu.make_async_copy(wd_hbm.at[layer, expert], wd_f.at[s], sem.at[s]))

    @pl.when(i == 0)
    def _():
        for c in weight_copies(e, slot):
            c.start()

    @pl.when(first)
    def _():
        for c in weight_copies(e, slot):
            c.wait()
        wgu_s[:, 0:D_EXPERT] = wg_f[slot].astype(BF16)
        wgu_s[:, D_EXPERT:2 * D_EXPERT] = wu_f[slot].astype(BF16)
        wd_s[...] = wd_f[slot].astype(BF16)

    @pl.when(first & (nx_ref[i] >= 0))
    def _():
        for c in weight_copies(nx_ref[i], 1 - slot):
            c.start()

    @pl.when(used)
    def _():
        x = xs_ref[:, 0:D_MODEL]
        gw = xs_ref[:, D_MODEL:D_MODEL + E_PAD].astype(F32) + xs_ref[:, D_MODEL + E_PAD:XW].astype(F32)
        lane = lax.broadcasted_iota(jnp.int32, (RT, E_PAD), 1)
        ge = jnp.sum(jnp.where(lane == e, gw, 0.0), axis=-1, keepdims=True)
        gu = _dot(x, wgu_s[...])
        a = _silu(gu[:, 0:D_EXPERT]) * gu[:, D_EXPERT:2 * D_EXPERT]
        y_ref[...] = _dot((a * ge).astype(BF16), wd_s[...]).astype(BF16)

    @pl.when(i >= nu_ref[0])
    def _():
        y_ref[...] = jnp.zeros_like(y_ref)


def _experts(meta, layer, xs, w_eg, w_eu, w_ed):
    any_spec = pl.BlockSpec(memory_space=pl.ANY)
    grid_spec = pltpu.PrefetchScalarGridSpec(
        num_scalar_prefetch=5,
        grid=(N_XT,),
        in_specs=[pl.BlockSpec((RT, XW), lambda i, te, ti, *_: (ti[i], 0)), any_spec, any_spec, any_spec],
        out_specs=pl.BlockSpec((RT, D_MODEL), lambda i, *_: (i, 0)),
        scratch_shapes=[pltpu.VMEM((2, D_MODEL, D_EXPERT), F32), pltpu.VMEM((2, D_MODEL, D_EXPERT), F32),
                        pltpu.VMEM((2, D_EXPERT, D_MODEL), F32),
                        pltpu.VMEM((D_MODEL, 2 * D_EXPERT), BF16), pltpu.VMEM((D_EXPERT, D_MODEL), BF16),
                        pltpu.SemaphoreType.DMA((2,))],
    )
    return pl.pallas_call(
        functools.partial(_expert_kernel, layer=layer),
        grid_spec=grid_spec,
        out_shape=jax.ShapeDtypeStruct((S_MAX, D_MODEL), BF16),
        compiler_params=_params(("arbitrary",)),
        name="experts",
    )(meta['te'], meta['ti'], meta['nu'], meta['nx'], meta['par'], xs, w_eg, w_eu, w_ed)


def _combine_kernel(grow_ref, ns_ref, xs_ref, m_ref, sl_ref, y_hbm, oc_ref, ol_ref, yc_scr, sem):
    t = pl.program_id(0)
    slot = t % 2

    def issue(tt, s):
        _for_each_slab(grow_ref, ns_ref, tt,
                       lambda lr, gr: _slab_copy(y_hbm, gr, yc_scr.at[s], lr, sem.at[s]).start())

    @pl.when(t == 0)
    def _():
        yc_scr[...] = jnp.zeros_like(yc_scr)
        issue(0, 0)

    @pl.when(t + 1 < NT_R)
    def _():
        issue(t + 1, 1 - slot)

    def wait_body(_, c):
        _slab_copy(y_hbm, 0, yc_scr.at[slot], 0, sem.at[slot]).wait()
        return c
    lax.fori_loop(0, ns_ref[t], wait_body, 0)

    col = lax.broadcasted_iota(jnp.int32, (TR, S_LOC), 1)
    sl = sl_ref[...].astype(jnp.int32)
    hit = col == sl[:, 0:1]
    for k in range(1, TOP_K):
        hit = hit | (col == sl[:, k:k + 1])
    onehot = jnp.where(hit, 1.0, 0.0).astype(BF16)
    out = xs_ref[...] + m_ref[5:6, :] * _dot(onehot, yc_scr[slot])

    @pl.when(t < N_CTX // TR)
    def _():
        oc_ref[...] = out

    @pl.when(t >= N_CTX // TR)
    def _():
        ol_ref[...] = out


def _combine(meta, xsh, mods, sl, y):
    nct = N_CTX // TR
    out_specs = [pl.BlockSpec((TR, D_MODEL), lambda i, *_: (jnp.minimum(i, nct - 1), 0)),
                 pl.BlockSpec((TR, D_MODEL), lambda i, *_: (jnp.maximum(i - nct, 0), 0))]
    out_shape = [jax.ShapeDtypeStruct((N_CTX, D_MODEL), F32), jax.ShapeDtypeStruct((N_LAT, D_MODEL), F32)]
    grid_spec = pltpu.PrefetchScalarGridSpec(
        num_scalar_prefetch=2,
        grid=(NT_R,),
        in_specs=[
            pl.BlockSpec((TR, D_MODEL), lambda i, *_: (i, 0)),
            pl.BlockSpec((None, 6, D_MODEL), lambda i, *_: (_seg_of_tile(i, TR), 0, 0)),
            pl.BlockSpec((TR, E_PAD), lambda i, *_: (i, 0)),
            pl.BlockSpec(memory_space=pl.ANY),
        ],
        out_specs=out_specs,
        scratch_shapes=[pltpu.VMEM((2, S_LOC, D_MODEL), BF16), pltpu.SemaphoreType.DMA((2,))],
    )
    return pl.pallas_call(
        _combine_kernel,
        grid_spec=grid_spec,
        out_shape=out_shape,
        compiler_params=_params(("arbitrary",)),
        name="combine",
    )(meta['grow'], meta['ns'], xsh, mods, sl, y)


def _route_meta(cnt):
    pc = cnt[:, 0, :N_EXPERTS].astype(jnp.int32)
    tot = pc.sum(axis=0)
    tot_pad = ((tot + RT - 1) // RT) * RT
    ends = jnp.cumsum(tot_pad)
    base = ends - tot_pad
    dst = base[None, :] + jnp.cumsum(pc, axis=0) - pc
    cum = jnp.cumsum(pc, axis=1)
    ns = cum[:, -1] // SLAB
    lrow = jnp.arange(NSL, dtype=jnp.int32)[None, :, None] * SLAB
    owner = jnp.minimum(jnp.sum(cum[:, None, :] <= lrow, axis=2), N_EXPERTS - 1)
    mine = owner[:, :, None] == jnp.arange(N_EXPERTS, dtype=jnp.int32)[None, None, :]
    grow = jnp.sum(jnp.where(mine, (dst - cum + pc)[:, None, :], 0), axis=2) + lrow[:, :, 0]
    tn = (tot_pad - tot) // SLAB
    nd = ns + jnp.pad(tn, (0, NT_R - N_EXPERTS))
    n_used = ends[-1] // RT
    ti = jnp.minimum(jnp.arange(N_XT, dtype=jnp.int32), n_used - 1)
    te = jnp.minimum(jnp.sum(ends[None, :] <= (ti * RT)[:, None], axis=1), N_EXPERTS - 1)
    eidx = jnp.arange(N_EXPERTS, dtype=jnp.int32)
    has = tot_pad > 0
    later = has[None, :] & (eidx[None, :] > eidx[:, None])
    nxt_e = jnp.min(jnp.where(later, eidx[None, :], N_EXPERTS), axis=1)
    nxt_e = jnp.where(nxt_e == N_EXPERTS, -1, nxt_e)
    ordinal = jnp.cumsum(has.astype(jnp.int32)) - 1
    is_e = te[:, None] == eidx[None, :]
    nx = jnp.sum(jnp.where(is_e, nxt_e[None, :], 0), axis=1)
    par = jnp.sum(jnp.where(is_e, ordinal[None, :], 0), axis=1) & 1
    i32 = lambda a: a.astype(jnp.int32)
    return dict(grow=i32(grow.reshape(-1)), ns=i32(ns), nd=i32(nd), toff=i32(base + tot), tn=i32(tn),
                te=i32(te), ti=i32(ti), nu=i32(n_used.reshape(1)), nx=i32(nx), par=i32(par))


def _moe_routed(layer, routed, mods, w_eg, w_eu, w_ed):
    xa, sl, slt, cnt, xsh = routed
    meta = _route_meta(cnt)
    xs = _compact(meta, xa, slt)
    y = _experts(meta, layer, xs, w_eg, w_eu, w_ed)
    return _combine(meta, xsh, mods, sl, y)


def _rope_tables(width):
    t = np.arange(DEC_SEQ)
    quarter = HEAD_DIM // 4
    inv = (ROPE_BASE ** (-np.arange(quarter) / quarter)).astype(np.float32)
    ang_r = (t // GRID_W).astype(np.float32)[:, None] * inv[None]
    ang_c = (t % GRID_W).astype(np.float32)[:, None] * inv[None]
    cos = np.concatenate([np.cos(ang_r), np.cos(ang_r), np.cos(ang_c), np.cos(ang_c)], axis=1)
    sin = np.concatenate([-np.sin(ang_r), np.sin(ang_r), -np.sin(ang_c), np.sin(ang_c)], axis=1)
    reps = width // HEAD_DIM
    return (jnp.asarray(np.tile(cos, (1, reps)), F32), jnp.asarray(np.tile(sin, (1, reps)), F32))


def _permute_w_in(w):
    a = w[:, 0:1280]
    bq, bk, bv = w[:, 1280:1664], w[:, 1664:1792], w[:, 1792:1920]
    cu = w[:, 1920:2176]
    d = w[:, 2176:2944]
    pad = jnp.zeros((w.shape[0], Z_W - 2944), w.dtype)
    return jnp.concatenate([a, cu, d, bq, bk, bv, pad], axis=1)


def _block_diag(blocks):
    g = blocks.shape[0]
    eye = jnp.eye(g, dtype=blocks.dtype)
    return jnp.einsum('gh,gij->gihj', eye, blocks).reshape(g * HEAD_DIM, g * HEAD_DIM)


def kernel(x_prompt, x_sample, cache_win_k, cache_win_v, cache_na_k, cache_na_v, state_hgrn, c, c_ctx, w_mod, b_mod, norm1_g, norm2_g, w_in, w_mgate, b_mgate, hg_lb, hg_onorm, win_qn, win_kn, win_sink, pool_w, pool_scale, na_qn, na_kn, na_rpb, w_branch, w_out, w_router, b_router, w_eg, w_eu, w_ed, w_sg, w_su, w_sd):
    lbp = jax.nn.softmax(hg_lb.astype(F32), axis=0)
    lbs = jnp.cumsum(lbp, axis=0) - lbp[0:1]

    cvec8 = jnp.concatenate([c_ctx[None], c, jnp.zeros((3, D_MODEL), F32)], axis=0)
    mods_all = _modulation(cvec8, w_mod, b_mod).reshape(DEPTH, 8, 6, D_MODEL)

    bd384 = jnp.asarray(_bd_ones(WQ_B), BF16)
    bd256 = bd384[:HG_W, :HG_W]
    bd256_f32 = jnp.asarray(_bd_ones(HG_W), F32)
    rope_q = _rope_tables(WQ_B)
    rope_k = _rope_tables(WK_B)
    tile = lambda g, reps: jnp.tile(g, reps)[None, :]
    tri = jnp.asarray(np.tril(np.ones((TR, TR), np.float32), -1), BF16)
    ut = jnp.asarray(np.triu(np.ones((E_PAD, E_PAD), np.float32), 1), BF16)

    x_c, x_l = x_prompt.reshape(N_CTX, D_MODEL), x_sample.reshape(N_LAT, D_MODEL)
    new_k, new_v, new_kd, new_vd, new_s = [], [], [], [], []
    for l in range(DEPTH):
        mods = mods_all[l]
        z = _projection(x_c, x_l, mods, norm1_g[l][None], _permute_w_in(w_in[l]).astype(BF16))

        o_f, o_b, sfin_f, sfin_b = _hgrn(z, lbs[l], state_hgrn.astype(F32), l, bd256_f32)
        new_s.append(jnp.stack([sfin_f, sfin_b], axis=1))

        w_pool = _block_diag(pool_w[l]).astype(BF16)
        yc_c = _pool(z, 0, BATCH, SEQ, w_pool, pool_scale[l][None])
        yc_l = _pool(z, N_CTX, DEC_BATCH, DEC_SEQ, w_pool, pool_scale[l][None])

        gains = (tile(win_qn[l], WIN_HEADS), tile(win_kn[l], WIN_KV), tile(na_qn[l], NA_HEADS), tile(na_kn[l], NA_HEADS))
        qb_c, kb_c, vb_c, qd_c, kd_c, vd_c, kb32, kd32, vb32, vd32 = _prep(z, 0, N_CTX, gains, bd384, None)
        qb_l, kb_l, vb_l, qd_l, kd_l, vd_l = _prep(z, N_CTX, N_LAT, gains, bd384, rope_q + rope_k)
        sink = win_sink[l][None]
        yb_c, yd_c = _ctx_attn(sink, qb_c, kb_c, vb_c, qd_c, kd_c, vd_c)
        kc = cache_win_k[:, l].reshape(DEC_BATCH, PAST_LEN, WK_B).astype(BF16)
        vc = cache_win_v[:, l].reshape(DEC_BATCH, PAST_LEN, WK_B).astype(BF16)
        yb_l = _win_attn(sink, qb_l, kb_l, vb_l, kc, vc)
        kcd = cache_na_k[:, l].reshape(DEC_BATCH, PAST_LEN, WD).astype(BF16)
        vcd = cache_na_v[:, l].reshape(DEC_BATCH, PAST_LEN, WD).astype(BF16)
        yd_l = _natten(qd_l, kd_l, vd_l, kcd, vcd, _natten_bias(na_rpb[l]))

        new_k.append(kb32.reshape(BATCH, SEQ, WIN_KV, HEAD_DIM))
        new_v.append(vb32.reshape(BATCH, SEQ, WIN_KV, HEAD_DIM))
        new_kd.append(kd32.reshape(BATCH, SEQ, NA_HEADS, HEAD_DIM))
        new_vd.append(vd32.reshape(BATCH, SEQ, NA_HEADS, HEAD_DIM))

        wbr = w_branch[l].astype(BF16)
        wr = jnp.pad(w_router[l], ((0, 0), (0, E_PAD - N_EXPERTS)))
        br = jnp.pad(b_router[l], (0, E_PAD - N_EXPERTS))[None]
        routed = _merge(x_c, x_l, mods, norm1_g[l][None], o_f, o_b, z, tile(hg_onorm[l], HG_HEADS), bd256,
                        yb_c, yb_l, yc_c, yc_l, yd_c, yd_l, w_mgate[l].astype(BF16), b_mgate[l][None],
                        wbr[0:256], wbr[256:640], wbr[640:896], wbr[896:1152], w_out[l].astype(BF16),
                        norm2_g[l][None], wr, br, w_sg[l].astype(BF16), w_su[l].astype(BF16),
                        w_sd[l].astype(BF16), tri, ut)
        x_c, x_l = _moe_routed(l, routed, mods, w_eg, w_eu, w_ed)

    y_p = x_c.reshape(BATCH, SEQ, D_MODEL)
    y_s = x_l.reshape(DEC_BATCH, DEC_SEQ, D_MODEL)
    return (y_p, y_s, jnp.stack(new_k, axis=1), jnp.stack(new_v, axis=1), jnp.stack(new_kd, axis=1),
            jnp.stack(new_vd, axis=1), jnp.stack(new_s, axis=1))
```

```python
import functools

import numpy as np
import jax
import jax.numpy as jnp
from jax import lax
from jax.experimental import pallas as pl
from jax.experimental.pallas import tpu as pltpu

F32 = jnp.float32
BF16 = jnp.bfloat16

D_MODEL = 1024
BATCH = 16
SEQ = 256
DEPTH = 2
DEC_BATCH = 4
DEC_SEQ = 2048
PAST_LEN = 256
NEG = -1e30
GRID_W = 64
HEAD_DIM = 64
SCALE = HEAD_DIM ** -0.5
ROPE_BASE = 10000.0
EPS = 1e-6
HG_HEADS = 4
HG_W = 256
WIN_HEADS = 6
WIN_KV = 2
WIN = 128
POOL_SIZES = (2, 4, 8, 16)
POOL_W = 256
NA_HEADS = 4
NA_KH = 8
NA_KW = 16
N_EXPERTS = 32
TOP_K = 4
D_EXPERT = 256
ROUTED_SCALE = 2.5

N_CTX = BATCH * SEQ
N_LAT = DEC_BATCH * DEC_SEQ
NTOK = N_CTX + N_LAT
WQ_B = WIN_HEADS * HEAD_DIM
WK_B = WIN_KV * HEAD_DIM
WD = NA_HEADS * HEAD_DIM

Z_W = 3072
COL_AQ, COL_AFF, COL_AFB, COL_AI, COL_AG, COL_CU, COL_DQ, COL_DK, COL_DV = (
    0, 256, 512, 768, 1024, 1280, 1536, 1792, 2048)
COL_BQ, COL_BK, COL_BV = 2304, 2688, 2816

TM = 512
TMP = 2048
TN = 512
HB = 256
VMEM_LIMIT = 56 * 1024 * 1024


def _params(sem, vmem=VMEM_LIMIT):
    return pltpu.CompilerParams(dimension_semantics=sem, vmem_limit_bytes=vmem)


def _seg_of_tile(i, tile):
    nct = N_CTX // tile
    per = DEC_SEQ // tile
    return jnp.where(i < nct, 0, 1 + (i - nct) // per)


def _bd_ones(w):
    idx = np.arange(w) // HEAD_DIM
    return (idx[:, None] == idx[None, :]).astype(np.float32)


def _dot(a, b):
    return jnp.dot(a, b, preferred_element_type=F32)


def _dot_nt(a, b):
    return lax.dot_general(a, b, (((1,), (1,)), ((), ())), preferred_element_type=F32)


def _split_dot(x, w_bf16):
    hi = x.astype(BF16)
    lo = (x - hi.astype(F32)).astype(BF16)
    return _dot(hi, w_bf16) + _dot(lo, w_bf16)


def _head_rms(x, bd, gain):
    ms = _split_dot(x * x, bd) * (1.0 / HEAD_DIM)
    return x * lax.rsqrt(ms + EPS) * gain


def _silu(x):
    return x * jax.nn.sigmoid(x)


def _mod_kernel(c_ref, w_ref, b_ref, o_ref):
    c = c_ref[...]
    a = _silu(c).astype(BF16)
    o_ref[...] = _dot(a, w_ref[...].astype(BF16)) + b_ref[...]


def _modulation(cvec8, w_mod, b_mod):
    n = 6 * D_MODEL
    tn = 1536
    return pl.pallas_call(
        _mod_kernel,
        grid=(DEPTH, n // tn),
        in_specs=[
            pl.BlockSpec((8, D_MODEL), lambda l, j: (0, 0)),
            pl.BlockSpec((None, D_MODEL, tn), lambda l, j: (l, 0, j)),
            pl.BlockSpec((None, 1, tn), lambda l, j: (l, 0, j)),
        ],
        out_specs=pl.BlockSpec((None, 8, tn), lambda l, j: (l, 0, j)),
        out_shape=jax.ShapeDtypeStruct((DEPTH, 8, n), F32),
        compiler_params=_params(("arbitrary", "arbitrary")),
        name="modulation",
    )(cvec8, w_mod, b_mod.reshape(DEPTH, 1, n))


def _prenorm(x, gain, shift, scale):
    ms = jnp.mean(x * x, axis=-1, keepdims=True)
    return x * lax.rsqrt(ms + EPS) * gain * (1.0 + scale) + shift


def _two_group_specs(tile, width, nargs=1):
    nct = N_CTX // tile
    if nargs == 1:
        return [pl.BlockSpec((tile, width), lambda i: (jnp.minimum(i, nct - 1), 0)),
                pl.BlockSpec((tile, width), lambda i: (jnp.maximum(i - nct, 0), 0))]
    return [pl.BlockSpec((tile, width), lambda i, j: (jnp.minimum(i, nct - 1), 0)),
            pl.BlockSpec((tile, width), lambda i, j: (jnp.maximum(i - nct, 0), 0))]


def _pick_group(tile, c_ref, l_ref):
    return jnp.where(pl.program_id(0) < N_CTX // tile, c_ref[...], l_ref[...])


def _proj_kernel(xc_ref, xl_ref, m_ref, g_ref, win_ref, z_ref, h_scr):
    @pl.when(pl.program_id(1) == 0)
    def _():
        x = _pick_group(TMP, xc_ref, xl_ref)
        h_scr[...] = _prenorm(x, g_ref[...], m_ref[0:1, :], m_ref[1:2, :]).astype(BF16)

    z_ref[...] = _dot(h_scr[...], win_ref[...])


def _projection(x_c, x_l, mods, norm_g, w_in_p):
    return pl.pallas_call(
        _proj_kernel,
        grid=(NTOK // TMP, Z_W // TN),
        in_specs=_two_group_specs(TMP, D_MODEL, nargs=2) + [
            pl.BlockSpec((None, 6, D_MODEL), lambda i, j: (_seg_of_tile(i, TMP), 0, 0)),
            pl.BlockSpec((1, D_MODEL), lambda i, j: (0, 0)),
            pl.BlockSpec((D_MODEL, TN), lambda i, j: (0, j)),
        ],
        out_specs=pl.BlockSpec((TMP, TN), lambda i, j: (i, j)),
        out_shape=jax.ShapeDtypeStruct((NTOK, Z_W), F32),
        scratch_shapes=[pltpu.VMEM((TMP, D_MODEL), BF16)],
        compiler_params=_params(("arbitrary", "arbitrary")),
        name="projection",
    )(x_c, x_l, mods, norm_g, w_in_p)


def _hgrn_kernel(qf_ref, ff_ref, vf_ref, qb_ref, fb_ref, vb_ref, lb_ref, s0f_ref, s0b_ref, bd_ref,
                 of_ref, ob_ref, sff_ref, sfb_ref, sf_scr, sb_scr):
    _hgrn_direction(False, qf_ref, ff_ref, vf_ref, lb_ref[0], s0f_ref, bd_ref, of_ref, sff_ref, sf_scr)
    _hgrn_direction(True, qb_ref, fb_ref, vb_ref, lb_ref[1], s0b_ref, bd_ref, ob_ref, sfb_ref, sb_scr)


HSB = 128


def _hgrn_subblock(rev, q, zf, v, lb, s_t, bd):
    n = HSB
    f = lb + (1.0 - lb) * jax.nn.sigmoid(zf)
    lf = jnp.log2(f)
    kk = 1.0 - f

    row = lax.broadcasted_iota(jnp.int32, (n, HG_W), 0)
    tq = lax.broadcasted_iota(jnp.int32, (n, n), 0)
    tk = lax.broadcasted_iota(jnp.int32, (n, n), 1)

    def before(x, m):
        return pltpu.roll(x, (n - m) if rev else m, 0)

    def after(x, m):
        return pltpu.roll(x, m if rev else (n - m), 0)

    q16 = q.astype(BF16)
    k16 = kk.astype(BF16)
    att = [jnp.where(tq == tk, _dot_nt(_hs(q16, h), _hs(k16, h)), 0.0) for h in range(HG_HEADS)]

    tot = lf
    pin = lf
    sex = jnp.zeros_like(lf)
    m = 1
    while m < n:
        late = ((row & (2 * m - 1)) < m) if rev else ((row & (2 * m - 1)) >= m)
        qm = jnp.where(late, q * jnp.exp2(pin), 0.0).astype(BF16)
        km = jnp.where(late, 0.0, kk * jnp.exp2(sex)).astype(BF16)
        shift = (2 * m).bit_length() - 1
        same = (tq >> shift) == (tk >> shift)
        for h in range(HG_HEADS):
            sc = _dot_nt(_hs(qm, h), _hs(km, h))
            att[h] = att[h] + (sc if 2 * m == n else jnp.where(same, sc, 0.0))
        tb = before(tot, m)
        ta = after(tot, m)
        pin = pin + jnp.where(late, tb, 0.0)
        sex = sex + jnp.where(late, 0.0, ta)
        tot = tot + jnp.where(late, tb, ta)
        m *= 2

    lane_head = lax.broadcasted_iota(jnp.int32, (n, HG_W), 1) // HEAD_DIM
    o = _dot_nt((q * jnp.exp2(pin)).astype(BF16), s_t.astype(BF16))
    for h in range(HG_HEADS):
        vh = jnp.where(lane_head == h, v, 0.0).astype(BF16)
        o = o + _dot(att[h].astype(BF16), vh)

    kt = (kk * jnp.exp2(sex)).astype(BF16)
    dec = jnp.exp2(tot[0:1, :])
    s_new = s_t * dec + _dot(v.T.astype(BF16), kt) * bd
    return o, s_new


def _hgrn_direction(rev, q_ref, f_ref, v_ref, lb, s0_ref, bd_ref, o_ref, sfin_ref, s_scr):
    i = pl.program_id(0)
    blk = (pl.num_programs(0) - 1 - i) if rev else i
    nct = N_CTX // HB
    per_c = SEQ // HB
    per_l = DEC_SEQ // HB
    is_ctx = blk < nct
    pos = jnp.where(is_ctx, blk % per_c, (blk - nct) % per_l)
    last = jnp.where(is_ctx, per_c - 1, per_l - 1)
    first_pos = last if rev else 0
    final_pos = 0 if rev else last

    @pl.when(pos == first_pos)
    def _():
        s_scr[...] = jnp.zeros_like(s_scr)

    @pl.when((pos == first_pos) & jnp.logical_not(is_ctx))
    def _():
        for h in range(HG_HEADS):
            hs = slice(HEAD_DIM * h, HEAD_DIM * (h + 1))
            s_scr[hs, hs] = s0_ref[h].T

    bd = bd_ref[...]
    s_new = s_scr[...]
    subs = range(HB // HSB)
    for sb in (reversed(subs) if rev else subs):
        rs = slice(HSB * sb, HSB * (sb + 1))
        o, s_new = _hgrn_subblock(rev, q_ref[rs, :], f_ref[rs, :], v_ref[rs, :], lb, s_new, bd)
        o_ref[rs, :] = o
    s_scr[...] = s_new

    @pl.when((pos == final_pos) & is_ctx)
    def _():
        for h in range(HG_HEADS):
            hs = slice(HEAD_DIM * h, HEAD_DIM * (h + 1))
            sfin_ref[h] = s_new[hs, hs].T


def _hgrn(z, lbs_l, state_hgrn, layer, bd):
    nb = NTOK // HB
    nct = N_CTX // HB
    rblk = lambda i: nb - 1 - i

    def seq_of(b):
        return jnp.where(b < nct, b // (SEQ // HB), BATCH + (b - nct) // (DEC_SEQ // HB))

    col = lambda blk, c: pl.BlockSpec((HB, HG_W), lambda i: (blk(i), c // HG_W))
    state = lambda blk, d: pl.BlockSpec(
        (None, None, None, HG_HEADS, HEAD_DIM, HEAD_DIM),
        lambda i: (jnp.maximum(seq_of(blk(i)) - BATCH, 0), layer, d, 0, 0, 0))
    final = lambda blk: pl.BlockSpec((None, HG_HEADS, HEAD_DIM, HEAD_DIM),
                                     lambda i: (jnp.minimum(seq_of(blk(i)), BATCH - 1), 0, 0, 0))
    fwd = lambda i: i
    return pl.pallas_call(
        _hgrn_kernel,
        grid=(nb,),
        in_specs=[
            col(fwd, COL_AQ), col(fwd, COL_AFF), col(fwd, COL_AI),
            col(rblk, COL_AQ), col(rblk, COL_AFB), col(rblk, COL_AI),
            pl.BlockSpec((2, 1, HG_W), lambda i: (0, 0, 0)),
            state(fwd, 0), state(rblk, 1),
            pl.BlockSpec((HG_W, HG_W), lambda i: (0, 0)),
        ],
        out_specs=[col(fwd, 0), col(rblk, 0), final(fwd), final(rblk)],
        out_shape=[
            jax.ShapeDtypeStruct((NTOK, HG_W), F32),
            jax.ShapeDtypeStruct((NTOK, HG_W), F32),
            jax.ShapeDtypeStruct((BATCH, HG_HEADS, HEAD_DIM, HEAD_DIM), F32),
            jax.ShapeDtypeStruct((BATCH, HG_HEADS, HEAD_DIM, HEAD_DIM), F32),
        ],
        scratch_shapes=[pltpu.VMEM((HG_W, HG_W), F32), pltpu.VMEM((HG_W, HG_W), F32)],
        compiler_params=_params(("arbitrary",)),
        name="hgrn",
    )(z, z, z, z, z, z, lbs_l.reshape(2, 1, HG_W), state_hgrn, state_hgrn, bd)


def _pool_kernel(u_ref, w_ref, sc_ref, o_ref, *, t_len):
    u = u_ref[...]
    row = lax.broadcasted_iota(jnp.int32, (t_len, POOL_W), 0)
    grp = lax.broadcasted_iota(jnp.int32, (t_len, POOL_W), 1) // HEAD_DIM
    half = jnp.left_shift(1, grp)
    acc = jnp.zeros_like(u)
    for j in range(-8, 8):
        src = row + j
        ok = (j >= -half) & (j < half) & (src >= 0) & (src < t_len)
        shifted = u if j == 0 else pltpu.roll(u, (-j) % t_len, 0)
        acc = acc + jnp.where(ok, shifted, 0.0)
    cnt = (jnp.minimum(row + half, t_len) - jnp.maximum(row - half, 0)).astype(F32)
    y = _dot((acc / cnt - u).astype(BF16), w_ref[...]) * sc_ref[...]
    o_ref[...] = y.astype(BF16)


def _pool(z, row0, nseq, t_len, w_bd, scale):
    return pl.pallas_call(
        functools.partial(_pool_kernel, t_len=t_len),
        grid=(nseq,),
        in_specs=[
            pl.BlockSpec((t_len, POOL_W), lambda b: (row0 // t_len + b, COL_CU // POOL_W)),
            pl.BlockSpec((POOL_W, POOL_W), lambda b: (0, 0)),
            pl.BlockSpec((1, POOL_W), lambda b: (0, 0)),
        ],
        out_specs=pl.BlockSpec((t_len, POOL_W), lambda b: (b, 0)),
        out_shape=jax.ShapeDtypeStruct((nseq * t_len, POOL_W), BF16),
        compiler_params=_params(("arbitrary",)),
        name="pool",
    )(z, w_bd, scale)


def _rope(x, cos, sin):
    w = x.shape[-1]
    lane = lax.broadcasted_iota(jnp.int32, x.shape, 1)
    up = pltpu.roll(x, w - 16, 1)
    dn = pltpu.roll(x, 16, 1)
    return x * cos + jnp.where((lane & 31) < 16, up, dn) * sin


def _prep_kernel(*refs, rope):
    if rope:
        (bq, bk, bv, dq, dk, dv, gq, gk, gdq, gdk, bd, cq, sq, ck, sk,
         oq, ok_, ov, odq, odk, odv) = refs
    else:
        (bq, bk, bv, dq, dk, dv, gq, gk, gdq, gdk, bd,
         oq, ok_, ov, odq, odk, odv, ok32, odk32, ov32, odv32) = refs
    bdm = bd[...]
    q = _head_rms(bq[...], bdm, gq[...])
    k = _head_rms(bk[...], bdm[:WK_B, :WK_B], gk[...])
    qd = _head_rms(dq[...], bdm[:WD, :WD], gdq[...])
    kd = _head_rms(dk[...], bdm[:WD, :WD], gdk[...])
    if rope:
        q = _rope(q, cq[...], sq[...])
        k = _rope(k, ck[...], sk[...])
    else:
        ok32[...] = k
        odk32[...] = kd
        ov32[...] = bv[...]
        odv32[...] = dv[...]
    oq[...] = (q * SCALE).astype(BF16)
    ok_[...] = k.astype(BF16)
    ov[...] = bv[...].astype(BF16)
    odq[...] = (qd * SCALE).astype(BF16)
    odk[...] = kd.astype(BF16)
    odv[...] = dv[...].astype(BF16)


def _prep(z, row0, nrows, gains, bd, rope_tabs):
    tm = 512
    nt = nrows // tm
    r0 = row0 // tm
    rope = rope_tabs is not None
    col = lambda c, w: (lambda i: (r0 + i, c // w))
    in_specs = [
        pl.BlockSpec((tm, WQ_B), col(COL_BQ, WQ_B)),
        pl.BlockSpec((tm, WK_B), col(COL_BK, WK_B)),
        pl.BlockSpec((tm, WK_B), col(COL_BV, WK_B)),
        pl.BlockSpec((tm, WD), col(COL_DQ, WD)),
        pl.BlockSpec((tm, WD), col(COL_DK, WD)),
        pl.BlockSpec((tm, WD), col(COL_DV, WD)),
        pl.BlockSpec((1, WQ_B), lambda i: (0, 0)),
        pl.BlockSpec((1, WK_B), lambda i: (0, 0)),
        pl.BlockSpec((1, WD), lambda i: (0, 0)),
        pl.BlockSpec((1, WD), lambda i: (0, 0)),
        pl.BlockSpec((WQ_B, WQ_B), lambda i: (0, 0)),
    ]
    args = [z, z, z, z, z, z, *gains, bd]
    per = DEC_SEQ // tm
    if rope:
        in_specs += [
            pl.BlockSpec((tm, WQ_B), lambda i: (i % per, 0)),
            pl.BlockSpec((tm, WQ_B), lambda i: (i % per, 0)),
            pl.BlockSpec((tm, WK_B), lambda i: (i % per, 0)),
            pl.BlockSpec((tm, WK_B), lambda i: (i % per, 0)),
        ]
        args += list(rope_tabs)
    widths = [WQ_B, WK_B, WK_B, WD, WD, WD]
    out_specs = [pl.BlockSpec((tm, w), lambda i: (i, 0)) for w in widths]
    out_shape = [jax.ShapeDtypeStruct((nrows, w), BF16) for w in widths]
    if not rope:
        out_specs += [pl.BlockSpec((tm, w), lambda i: (i, 0)) for w in (WK_B, WD, WK_B, WD)]
        out_shape += [jax.ShapeDtypeStruct((nrows, w), F32) for w in (WK_B, WD, WK_B, WD)]
    return pl.pallas_call(
        functools.partial(_prep_kernel, rope=rope),
        grid=(nt,),
        in_specs=in_specs,
        out_specs=out_specs,
        out_shape=out_shape,
        compiler_params=_params(("arbitrary",)),
        name="prep_lat" if rope else "prep_ctx",
    )(*args)


def _softmax_pv(scores, values, sink):
    m = scores[0].max(axis=-1, keepdims=True)
    for s in scores[1:]:
        m = jnp.maximum(m, s.max(axis=-1, keepdims=True))
    if sink is not None:
        m = jnp.maximum(m, sink)
    den = jnp.zeros_like(m) if sink is None else jnp.exp(sink - m)
    acc = None
    for s, v in zip(scores, values):
        p = jnp.exp(s - m)
        den = den + p.sum(axis=-1, keepdims=True)
        pv = _dot(p.astype(BF16), v)
        acc = pv if acc is None else acc + pv
    return acc / den


def _hs(x, h):
    return x[:, HEAD_DIM * h:HEAD_DIM * (h + 1)]


GQA = WIN_HEADS // WIN_KV


def _stack_group(q, sink_ref, kv, rows):
    qs = jnp.concatenate([_hs(q, GQA * kv + j) for j in range(GQA)], axis=0)
    part = lax.broadcasted_iota(jnp.int32, (GQA * rows, 1), 0) // rows
    sink = jnp.zeros((GQA * rows, 1), F32)
    for j in range(GQA):
        sink = jnp.where(part == j, sink_ref[0, GQA * kv + j], sink)
    return qs, sink


def _ctx_attn_kernel(sink_ref, q_ref, k_ref, v_ref, qd_ref, kd_ref, vd_ref, ob_ref, od_ref):
    q, k, v = q_ref[...], k_ref[...], v_ref[...]
    for kv in range(WIN_KV):
        qs, sink = _stack_group(q, sink_ref, kv, SEQ)
        s = _dot_nt(qs, _hs(k, kv))
        o = _softmax_pv([s], [_hs(v, kv)], sink)
        for j in range(GQA):
            h = GQA * kv + j
            ob_ref[:, HEAD_DIM * h:HEAD_DIM * (h + 1)] = o[SEQ * j:SEQ * (j + 1)].astype(BF16)
    qd, kd, vd = qd_ref[...], kd_ref[...], vd_ref[...]
    for h in range(NA_HEADS):
        s = _dot_nt(_hs(qd, h), _hs(kd, h))
        o = _softmax_pv([s], [_hs(vd, h)], None)
        od_ref[:, HEAD_DIM * h:HEAD_DIM * (h + 1)] = o.astype(BF16)


def _ctx_attn(sink, q, k, v, qd, kd, vd):
    blk = lambda w: pl.BlockSpec((SEQ, w), lambda b: (b, 0))
    return pl.pallas_call(
        _ctx_attn_kernel,
        grid=(BATCH,),
        in_specs=[pl.BlockSpec(memory_space=pltpu.SMEM),
                  blk(WQ_B), blk(WK_B), blk(WK_B), blk(WD), blk(WD), blk(WD)],
        out_specs=[blk(WQ_B), blk(WD)],
        out_shape=[jax.ShapeDtypeStruct((N_CTX, WQ_B), BF16), jax.ShapeDtypeStruct((N_CTX, WD), BF16)],
        compiler_params=_params(("arbitrary",)),
        name="ctx_attn",
    )(sink, q, k, v, qd, kd, vd)


WIN_SPAN = 3 * WIN


WIN_QPS = 2


def _win_attn_kernel(sink_ref, q_ref, k_ref, v_ref, kc_ref, vc_ref, o_ref):
    kc, vc = kc_ref[...], vc_ref[...]
    rows = GQA * WIN
    for u in range(WIN_QPS):
        qi = pl.program_id(1) * WIN_QPS + u
        qrows = slice(WIN * u, WIN * (u + 1))
        start = pl.multiple_of(jnp.clip(qi * WIN - WIN, 0, DEC_SEQ - WIN_SPAN), WIN)
        q = q_ref[qrows, :]
        kw = k_ref[pl.ds(start, WIN_SPAN), :]
        vw = v_ref[pl.ds(start, WIN_SPAN), :]
        qpos = qi * WIN + lax.broadcasted_iota(jnp.int32, (rows, WIN_SPAN), 0) % WIN
        kpos = start + lax.broadcasted_iota(jnp.int32, (rows, WIN_SPAN), 1)
        valid = jnp.abs(qpos - kpos) <= WIN
        for kv in range(WIN_KV):
            qs, sink = _stack_group(q, sink_ref, kv, WIN)
            s_loc = jnp.where(valid, _dot_nt(qs, _hs(kw, kv)), NEG)
            s_ctx = _dot_nt(qs, _hs(kc, kv))
            o = _softmax_pv([s_loc, s_ctx], [_hs(vw, kv), _hs(vc, kv)], sink)
            for j in range(GQA):
                h = GQA * kv + j
                o_ref[qrows, HEAD_DIM * h:HEAD_DIM * (h + 1)] = o[WIN * j:WIN * (j + 1)].astype(BF16)


def _win_attn(sink, q, k, v, kc, vc):
    nq = DEC_SEQ // (WIN * WIN_QPS)
    return pl.pallas_call(
        _win_attn_kernel,
        grid=(DEC_BATCH, nq),
        in_specs=[
            pl.BlockSpec(memory_space=pltpu.SMEM),
            pl.BlockSpec((WIN * WIN_QPS, WQ_B), lambda b, i: (b * nq + i, 0)),
            pl.BlockSpec((None, DEC_SEQ, WK_B), lambda b, i: (b, 0, 0)),
            pl.BlockSpec((None, DEC_SEQ, WK_B), lambda b, i: (b, 0, 0)),
            pl.BlockSpec((None, PAST_LEN, WK_B), lambda b, i: (b, 0, 0)),
            pl.BlockSpec((None, PAST_LEN, WK_B), lambda b, i: (b, 0, 0)),
        ],
        out_specs=pl.BlockSpec((WIN * WIN_QPS, WQ_B), lambda b, i: (b * nq + i, 0)),
        out_shape=jax.ShapeDtypeStruct((N_LAT, WQ_B), BF16),
        compiler_params=_params(("arbitrary", "arbitrary")),
        name="win_attn",
    )(sink, q, k.reshape(DEC_BATCH, DEC_SEQ, WK_B), v.reshape(DEC_BATCH, DEC_SEQ, WK_B), kc, vc)


NA_ROWS = DEC_SEQ // GRID_W
NA_G = 4
NA_NG = NA_ROWS // NA_G
NA_UROWS = NA_KH + NA_G - 1
NA_UKEYS = NA_UROWS * GRID_W
NA_QROWS = NA_G * GRID_W


def _na_union_start(g):
    return jnp.clip(g * NA_G - NA_KH // 2, 0, NA_ROWS - NA_UROWS)


NA_GPS = 2


def _natten_kernel(q_ref, k_ref, v_ref, kc_ref, vc_ref, bias_ref, o_ref):
    kc, vc = kc_ref[...], vc_ref[...]
    for u in range(NA_GPS):
        g = pl.program_id(1) * NA_GPS + u
        variant = jnp.where(g == 0, 0, jnp.where(g == NA_NG - 1, 2, 1))
        qrows = slice(NA_QROWS * u, NA_QROWS * (u + 1))
        start = pl.multiple_of(_na_union_start(g) * GRID_W, GRID_W)
        q = q_ref[qrows, :]
        kw = k_ref[pl.ds(start, NA_UKEYS), :]
        vw = v_ref[pl.ds(start, NA_UKEYS), :]
        for h in range(NA_HEADS):
            qh = _hs(q, h)
            s_loc = _dot_nt(qh, _hs(kw, h)) + bias_ref[variant, h]
            s_ctx = _dot_nt(qh, _hs(kc, h))
            o = _softmax_pv([s_loc, s_ctx], [_hs(vw, h), _hs(vc, h)], None)
            o_ref[qrows, HEAD_DIM * h:HEAD_DIM * (h + 1)] = o.astype(BF16)


def _natten(q, k, v, kc, vc, bias):
    nsteps = NA_NG // NA_GPS
    rows = NA_QROWS * NA_GPS
    return pl.pallas_call(
        _natten_kernel,
        grid=(DEC_BATCH, nsteps),
        in_specs=[
            pl.BlockSpec((rows, WD), lambda b, g: (b * nsteps + g, 0)),
            pl.BlockSpec((None, DEC_SEQ, WD), lambda b, g: (b, 0, 0)),
            pl.BlockSpec((None, DEC_SEQ, WD), lambda b, g: (b, 0, 0)),
            pl.BlockSpec((None, PAST_LEN, WD), lambda b, g: (b, 0, 0)),
            pl.BlockSpec((None, PAST_LEN, WD), lambda b, g: (b, 0, 0)),
            pl.BlockSpec(bias.shape, lambda b, g: (0, 0, 0, 0)),
        ],
        out_specs=pl.BlockSpec((rows, WD), lambda b, g: (b * nsteps + g, 0)),
        out_shape=jax.ShapeDtypeStruct((N_LAT, WD), BF16),
        compiler_params=_params(("arbitrary", "arbitrary")),
        name="natten",
    )(q, k.reshape(DEC_BATCH, DEC_SEQ, WD), v.reshape(DEC_BATCH, DEC_SEQ, WD), kc, vc, bias)


def _natten_bias(rpb):
    c = np.arange(GRID_W)
    cstart = np.clip(c - NA_KW // 2, 0, GRID_W - NA_KW)
    inwin = (c[None, :] >= cstart[:, None]) & (c[None, :] < cstart[:, None] + NA_KW)
    dc = np.clip(c[None, :] - c[:, None] + NA_KW - 1, 0, 2 * NA_KW - 2)
    pick = (dc[None] == np.arange(2 * NA_KW - 1)[:, None, None]).astype(np.float32)
    toep = jnp.einsum('hdj,jck->hdck', rpb.astype(F32), jnp.asarray(pick),
                      precision=lax.Precision.HIGHEST)
    toep = jnp.where(inwin[None, None], toep, NEG)
    masked = jnp.full((NA_HEADS, GRID_W, GRID_W), NEG, F32)
    variants = []
    for g in (0, 1, NA_NG - 1):
        u0 = int(np.clip(g * NA_G - NA_KH // 2, 0, NA_ROWS - NA_UROWS))
        rows = []
        for j in range(NA_G):
            r = g * NA_G + j
            w0 = int(np.clip(r - NA_KH // 2, 0, NA_ROWS - NA_KH))
            blocks = []
            for i in range(NA_UROWS):
                kr = u0 + i
                blocks.append(toep[:, kr - r + NA_KH - 1] if w0 <= kr < w0 + NA_KH else masked)
            rows.append(jnp.concatenate(blocks, axis=-1))
        variants.append(jnp.concatenate(rows, axis=-2))
    return jnp.stack(variants, axis=0)


def _merge_kernel(xc_ref, xl_ref, m_ref, g_ref, of_ref, ob_ref, ag_ref, on_ref, bd_ref,
                  ybc_ref, ybl_ref, ycc_ref, ycl_ref, ydc_ref, ydl_ref,
                  wg_ref, bg_ref, wa_ref, wb_ref, wc_ref, wd_ref, wo_ref,
                  g2_ref, wr_ref, br_ref, wsg_ref, wsu_ref, wsd_ref, tri_ref, ut_ref,
                  xa_ref, sl_ref, slt_ref, cnt_ref, xs_ref):
    x = _pick_group(TM, xc_ref, xl_ref)
    h = _prenorm(x, g_ref[...], m_ref[0:1, :], m_ref[1:2, :]).astype(BF16)
    o = of_ref[...] + ob_ref[...]
    ya = _head_rms(o, bd_ref[...], on_ref[...]) * _silu(ag_ref[...])
    yb = _pick_group(TM, ybc_ref, ybl_ref)
    yc = _pick_group(TM, ycc_ref, ycl_ref)
    yd = _pick_group(TM, ydc_ref, ydl_ref)
    d = D_MODEL
    merged = jnp.zeros((TM, d), F32)
    branches = ((ya.astype(BF16), wa_ref), (yb, wb_ref), (yc, wc_ref), (yd, wd_ref))
    for b, (y, w_ref) in enumerate(branches):
        gate = jax.nn.sigmoid(_dot(h, wg_ref[:, d * b:d * (b + 1)]) + bg_ref[:, d * b:d * (b + 1)])
        merged = merged + gate * _dot(y, w_ref[...])
    x1 = x + m_ref[2:3, :] * _dot(merged.astype(BF16), wo_ref[...])
    for t in range(TM // TR):
        _route_rows(x1[TR * t:TR * (t + 1)], t, m_ref, g2_ref, wr_ref, br_ref, wsg_ref, wsu_ref, wsd_ref,
                    tri_ref, ut_ref, xa_ref, sl_ref, slt_ref, cnt_ref, xs_ref)


def _merge(x_c, x_l, mods, norm_g, o_f, o_b, z, onorm, bd, yb_c, yb_l, yc_c, yc_l, yd_c, yd_l,
           w_gate, b_gate, wa, wb, wc, wd, wo, norm2_g, w_router, b_router, wsg, wsu, wsd, tri, ut):
    full = lambda a: pl.BlockSpec(a.shape, lambda i: (0, 0))
    tpm = TM // TR
    return pl.pallas_call(
        _merge_kernel,
        grid=(NTOK // TM,),
        in_specs=_two_group_specs(TM, D_MODEL) + [
            pl.BlockSpec((None, 6, D_MODEL), lambda i: (_seg_of_tile(i, TM), 0, 0)),
            full(norm_g),
            pl.BlockSpec((TM, HG_W), lambda i: (i, 0)),
            pl.BlockSpec((TM, HG_W), lambda i: (i, 0)),
            pl.BlockSpec((TM, HG_W), lambda i: (i, COL_AG // HG_W)),
            full(onorm), full(bd),
        ] + _two_group_specs(TM, WQ_B) + _two_group_specs(TM, POOL_W) + _two_group_specs(TM, WD) + [
            full(w_gate), full(b_gate), full(wa), full(wb), full(wc), full(wd), full(wo),
            full(norm2_g), full(w_router), full(b_router), full(wsg), full(wsu), full(wsd), full(tri), full(ut),
        ],
        out_specs=[
            pl.BlockSpec((TM, XW), lambda i: (i, 0)),
            pl.BlockSpec((TM, E_PAD), lambda i: (i, 0)),
            pl.BlockSpec((tpm, 8, TR), lambda i: (i, 0, 0)),
            pl.BlockSpec((tpm, 1, E_PAD), lambda i: (i, 0, 0)),
            pl.BlockSpec((TM, D_MODEL), lambda i: (i, 0)),
        ],
        out_shape=[
            jax.ShapeDtypeStruct((NTOK, XW), BF16),
            jax.ShapeDtypeStruct((NTOK, E_PAD), F32),
            jax.ShapeDtypeStruct((NT_R, 8, TR), F32),
            jax.ShapeDtypeStruct((NT_R, 1, E_PAD), F32),
            jax.ShapeDtypeStruct((NTOK, D_MODEL), F32),
        ],
        compiler_params=_params(("arbitrary",)),
        name="merge",
    )(x_c, x_l, mods, norm_g, o_f, o_b, z, onorm, bd, yb_c, yb_l, yc_c, yc_l, yd_c, yd_l,
      w_gate, b_gate, wa, wb, wc, wd, wo, norm2_g, w_router, b_router, wsg, wsu, wsd, tri, ut)


E_PAD = 128
TR = 256
RT = 512
SLAB = 16
NT_R = NTOK // TR
S_LOC = 1536
S_MAX = NTOK * TOP_K + NT_R * N_EXPERTS * (SLAB - 1) + N_EXPERTS * (RT - SLAB)
N_XT = S_MAX // RT
XW = D_MODEL + 2 * E_PAD
assert S_LOC >= TR * TOP_K + N_EXPERTS * (SLAB - 1) and S_LOC % 128 == 0 and S_MAX % RT == 0


def _route_rows(x, t, m_ref, g_ref, wr_ref, br_ref, wsg_ref, wsu_ref, wsd_ref, tri_ref, ut_ref,
                xa_ref, sl_ref, slt_ref, cnt_ref, xs_ref):
    rows = slice(TR * t, TR * (t + 1))
    lane = lax.broadcasted_iota(jnp.int32, (TR, E_PAD), 1)
    ms = jnp.mean(x * x, axis=-1, keepdims=True)
    y = x * lax.rsqrt(ms + EPS) * g_ref[...]
    h = y * (1.0 + m_ref[4:5, :]) + m_ref[3:4, :]
    h_hi = h.astype(BF16)
    h_lo = (h - h_hi.astype(F32)).astype(BF16)
    w = wr_ref[...]
    w_hi = w.astype(BF16)
    w_lo = (w - w_hi.astype(F32)).astype(BF16)
    logits = _dot(h_hi, w_hi) + _dot(h_hi, w_lo) + _dot(h_lo, w_hi)
    scores = jax.nn.sigmoid(logits)
    sel = jnp.where(lane < N_EXPERTS, scores + br_ref[...], -jnp.inf)
    picked = jnp.zeros((TR, E_PAD), F32)
    hot = jnp.zeros((TR, E_PAD), F32)
    idxs = []
    lane_f = lane.astype(F32)
    for _ in range(TOP_K):
        mx = sel.max(axis=-1, keepdims=True)
        idx = jnp.min(jnp.where(sel == mx, lane_f, float(E_PAD)), axis=-1, keepdims=True)
        hit = lane_f == idx
        picked = jnp.where(hit, scores, picked)
        hot = jnp.where(hit, 1.0, hot)
        sel = jnp.where(hit, -jnp.inf, sel)
        idxs.append(idx)
    wts = ROUTED_SCALE * picked / picked.sum(axis=-1, keepdims=True)

    cnt = hot.sum(axis=0, keepdims=True)
    pad = jnp.floor((cnt + (SLAB - 1.0)) * (1.0 / SLAB)) * SLAB
    loc = _dot(jnp.broadcast_to(pad, (8, E_PAD)).astype(BF16), ut_ref[...])[0:1, :]
    rank = _dot(tri_ref[...], hot.astype(BF16))
    slotmat = loc + rank
    sl = jnp.zeros((TR, E_PAD), F32)
    for k in range(TOP_K):
        s_k = jnp.sum(jnp.where(lane_f == idxs[k], slotmat, 0.0), axis=-1, keepdims=True)
        sl = jnp.where(lane == k, s_k, sl)
    sl_ref[rows, :] = sl
    slt_ref[t] = sl.T[0:8, :]
    cnt_ref[t] = pad

    w16 = wts.astype(BF16)
    xa_ref[rows, 0:D_MODEL] = h_hi
    xa_ref[rows, D_MODEL:D_MODEL + E_PAD] = w16
    xa_ref[rows, D_MODEL + E_PAD:XW] = (wts - w16.astype(F32)).astype(BF16)

    a = _silu(_dot(h_hi, wsg_ref[...])) * _dot(h_hi, wsu_ref[...])
    xs_ref[rows, :] = x + m_ref[5:6, :] * _dot(a.astype(BF16), wsd_ref[...])


def _slab_copy(src, src_row, dst, dst_row, sem):
    hint = lambda r: r if isinstance(r, int) else pl.multiple_of(r, SLAB)
    return pltpu.make_async_copy(src.at[pl.ds(hint(src_row), SLAB), :],
                                 dst.at[pl.ds(hint(dst_row), SLAB), :], sem)


NSL = S_LOC // SLAB


def _for_each_slab(grow_ref, ns_ref, t, fn):
    def per_slab(n, c):
        fn(n * SLAB, grow_ref[t * NSL + n])
        return c

    lax.fori_loop(0, ns_ref[t], per_slab, 0)


def _compact_kernel(grow_ref, ns_ref, nd_ref, toff_ref, tn_ref, nu_ref,
                    xa_ref, slt_ref, xs_hbm, xc_scr, zero_scr, sem):
    t = pl.program_id(0)
    slot = t % 2

    def wait_n(n, s):
        def body(_, c):
            _slab_copy(xc_scr.at[s], 0, xs_hbm, 0, sem.at[s]).wait()
            return c
        lax.fori_loop(0, n, body, 0)

    def unused_tile_copy(j):
        row = pl.multiple_of((nu_ref[0] + j) * RT, RT)
        return pltpu.make_async_copy(zero_scr, xs_hbm.at[pl.ds(row, RT), :], sem.at[2])

    @pl.when(t == 0)
    def _():
        zero_scr[...] = jnp.zeros_like(zero_scr)

        def body(j, c):
            unused_tile_copy(j).start()
            return c
        lax.fori_loop(0, N_XT - nu_ref[0], body, 0)

    @pl.when(t >= 2)
    def _():
        wait_n(nd_ref[t - 2], slot)

    row = lax.broadcasted_iota(jnp.int32, (S_LOC, TR), 0)
    slt = slt_ref[...].astype(jnp.int32)
    hit = row == slt[0:1, :]
    for k in range(1, TOP_K):
        hit = hit | (row == slt[k:k + 1, :])
    onehot = jnp.where(hit, 1.0, 0.0).astype(BF16)
    xc_scr[slot] = _dot(onehot, xa_ref[...]).astype(BF16)

    _for_each_slab(grow_ref, ns_ref, t,
                   lambda lr, gr: _slab_copy(xc_scr.at[slot], lr, xs_hbm, gr, sem.at[slot]).start())

    @pl.when(t < N_EXPERTS)
    def _():
        def body(s, c):
            _slab_copy(zero_scr, 0, xs_hbm, toff_ref[t] + s * SLAB, sem.at[slot]).start()
            return c
        lax.fori_loop(0, tn_ref[t], body, 0)

    @pl.when(t == NT_R - 1)
    def _():
        wait_n(nd_ref[t], slot)
        wait_n(nd_ref[t - 1], 1 - slot)

        def body(j, c):
            unused_tile_copy(j).wait()
            return c
        lax.fori_loop(0, N_XT - nu_ref[0], body, 0)


def _compact(meta, xa, slt):
    grid_spec = pltpu.PrefetchScalarGridSpec(
        num_scalar_prefetch=6,
        grid=(NT_R,),
        in_specs=[
            pl.BlockSpec((TR, XW), lambda i, *_: (i, 0)),
            pl.BlockSpec((None, 8, TR), lambda i, *_: (i, 0, 0)),
        ],
        out_specs=pl.BlockSpec(memory_space=pl.ANY),
        scratch_shapes=[pltpu.VMEM((2, S_LOC, XW), BF16), pltpu.VMEM((RT, XW), BF16),
                        pltpu.SemaphoreType.DMA((3,))],
    )
    return pl.pallas_call(
        _compact_kernel,
        grid_spec=grid_spec,
        out_shape=jax.ShapeDtypeStruct((S_MAX, XW), BF16),
        compiler_params=_params(("arbitrary",)),
        name="compact",
    )(meta['grow'], meta['ns'], meta['nd'], meta['toff'], meta['tn'], meta['nu'], xa, slt)


def _expert_kernel(te_ref, ti_ref, nu_ref, nx_ref, par_ref, xs_ref, wg_hbm, wu_hbm, wd_hbm, y_ref,
                   wg_f, wu_f, wd_f, wgu_s, wd_s, sem, *, layer):
    i = pl.program_id(0)
    e = te_ref[i]
    slot = par_ref[i]
    used = i < nu_ref[0]
    first = used & ((i == 0) | (e != te_ref[jnp.maximum(i - 1, 0)]))

    def weight_copies(expert, s):
        return (pltpu.make_async_copy(wg_hbm.at[layer, expert], wg_f.at[s], sem.at[s]),
                pltpu.make_async_copy(wu_hbm.at[layer, expert], wu_f.at[s], sem.at[s]),
                pltpu.make_async_copy(wd_hbm.at[layer, expert], wd_f.at[s], sem.at[s]))

    @pl.when(i == 0)
    def _():
        for c in weight_copies(e, slot):
            c.start()

    @pl.when(first)
    def _():
        for c in weight_copies(e, slot):
            c.wait()
        wgu_s[:, 0:D_EXPERT] = wg_f[slot].astype(BF16)
        wgu_s[:, D_EXPERT:2 * D_EXPERT] = wu_f[slot].astype(BF16)
        wd_s[...] = wd_f[slot].astype(BF16)

    @pl.when(first & (nx_ref[i] >= 0))
    def _():
        for c in weight_copies(nx_ref[i], 1 - slot):
            c.start()

    @pl.when(used)
    def _():
        x = xs_ref[:, 0:D_MODEL]
        gw = xs_ref[:, D_MODEL:D_MODEL + E_PAD].astype(F32) + xs_ref[:, D_MODEL + E_PAD:XW].astype(F32)
        lane = lax.broadcasted_iota(jnp.int32, (RT, E_PAD), 1)
        ge = jnp.sum(jnp.where(lane == e, gw, 0.0), axis=-1, keepdims=True)
        gu = _dot(x, wgu_s[...])
        a = _silu(gu[:, 0:D_EXPERT]) * gu[:, D_EXPERT:2 * D_EXPERT]
        y_ref[...] = _dot((a * ge).astype(BF16), wd_s[...]).astype(BF16)

    @pl.when(i >= nu_ref[0])
    def _():
        y_ref[...] = jnp.zeros_like(y_ref)


def _experts(meta, layer, xs, w_eg, w_eu, w_ed):
    any_spec = pl.BlockSpec(memory_space=pl.ANY)
    grid_spec = pltpu.PrefetchScalarGridSpec(
        num_scalar_prefetch=5,
        grid=(N_XT,),
        in_specs=[pl.BlockSpec((RT, XW), lambda i, te, ti, *_: (ti[i], 0)), any_spec, any_spec, any_spec],
        out_specs=pl.BlockSpec((RT, D_MODEL), lambda i, *_: (i, 0)),
        scratch_shapes=[pltpu.VMEM((2, D_MODEL, D_EXPERT), F32), pltpu.VMEM((2, D_MODEL, D_EXPERT), F32),
                        pltpu.VMEM((2, D_EXPERT, D_MODEL), F32),
                        pltpu.VMEM((D_MODEL, 2 * D_EXPERT), BF16), pltpu.VMEM((D_EXPERT, D_MODEL), BF16),
                        pltpu.SemaphoreType.DMA((2,))],
    )
    return pl.pallas_call(
        functools.partial(_expert_kernel, layer=layer),
        grid_spec=grid_spec,
        out_shape=jax.ShapeDtypeStruct((S_MAX, D_MODEL), BF16),
        compiler_params=_params(("arbitrary",)),
        name="experts",
    )(meta['te'], meta['ti'], meta['nu'], meta['nx'], meta['par'], xs, w_eg, w_eu, w_ed)


def _combine_kernel(grow_ref, ns_ref, xs_ref, m_ref, sl_ref, y_hbm, oc_ref, ol_ref, yc_scr, sem):
    t = pl.program_id(0)
    slot = t % 2

    def issue(tt, s):
        _for_each_slab(grow_ref, ns_ref, tt,
                       lambda lr, gr: _slab_copy(y_hbm, gr, yc_scr.at[s], lr, sem.at[s]).start())

    @pl.when(t == 0)
    def _():
        yc_scr[...] = jnp.zeros_like(yc_scr)
        issue(0, 0)

    @pl.when(t + 1 < NT_R)
    def _():
        issue(t + 1, 1 - slot)

    def wait_body(_, c):
        _slab_copy(y_hbm, 0, yc_scr.at[slot], 0, sem.at[slot]).wait()
        return c
    lax.fori_loop(0, ns_ref[t], wait_body, 0)

    col = lax.broadcasted_iota(jnp.int32, (TR, S_LOC), 1)
    sl = sl_ref[...].astype(jnp.int32)
    hit = col == sl[:, 0:1]
    for k in range(1, TOP_K):
        hit = hit | (col == sl[:, k:k + 1])
    onehot = jnp.where(hit, 1.0, 0.0).astype(BF16)
    out = xs_ref[...] + m_ref[5:6, :] * _dot(onehot, yc_scr[slot])

    @pl.when(t < N_CTX // TR)
    def _():
        oc_ref[...] = out

    @pl.when(t >= N_CTX // TR)
    def _():
        ol_ref[...] = out


def _combine(meta, xsh, mods, sl, y):
    nct = N_CTX // TR
    out_specs = [pl.BlockSpec((TR, D_MODEL), lambda i, *_: (jnp.minimum(i, nct - 1), 0)),
                 pl.BlockSpec((TR, D_MODEL), lambda i, *_: (jnp.maximum(i - nct, 0), 0))]
    out_shape = [jax.ShapeDtypeStruct((N_CTX, D_MODEL), F32), jax.ShapeDtypeStruct((N_LAT, D_MODEL), F32)]
    grid_spec = pltpu.PrefetchScalarGridSpec(
        num_scalar_prefetch=2,
        grid=(NT_R,),
        in_specs=[
            pl.BlockSpec((TR, D_MODEL), lambda i, *_: (i, 0)),
            pl.BlockSpec((None, 6, D_MODEL), lambda i, *_: (_seg_of_tile(i, TR), 0, 0)),
            pl.BlockSpec((TR, E_PAD), lambda i, *_: (i, 0)),
            pl.BlockSpec(memory_space=pl.ANY),
        ],
        out_specs=out_specs,
        scratch_shapes=[pltpu.VMEM((2, S_LOC, D_MODEL), BF16), pltpu.SemaphoreType.DMA((2,))],
    )
    return pl.pallas_call(
        _combine_kernel,
        grid_spec=grid_spec,
        out_shape=out_shape,
        compiler_params=_params(("arbitrary",)),
        name="combine",
    )(meta['grow'], meta['ns'], xsh, mods, sl, y)


def _route_meta(cnt):
    pc = cnt[:, 0, :N_EXPERTS].astype(jnp.int32)
    tot = pc.sum(axis=0)
    tot_pad = ((tot + RT - 1) // RT) * RT
    ends = jnp.cumsum(tot_pad)
    base = ends - tot_pad
    dst = base[None, :] + jnp.cumsum(pc, axis=0) - pc
    cum = jnp.cumsum(pc, axis=1)
    ns = cum[:, -1] // SLAB
    lrow = jnp.arange(NSL, dtype=jnp.int32)[None, :, None] * SLAB
    owner = jnp.minimum(jnp.sum(cum[:, None, :] <= lrow, axis=2), N_EXPERTS - 1)
    mine = owner[:, :, None] == jnp.arange(N_EXPERTS, dtype=jnp.int32)[None, None, :]
    grow = jnp.sum(jnp.where(mine, (dst - cum + pc)[:, None, :], 0), axis=2) + lrow[:, :, 0]
    tn = (tot_pad - tot) // SLAB
    nd = ns + jnp.pad(tn, (0, NT_R - N_EXPERTS))
    n_used = ends[-1] // RT
    ti = jnp.minimum(jnp.arange(N_XT, dtype=jnp.int32), n_used - 1)
    te = jnp.minimum(jnp.sum(ends[None, :] <= (ti * RT)[:, None], axis=1), N_EXPERTS - 1)
    eidx = jnp.arange(N_EXPERTS, dtype=jnp.int32)
    has = tot_pad > 0
    later = has[None, :] & (eidx[None, :] > eidx[:, None])
    nxt_e = jnp.min(jnp.where(later, eidx[None, :], N_EXPERTS), axis=1)
    nxt_e = jnp.where(nxt_e == N_EXPERTS, -1, nxt_e)
    ordinal = jnp.cumsum(has.astype(jnp.int32)) - 1
    is_e = te[:, None] == eidx[None, :]
    nx = jnp.sum(jnp.where(is_e, nxt_e[None, :], 0), axis=1)
    par = jnp.sum(jnp.where(is_e, ordinal[None, :], 0), axis=1) & 1
    i32 = lambda a: a.astype(jnp.int32)
    return dict(grow=i32(grow.reshape(-1)), ns=i32(ns), nd=i32(nd), toff=i32(base + tot), tn=i32(tn),
                te=i32(te), ti=i32(ti), nu=i32(n_used.reshape(1)), nx=i32(nx), par=i32(par))


def _moe_routed(layer, routed, mods, w_eg, w_eu, w_ed):
    xa, sl, slt, cnt, xsh = routed
    meta = _route_meta(cnt)
    xs = _compact(meta, xa, slt)
    y = _experts(meta, layer, xs, w_eg, w_eu, w_ed)
    return _combine(meta, xsh, mods, sl, y)


def _rope_tables(width):
    t = np.arange(DEC_SEQ)
    quarter = HEAD_DIM // 4
    inv = (ROPE_BASE ** (-np.arange(quarter) / quarter)).astype(np.float32)
    ang_r = (t // GRID_W).astype(np.float32)[:, None] * inv[None]
    ang_c = (t % GRID_W).astype(np.float32)[:, None] * inv[None]
    cos = np.concatenate([np.cos(ang_r), np.cos(ang_r), np.cos(ang_c), np.cos(ang_c)], axis=1)
    sin = np.concatenate([-np.sin(ang_r), np.sin(ang_r), -np.sin(ang_c), np.sin(ang_c)], axis=1)
    reps = width // HEAD_DIM
    return (jnp.asarray(np.tile(cos, (1, reps)), F32), jnp.asarray(np.tile(sin, (1, reps)), F32))


def _permute_w_in(w):
    a = w[:, 0:1280]
    bq, bk, bv = w[:, 1280:1664], w[:, 1664:1792], w[:, 1792:1920]
    cu = w[:, 1920:2176]
    d = w[:, 2176:2944]
    pad = jnp.zeros((w.shape[0], Z_W - 2944), w.dtype)
    return jnp.concatenate([a, cu, d, bq, bk, bv, pad], axis=1)


def _block_diag(blocks):
    g = blocks.shape[0]
    eye = jnp.eye(g, dtype=blocks.dtype)
    return jnp.einsum('gh,gij->gihj', eye, blocks).reshape(g * HEAD_DIM, g * HEAD_DIM)


def kernel(x_prompt, x_sample, cache_win_k, cache_win_v, cache_na_k, cache_na_v, state_hgrn, c, c_ctx, w_mod, b_mod, norm1_g, norm2_g, w_in, w_mgate, b_mgate, hg_lb, hg_onorm, win_qn, win_kn, win_sink, pool_w, pool_scale, na_qn, na_kn, na_rpb, w_branch, w_out, w_router, b_router, w_eg, w_eu, w_ed, w_sg, w_su, w_sd):
    lbp = jax.nn.softmax(hg_lb.astype(F32), axis=0)
    lbs = jnp.cumsum(lbp, axis=0) - lbp[0:1]

    cvec8 = jnp.concatenate([c_ctx[None], c, jnp.zeros((3, D_MODEL), F32)], axis=0)
    mods_all = _modulation(cvec8, w_mod, b_mod).reshape(DEPTH, 8, 6, D_MODEL)

    bd384 = jnp.asarray(_bd_ones(WQ_B), BF16)
    bd256 = bd384[:HG_W, :HG_W]
    bd256_f32 = jnp.asarray(_bd_ones(HG_W), F32)
    rope_q = _rope_tables(WQ_B)
    rope_k = _rope_tables(WK_B)
    tile = lambda g, reps: jnp.tile(g, reps)[None, :]
    tri = jnp.asarray(np.tril(np.ones((TR, TR), np.float32), -1), BF16)
    ut = jnp.asarray(np.triu(np.ones((E_PAD, E_PAD), np.float32), 1), BF16)

    x_c, x_l = x_prompt.reshape(N_CTX, D_MODEL), x_sample.reshape(N_LAT, D_MODEL)
    new_k, new_v, new_kd, new_vd, new_s = [], [], [], [], []
    for l in range(DEPTH):
        mods = mods_all[l]
        z = _projection(x_c, x_l, mods, norm1_g[l][None], _permute_w_in(w_in[l]).astype(BF16))

        o_f, o_b, sfin_f, sfin_b = _hgrn(z, lbs[l], state_hgrn.astype(F32), l, bd256_f32)
        new_s.append(jnp.stack([sfin_f, sfin_b], axis=1))

        w_pool = _block_diag(pool_w[l]).astype(BF16)
        yc_c = _pool(z, 0, BATCH, SEQ, w_pool, pool_scale[l][None])
        yc_l = _pool(z, N_CTX, DEC_BATCH, DEC_SEQ, w_pool, pool_scale[l][None])

        gains = (tile(win_qn[l], WIN_HEADS), tile(win_kn[l], WIN_KV), tile(na_qn[l], NA_HEADS), tile(na_kn[l], NA_HEADS))
        qb_c, kb_c, vb_c, qd_c, kd_c, vd_c, kb32, kd32, vb32, vd32 = _prep(z, 0, N_CTX, gains, bd384, None)
        qb_l, kb_l, vb_l, qd_l, kd_l, vd_l = _prep(z, N_CTX, N_LAT, gains, bd384, rope_q + rope_k)
        sink = win_sink[l][None]
        yb_c, yd_c = _ctx_attn(sink, qb_c, kb_c, vb_c, qd_c, kd_c, vd_c)
        kc = cache_win_k[:, l].reshape(DEC_BATCH, PAST_LEN, WK_B).astype(BF16)
        vc = cache_win_v[:, l].reshape(DEC_BATCH, PAST_LEN, WK_B).astype(BF16)
        yb_l = _win_attn(sink, qb_l, kb_l, vb_l, kc, vc)
        kcd = cache_na_k[:, l].reshape(DEC_BATCH, PAST_LEN, WD).astype(BF16)
        vcd = cache_na_v[:, l].reshape(DEC_BATCH, PAST_LEN, WD).astype(BF16)
        yd_l = _natten(qd_l, kd_l, vd_l, kcd, vcd, _natten_bias(na_rpb[l]))

        new_k.append(kb32.reshape(BATCH, SEQ, WIN_KV, HEAD_DIM))
        new_v.append(vb32.reshape(BATCH, SEQ, WIN_KV, HEAD_DIM))
        new_kd.append(kd32.reshape(BATCH, SEQ, NA_HEADS, HEAD_DIM))
        new_vd.append(vd32.reshape(BATCH, SEQ, NA_HEADS, HEAD_DIM))

        wbr = w_branch[l].astype(BF16)
        wr = jnp.pad(w_router[l], ((0, 0), (0, E_PAD - N_EXPERTS)))
        br = jnp.pad(b_router[l], (0, E_PAD - N_EXPERTS))[None]
        routed = _merge(x_c, x_l, mods, norm1_g[l][None], o_f, o_b, z, tile(hg_onorm[l], HG_HEADS), bd256,
                        yb_c, yb_l, yc_c, yc_l, yd_c, yd_l, w_mgate[l].astype(BF16), b_mgate[l][None],
                        wbr[0:256], wbr[256:640], wbr[640:896], wbr[896:1152], w_out[l].astype(BF16),
                        norm2_g[l][None], wr, br, w_sg[l].astype(BF16), w_su[l].astype(BF16),
                        w_sd[l].astype(BF16), tri, ut)
        x_c, x_l = _moe_routed(l, routed, mods, w_eg, w_eu, w_ed)

    y_p = x_c.reshape(BATCH, SEQ, D_MODEL)
    y_s = x_l.reshape(DEC_BATCH, DEC_SEQ, D_MODEL)
    return (y_p, y_s, jnp.stack(new_k, axis=1), jnp.stack(new_v, axis=1), jnp.stack(new_kd, axis=1),
            jnp.stack(new_vd, axis=1), jnp.stack(new_s, axis=1))
```

```python
import functools

import numpy as np
import jax
import jax.numpy as jnp
from jax import lax
from jax.experimental import pallas as pl
from jax.experimental.pallas import tpu as pltpu

F32 = jnp.float32
BF16 = jnp.bfloat16

D_MODEL = 1024
BATCH = 16
SEQ = 256
DEPTH = 2
DEC_BATCH = 4
DEC_SEQ = 2048
PAST_LEN = 256
NEG = -1e30
GRID_W = 64
HEAD_DIM = 64
SCALE = HEAD_DIM ** -0.5
ROPE_BASE = 10000.0
EPS = 1e-6
HG_HEADS = 4
HG_W = 256
WIN_HEADS = 6
WIN_KV = 2
WIN = 128
POOL_SIZES = (2, 4, 8, 16)
POOL_W = 256
NA_HEADS = 4
NA_KH = 8
NA_KW = 16
N_EXPERTS = 32
TOP_K = 4
D_EXPERT = 256
ROUTED_SCALE = 2.5

N_CTX = BATCH * SEQ
N_LAT = DEC_BATCH * DEC_SEQ
NTOK = N_CTX + N_LAT
WQ_B = WIN_HEADS * HEAD_DIM
WK_B = WIN_KV * HEAD_DIM
WD = NA_HEADS * HEAD_DIM

Z_W = 3072
COL_AQ, COL_AFF, COL_AFB, COL_AI, COL_AG, COL_CU, COL_DQ, COL_DK, COL_DV = (
    0, 256, 512, 768, 1024, 1280, 1536, 1792, 2048)
COL_BQ, COL_BK, COL_BV = 2304, 2688, 2816

TM = 512
TMP = 2048
TN = 512
HB = 256
VMEM_LIMIT = 56 * 1024 * 1024


def _params(sem, vmem=VMEM_LIMIT):
    return pltpu.CompilerParams(dimension_semantics=sem, vmem_limit_bytes=vmem)


def _seg_of_tile(i, tile):
    nct = N_CTX // tile
    per = DEC_SEQ // tile
    return jnp.where(i < nct, 0, 1 + (i - nct) // per)


def _bd_ones(w):
    idx = np.arange(w) // HEAD_DIM
    return (idx[:, None] == idx[None, :]).astype(np.float32)


def _dot(a, b):
    return jnp.dot(a, b, preferred_element_type=F32)


def _dot_nt(a, b):
    return lax.dot_general(a, b, (((1,), (1,)), ((), ())), preferred_element_type=F32)


def _split_dot(x, w_bf16):
    hi = x.astype(BF16)
    lo = (x - hi.astype(F32)).astype(BF16)
    return _dot(hi, w_bf16) + _dot(lo, w_bf16)


def _head_rms(x, bd, gain):
    ms = _split_dot(x * x, bd) * (1.0 / HEAD_DIM)
    return x * lax.rsqrt(ms + EPS) * gain


def _silu(x):
    return x * jax.nn.sigmoid(x)


def _mod_kernel(c_ref, w_ref, b_ref, o_ref):
    c = c_ref[...]
    a = _silu(c).astype(BF16)
    o_ref[...] = _dot(a, w_ref[...].astype(BF16)) + b_ref[...]


def _modulation(cvec8, w_mod, b_mod):
    n = 6 * D_MODEL
    tn = 1536
    return pl.pallas_call(
        _mod_kernel,
        grid=(DEPTH, n // tn),
        in_specs=[
            pl.BlockSpec((8, D_MODEL), lambda l, j: (0, 0)),
            pl.BlockSpec((None, D_MODEL, tn), lambda l, j: (l, 0, j)),
            pl.BlockSpec((None, 1, tn), lambda l, j: (l, 0, j)),
        ],
        out_specs=pl.BlockSpec((None, 8, tn), lambda l, j: (l, 0, j)),
        out_shape=jax.ShapeDtypeStruct((DEPTH, 8, n), F32),
        compiler_params=_params(("arbitrary", "arbitrary")),
        name="modulation",
    )(cvec8, w_mod, b_mod.reshape(DEPTH, 1, n))


def _prenorm(x, gain, shift, scale):
    ms = jnp.mean(x * x, axis=-1, keepdims=True)
    return x * lax.rsqrt(ms + EPS) * gain * (1.0 + scale) + shift


def _two_group_specs(tile, width, nargs=1):
    nct = N_CTX // tile
    if nargs == 1:
        return [pl.BlockSpec((tile, width), lambda i: (jnp.minimum(i, nct - 1), 0)),
                pl.BlockSpec((tile, width), lambda i: (jnp.maximum(i - nct, 0), 0))]
    return [pl.BlockSpec((tile, width), lambda i, j: (jnp.minimum(i, nct - 1), 0)),
            pl.BlockSpec((tile, width), lambda i, j: (jnp.maximum(i - nct, 0), 0))]


def _pick_group(tile, c_ref, l_ref):
    return jnp.where(pl.program_id(0) < N_CTX // tile, c_ref[...], l_ref[...])


def _proj_kernel(xc_ref, xl_ref, m_ref, g_ref, win_ref, z_ref, h_scr):
    @pl.when(pl.program_id(1) == 0)
    def _():
        x = _pick_group(TMP, xc_ref, xl_ref)
        h_scr[...] = _prenorm(x, g_ref[...], m_ref[0:1, :], m_ref[1:2, :]).astype(BF16)

    z_ref[...] = _dot(h_scr[...], win_ref[...])


def _projection(x_c, x_l, mods, norm_g, w_in_p):
    return pl.pallas_call(
        _proj_kernel,
        grid=(NTOK // TMP, Z_W // TN),
        in_specs=_two_group_specs(TMP, D_MODEL, nargs=2) + [
            pl.BlockSpec((None, 6, D_MODEL), lambda i, j: (_seg_of_tile(i, TMP), 0, 0)),
            pl.BlockSpec((1, D_MODEL), lambda i, j: (0, 0)),
            pl.BlockSpec((D_MODEL, TN), lambda i, j: (0, j)),
        ],
        out_specs=pl.BlockSpec((TMP, TN), lambda i, j: (i, j)),
        out_shape=jax.ShapeDtypeStruct((NTOK, Z_W), F32),
        scratch_shapes=[pltpu.VMEM((TMP, D_MODEL), BF16)],
        compiler_params=_params(("arbitrary", "arbitrary")),
        name="projection",
    )(x_c, x_l, mods, norm_g, w_in_p)


def _hgrn_kernel(qf_ref, ff_ref, vf_ref, qb_ref, fb_ref, vb_ref, lb_ref, s0f_ref, s0b_ref, bd_ref,
                 of_ref, ob_ref, sff_ref, sfb_ref, sf_scr, sb_scr):
    _hgrn_direction(False, qf_ref, ff_ref, vf_ref, lb_ref[0], s0f_ref, bd_ref, of_ref, sff_ref, sf_scr)
    _hgrn_direction(True, qb_ref, fb_ref, vb_ref, lb_ref[1], s0b_ref, bd_ref, ob_ref, sfb_ref, sb_scr)


HSB = 128


def _hgrn_subblock(rev, q, zf, v, lb, s_t, bd):
    n = HSB
    f = lb + (1.0 - lb) * jax.nn.sigmoid(zf)
    lf = jnp.log2(f)
    kk = 1.0 - f

    row = lax.broadcasted_iota(jnp.int32, (n, HG_W), 0)
    tq = lax.broadcasted_iota(jnp.int32, (n, n), 0)
    tk = lax.broadcasted_iota(jnp.int32, (n, n), 1)

    def before(x, m):
        return pltpu.roll(x, (n - m) if rev else m, 0)

    def after(x, m):
        return pltpu.roll(x, m if rev else (n - m), 0)

    q16 = q.astype(BF16)
    k16 = kk.astype(BF16)
    att = [jnp.where(tq == tk, _dot_nt(_hs(q16, h), _hs(k16, h)), 0.0) for h in range(HG_HEADS)]

    tot = lf
    pin = lf
    sex = jnp.zeros_like(lf)
    m = 1
    while m < n:
        late = ((row & (2 * m - 1)) < m) if rev else ((row & (2 * m - 1)) >= m)
        qm = jnp.where(late, q * jnp.exp2(pin), 0.0).astype(BF16)
        km = jnp.where(late, 0.0, kk * jnp.exp2(sex)).astype(BF16)
        shift = (2 * m).bit_length() - 1
        same = (tq >> shift) == (tk >> shift)
        for h in range(HG_HEADS):
            sc = _dot_nt(_hs(qm, h), _hs(km, h))
            att[h] = att[h] + (sc if 2 * m == n else jnp.where(same, sc, 0.0))
        tb = before(tot, m)
        ta = after(tot, m)
        pin = pin + jnp.where(late, tb, 0.0)
        sex = sex + jnp.where(late, 0.0, ta)
        tot = tot + jnp.where(late, tb, ta)
        m *= 2

    lane_head = lax.broadcasted_iota(jnp.int32, (n, HG_W), 1) // HEAD_DIM
    o = _dot_nt((q * jnp.exp2(pin)).astype(BF16), s_t.astype(BF16))
    for h in range(HG_HEADS):
        vh = jnp.where(lane_head == h, v, 0.0).astype(BF16)
        o = o + _dot(att[h].astype(BF16), vh)

    kt = (kk * jnp.exp2(sex)).astype(BF16)
    dec = jnp.exp2(tot[0:1, :])
    s_new = s_t * dec + _dot(v.T.astype(BF16), kt) * bd
    return o, s_new


def _hgrn_direction(rev, q_ref, f_ref, v_ref, lb, s0_ref, bd_ref, o_ref, sfin_ref, s_scr):
    i = pl.program_id(0)
    blk = (pl.num_programs(0) - 1 - i) if rev else i
    nct = N_CTX // HB
    per_c = SEQ // HB
    per_l = DEC_SEQ // HB
    is_ctx = blk < nct
    pos = jnp.where(is_ctx, blk % per_c, (blk - nct) % per_l)
    last = jnp.where(is_ctx, per_c - 1, per_l - 1)
    first_pos = last if rev else 0
    final_pos = 0 if rev else last

    @pl.when(pos == first_pos)
    def _():
        s_scr[...] = jnp.zeros_like(s_scr)

    @pl.when((pos == first_pos) & jnp.logical_not(is_ctx))
    def _():
        for h in range(HG_HEADS):
            hs = slice(HEAD_DIM * h, HEAD_DIM * (h + 1))
            s_scr[hs, hs] = s0_ref[h].T

    bd = bd_ref[...]
    s_new = s_scr[...]
    subs = range(HB // HSB)
    for sb in (reversed(subs) if rev else subs):
        rs = slice(HSB * sb, HSB * (sb + 1))
        o, s_new = _hgrn_subblock(rev, q_ref[rs, :], f_ref[rs, :], v_ref[rs, :], lb, s_new, bd)
        o_ref[rs, :] = o
    s_scr[...] = s_new

    @pl.when((pos == final_pos) & is_ctx)
    def _():
        for h in range(HG_HEADS):
            hs = slice(HEAD_DIM * h, HEAD_DIM * (h + 1))
            sfin_ref[h] = s_new[hs, hs].T


def _hgrn(z, lbs_l, state_hgrn, layer, bd):
    nb = NTOK // HB
    nct = N_CTX // HB
    rblk = lambda i: nb - 1 - i

    def seq_of(b):
        return jnp.where(b < nct, b // (SEQ // HB), BATCH + (b - nct) // (DEC_SEQ // HB))

    col = lambda blk, c: pl.BlockSpec((HB, HG_W), lambda i: (blk(i), c // HG_W))
    state = lambda blk, d: pl.BlockSpec(
        (None, None, None, HG_HEADS, HEAD_DIM, HEAD_DIM),
        lambda i: (jnp.maximum(seq_of(blk(i)) - BATCH, 0), layer, d, 0, 0, 0))
    final = lambda blk: pl.BlockSpec((None, HG_HEADS, HEAD_DIM, HEAD_DIM),
                                     lambda i: (jnp.minimum(seq_of(blk(i)), BATCH - 1), 0, 0, 0))
    fwd = lambda i: i
    return pl.pallas_call(
        _hgrn_kernel,
        grid=(nb,),
        in_specs=[
            col(fwd, COL_AQ), col(fwd, COL_AFF), col(fwd, COL_AI),
            col(rblk, COL_AQ), col(rblk, COL_AFB), col(rblk, COL_AI),
            pl.BlockSpec((2, 1, HG_W), lambda i: (0, 0, 0)),
            state(fwd, 0), state(rblk, 1),
            pl.BlockSpec((HG_W, HG_W), lambda i: (0, 0)),
        ],
        out_specs=[col(fwd, 0), col(rblk, 0), final(fwd), final(rblk)],
        out_shape=[
            jax.ShapeDtypeStruct((NTOK, HG_W), F32),
            jax.ShapeDtypeStruct((NTOK, HG_W), F32),
            jax.ShapeDtypeStruct((BATCH, HG_HEADS, HEAD_DIM, HEAD_DIM), F32),
            jax.ShapeDtypeStruct((BATCH, HG_HEADS, HEAD_DIM, HEAD_DIM), F32),
        ],
        scratch_shapes=[pltpu.VMEM((HG_W, HG_W), F32), pltpu.VMEM((HG_W, HG_W), F32)],
        compiler_params=_params(("arbitrary",)),
        name="hgrn",
    )(z, z, z, z, z, z, lbs_l.reshape(2, 1, HG_W), state_hgrn, state_hgrn, bd)


def _pool_kernel(u_ref, w_ref, sc_ref, o_ref, *, t_len):
    u = u_ref[...]
    row = lax.broadcasted_iota(jnp.int32, (t_len, POOL_W), 0)
    grp = lax.broadcasted_iota(jnp.int32, (t_len, POOL_W), 1) // HEAD_DIM
    half = jnp.left_shift(1, grp)
    acc = jnp.zeros_like(u)
    for j in range(-8, 8):
        src = row + j
        ok = (j >= -half) & (j < half) & (src >= 0) & (src < t_len)
        shifted = u if j == 0 else pltpu.roll(u, (-j) % t_len, 0)
        acc = acc + jnp.where(ok, shifted, 0.0)
    cnt = (jnp.minimum(row + half, t_len) - jnp.maximum(row - half, 0)).astype(F32)
    y = _dot((acc / cnt - u).astype(BF16), w_ref[...]) * sc_ref[...]
    o_ref[...] = y.astype(BF16)


def _pool(z, row0, nseq, t_len, w_bd, scale):
    return pl.pallas_call(
        functools.partial(_pool_kernel, t_len=t_len),
        grid=(nseq,),
        in_specs=[
            pl.BlockSpec((t_len, POOL_W), lambda b: (row0 // t_len + b, COL_CU // POOL_W)),
            pl.BlockSpec((POOL_W, POOL_W), lambda b: (0, 0)),
            pl.BlockSpec((1, POOL_W), lambda b: (0, 0)),
        ],
        out_specs=pl.BlockSpec((t_len, POOL_W), lambda b: (b, 0)),
        out_shape=jax.ShapeDtypeStruct((nseq * t_len, POOL_W), BF16),
        compiler_params=_params(("arbitrary",)),
        name="pool",
    )(z, w_bd, scale)


def _rope(x, cos, sin):
    w = x.shape[-1]
    lane = lax.broadcasted_iota(jnp.int32, x.shape, 1)
    up = pltpu.roll(x, w - 16, 1)
    dn = pltpu.roll(x, 16, 1)
    return x * cos + jnp.where((lane & 31) < 16, up, dn) * sin


def _prep_kernel(*refs, rope):
    if rope:
        (bq, bk, bv, dq, dk, dv, gq, gk, gdq, gdk, bd, cq, sq, ck, sk,
         oq, ok_, ov, odq, odk, odv) = refs
    else:
        (bq, bk, bv, dq, dk, dv, gq, gk, gdq, gdk, bd,
         oq, ok_, ov, odq, odk, odv, ok32, odk32, ov32, odv32) = refs
    bdm = bd[...]
    q = _head_rms(bq[...], bdm, gq[...])
    k = _head_rms(bk[...], bdm[:WK_B, :WK_B], gk[...])
    qd = _head_rms(dq[...], bdm[:WD, :WD], gdq[...])
    kd = _head_rms(dk[...], bdm[:WD, :WD], gdk[...])
    if rope:
        q = _rope(q, cq[...], sq[...])
        k = _rope(k, ck[...], sk[...])
    else:
        ok32[...] = k
        odk32[...] = kd
        ov32[...] = bv[...]
        odv32[...] = dv[...]
    oq[...] = (q * SCALE).astype(BF16)
    ok_[...] = k.astype(BF16)
    ov[...] = bv[...].astype(BF16)
    odq[...] = (qd * SCALE).astype(BF16)
    odk[...] = kd.astype(BF16)
    odv[...] = dv[...].astype(BF16)


def _prep(z, row0, nrows, gains, bd, rope_tabs):
    tm = 512
    nt = nrows // tm
    r0 = row0 // tm
    rope = rope_tabs is not None
    col = lambda c, w: (lambda i: (r0 + i, c // w))
    in_specs = [
        pl.BlockSpec((tm, WQ_B), col(COL_BQ, WQ_B)),
        pl.BlockSpec((tm, WK_B), col(COL_BK, WK_B)),
        pl.BlockSpec((tm, WK_B), col(COL_BV, WK_B)),
        pl.BlockSpec((tm, WD), col(COL_DQ, WD)),
        pl.BlockSpec((tm, WD), col(COL_DK, WD)),
        pl.BlockSpec((tm, WD), col(COL_DV, WD)),
        pl.BlockSpec((1, WQ_B), lambda i: (0, 0)),
        pl.BlockSpec((1, WK_B), lambda i: (0, 0)),
        pl.BlockSpec((1, WD), lambda i: (0, 0)),
        pl.BlockSpec((1, WD), lambda i: (0, 0)),
        pl.BlockSpec((WQ_B, WQ_B), lambda i: (0, 0)),
    ]
    args = [z, z, z, z, z, z, *gains, bd]
    per = DEC_SEQ // tm
    if rope:
        in_specs += [
            pl.BlockSpec((tm, WQ_B), lambda i: (i % per, 0)),
            pl.BlockSpec((tm, WQ_B), lambda i: (i % per, 0)),
            pl.BlockSpec((tm, WK_B), lambda i: (i % per, 0)),
            pl.BlockSpec((tm, WK_B), lambda i: (i % per, 0)),
        ]
        args += list(rope_tabs)
    widths = [WQ_B, WK_B, WK_B, WD, WD, WD]
    out_specs = [pl.BlockSpec((tm, w), lambda i: (i, 0)) for w in widths]
    out_shape = [jax.ShapeDtypeStruct((nrows, w), BF16) for w in widths]
    if not rope:
        out_specs += [pl.BlockSpec((tm, w), lambda i: (i, 0)) for w in (WK_B, WD, WK_B, WD)]
        out_shape += [jax.ShapeDtypeStruct((nrows, w), F32) for w in (WK_B, WD, WK_B, WD)]
    return pl.pallas_call(
        functools.partial(_prep_kernel, rope=rope),
        grid=(nt,),
        in_specs=in_specs,
        out_specs=out_specs,
        out_shape=out_shape,
        compiler_params=_params(("arbitrary",)),
        name="prep_lat" if rope else "prep_ctx",
    )(*args)


def _softmax_pv(scores, values, sink):
    m = scores[0].max(axis=-1, keepdims=True)
    for s in scores[1:]:
        m = jnp.maximum(m, s.max(axis=-1, keepdims=True))
    if sink is not None:
        m = jnp.maximum(m, sink)
    den = jnp.zeros_like(m) if sink is None else jnp.exp(sink - m)
    acc = None
    for s, v in zip(scores, values):
        p = jnp.exp(s - m)
        den = den + p.sum(axis=-1, keepdims=True)
        pv = _dot(p.astype(BF16), v)
        acc = pv if acc is None else acc + pv
    return acc / den


def _hs(x, h):
    return x[:, HEAD_DIM * h:HEAD_DIM * (h + 1)]


GQA = WIN_HEADS // WIN_KV


def _stack_group(q, sink_ref, kv, rows):
    qs = jnp.concatenate([_hs(q, GQA * kv + j) for j in range(GQA)], axis=0)
    part = lax.broadcasted_iota(jnp.int32, (GQA * rows, 1), 0) // rows
    sink = jnp.zeros((GQA * rows, 1), F32)
    for j in range(GQA):
        sink = jnp.where(part == j, sink_ref[0, GQA * kv + j], sink)
    return qs, sink


CTX_BPS = 1


def _ctx_attn_kernel(sink_ref, q_ref, k_ref, v_ref, qd_ref, kd_ref, vd_ref, ob_ref, od_ref):
    for u in range(CTX_BPS):
        rows = slice(SEQ * u, SEQ * (u + 1))
        q, k, v = q_ref[rows, :], k_ref[rows, :], v_ref[rows, :]
        for kv in range(WIN_KV):
            qs, sink = _stack_group(q, sink_ref, kv, SEQ)
            s = _dot_nt(qs, _hs(k, kv))
            o = _softmax_pv([s], [_hs(v, kv)], sink)
            for j in range(GQA):
                h = GQA * kv + j
                ob_ref[rows, HEAD_DIM * h:HEAD_DIM * (h + 1)] = o[SEQ * j:SEQ * (j + 1)].astype(BF16)
        qd, kd, vd = qd_ref[rows, :], kd_ref[rows, :], vd_ref[rows, :]
        for h in range(NA_HEADS):
            s = _dot_nt(_hs(qd, h), _hs(kd, h))
            o = _softmax_pv([s], [_hs(vd, h)], None)
            od_ref[rows, HEAD_DIM * h:HEAD_DIM * (h + 1)] = o.astype(BF16)


def _ctx_attn(sink, q, k, v, qd, kd, vd):
    blk = lambda w: pl.BlockSpec((SEQ * CTX_BPS, w), lambda b: (b, 0))
    return pl.pallas_call(
        _ctx_attn_kernel,
        grid=(BATCH // CTX_BPS,),
        in_specs=[pl.BlockSpec(memory_space=pltpu.SMEM),
                  blk(WQ_B), blk(WK_B), blk(WK_B), blk(WD), blk(WD), blk(WD)],
        out_specs=[blk(WQ_B), blk(WD)],
        out_shape=[jax.ShapeDtypeStruct((N_CTX, WQ_B), BF16), jax.ShapeDtypeStruct((N_CTX, WD), BF16)],
        compiler_params=_params(("arbitrary",)),
        name="ctx_attn",
    )(sink, q, k, v, qd, kd, vd)


WIN_SPAN = 3 * WIN


WIN_QPS = 4


def _win_attn_kernel(sink_ref, q_ref, k_ref, v_ref, kc_ref, vc_ref, o_ref):
    kc, vc = kc_ref[...], vc_ref[...]
    rows = GQA * WIN
    for u in range(WIN_QPS):
        qi = pl.program_id(1) * WIN_QPS + u
        qrows = slice(WIN * u, WIN * (u + 1))
        start = pl.multiple_of(jnp.clip(qi * WIN - WIN, 0, DEC_SEQ - WIN_SPAN), WIN)
        q = q_ref[qrows, :]
        kw = k_ref[pl.ds(start, WIN_SPAN), :]
        vw = v_ref[pl.ds(start, WIN_SPAN), :]
        qpos = qi * WIN + lax.broadcasted_iota(jnp.int32, (rows, WIN_SPAN), 0) % WIN
        kpos = start + lax.broadcasted_iota(jnp.int32, (rows, WIN_SPAN), 1)
        valid = jnp.abs(qpos - kpos) <= WIN
        for kv in range(WIN_KV):
            qs, sink = _stack_group(q, sink_ref, kv, WIN)
            s_loc = jnp.where(valid, _dot_nt(qs, _hs(kw, kv)), NEG)
            s_ctx = _dot_nt(qs, _hs(kc, kv))
            o = _softmax_pv([s_loc, s_ctx], [_hs(vw, kv), _hs(vc, kv)], sink)
            for j in range(GQA):
                h = GQA * kv + j
                o_ref[qrows, HEAD_DIM * h:HEAD_DIM * (h + 1)] = o[WIN * j:WIN * (j + 1)].astype(BF16)


def _win_attn(sink, q, k, v, kc, vc):
    nq = DEC_SEQ // (WIN * WIN_QPS)
    return pl.pallas_call(
        _win_attn_kernel,
        grid=(DEC_BATCH, nq),
        in_specs=[
            pl.BlockSpec(memory_space=pltpu.SMEM),
            pl.BlockSpec((WIN * WIN_QPS, WQ_B), lambda b, i: (b * nq + i, 0)),
            pl.BlockSpec((None, DEC_SEQ, WK_B), lambda b, i: (b, 0, 0)),
            pl.BlockSpec((None, DEC_SEQ, WK_B), lambda b, i: (b, 0, 0)),
            pl.BlockSpec((None, PAST_LEN, WK_B), lambda b, i: (b, 0, 0)),
            pl.BlockSpec((None, PAST_LEN, WK_B), lambda b, i: (b, 0, 0)),
        ],
        out_specs=pl.BlockSpec((WIN * WIN_QPS, WQ_B), lambda b, i: (b * nq + i, 0)),
        out_shape=jax.ShapeDtypeStruct((N_LAT, WQ_B), BF16),
        compiler_params=_params(("arbitrary", "arbitrary")),
        name="win_attn",
    )(sink, q, k.reshape(DEC_BATCH, DEC_SEQ, WK_B), v.reshape(DEC_BATCH, DEC_SEQ, WK_B), kc, vc)


NA_ROWS = DEC_SEQ // GRID_W
NA_G = 4
NA_NG = NA_ROWS // NA_G
NA_UROWS = NA_KH + NA_G - 1
NA_UKEYS = NA_UROWS * GRID_W
NA_QROWS = NA_G * GRID_W


def _na_union_start(g):
    return jnp.clip(g * NA_G - NA_KH // 2, 0, NA_ROWS - NA_UROWS)


NA_GPS = 4


def _natten_kernel(q_ref, k_ref, v_ref, kc_ref, vc_ref, bias_ref, o_ref):
    kc, vc = kc_ref[...], vc_ref[...]
    for u in range(NA_GPS):
        g = pl.program_id(1) * NA_GPS + u
        variant = jnp.where(g == 0, 0, jnp.where(g == NA_NG - 1, 2, 1))
        qrows = slice(NA_QROWS * u, NA_QROWS * (u + 1))
        start = pl.multiple_of(_na_union_start(g) * GRID_W, GRID_W)
        q = q_ref[qrows, :]
        kw = k_ref[pl.ds(start, NA_UKEYS), :]
        vw = v_ref[pl.ds(start, NA_UKEYS), :]
        for h in range(NA_HEADS):
            qh = _hs(q, h)
            s_loc = _dot_nt(qh, _hs(kw, h)) + bias_ref[variant, h]
            s_ctx = _dot_nt(qh, _hs(kc, h))
            o = _softmax_pv([s_loc, s_ctx], [_hs(vw, h), _hs(vc, h)], None)
            o_ref[qrows, HEAD_DIM * h:HEAD_DIM * (h + 1)] = o.astype(BF16)


def _natten(q, k, v, kc, vc, bias):
    nsteps = NA_NG // NA_GPS
    rows = NA_QROWS * NA_GPS
    return pl.pallas_call(
        _natten_kernel,
        grid=(DEC_BATCH, nsteps),
        in_specs=[
            pl.BlockSpec((rows, WD), lambda b, g: (b * nsteps + g, 0)),
            pl.BlockSpec((None, DEC_SEQ, WD), lambda b, g: (b, 0, 0)),
            pl.BlockSpec((None, DEC_SEQ, WD), lambda b, g: (b, 0, 0)),
            pl.BlockSpec((None, PAST_LEN, WD), lambda b, g: (b, 0, 0)),
            pl.BlockSpec((None, PAST_LEN, WD), lambda b, g: (b, 0, 0)),
            pl.BlockSpec(bias.shape, lambda b, g: (0, 0, 0, 0)),
        ],
        out_specs=pl.BlockSpec((rows, WD), lambda b, g: (b * nsteps + g, 0)),
        out_shape=jax.ShapeDtypeStruct((N_LAT, WD), BF16),
        compiler_params=_params(("arbitrary", "arbitrary")),
        name="natten",
    )(q, k.reshape(DEC_BATCH, DEC_SEQ, WD), v.reshape(DEC_BATCH, DEC_SEQ, WD), kc, vc, bias)


def _natten_bias(rpb):
    c = np.arange(GRID_W)
    cstart = np.clip(c - NA_KW // 2, 0, GRID_W - NA_KW)
    inwin = (c[None, :] >= cstart[:, None]) & (c[None, :] < cstart[:, None] + NA_KW)
    dc = np.clip(c[None, :] - c[:, None] + NA_KW - 1, 0, 2 * NA_KW - 2)
    pick = (dc[None] == np.arange(2 * NA_KW - 1)[:, None, None]).astype(np.float32)
    toep = jnp.einsum('hdj,jck->hdck', rpb.astype(F32), jnp.asarray(pick),
                      precision=lax.Precision.HIGHEST)
    toep = jnp.where(inwin[None, None], toep, NEG)
    masked = jnp.full((NA_HEADS, GRID_W, GRID_W), NEG, F32)
    variants = []
    for g in (0, 1, NA_NG - 1):
        u0 = int(np.clip(g * NA_G - NA_KH // 2, 0, NA_ROWS - NA_UROWS))
        rows = []
        for j in range(NA_G):
            r = g * NA_G + j
            w0 = int(np.clip(r - NA_KH // 2, 0, NA_ROWS - NA_KH))
            blocks = []
            for i in range(NA_UROWS):
                kr = u0 + i
                blocks.append(toep[:, kr - r + NA_KH - 1] if w0 <= kr < w0 + NA_KH else masked)
            rows.append(jnp.concatenate(blocks, axis=-1))
        variants.append(jnp.concatenate(rows, axis=-2))
    return jnp.stack(variants, axis=0)


def _merge_kernel(xc_ref, xl_ref, m_ref, g_ref, of_ref, ob_ref, ag_ref, on_ref, bd_ref,
                  ybc_ref, ybl_ref, ycc_ref, ycl_ref, ydc_ref, ydl_ref,
                  wg_ref, bg_ref, wa_ref, wb_ref, wc_ref, wd_ref, wo_ref,
                  g2_ref, wr_ref, br_ref, wsg_ref, wsu_ref, wsd_ref, tri_ref, ut_ref,
                  xa_ref, sl_ref, slt_ref, cnt_ref, xs_ref):
    x = _pick_group(TM, xc_ref, xl_ref)
    h = _prenorm(x, g_ref[...], m_ref[0:1, :], m_ref[1:2, :]).astype(BF16)
    o = of_ref[...] + ob_ref[...]
    ya = _head_rms(o, bd_ref[...], on_ref[...]) * _silu(ag_ref[...])
    yb = _pick_group(TM, ybc_ref, ybl_ref)
    yc = _pick_group(TM, ycc_ref, ycl_ref)
    yd = _pick_group(TM, ydc_ref, ydl_ref)
    d = D_MODEL
    merged = jnp.zeros((TM, d), F32)
    branches = ((ya.astype(BF16), wa_ref), (yb, wb_ref), (yc, wc_ref), (yd, wd_ref))
    for b, (y, w_ref) in enumerate(branches):
        gate = jax.nn.sigmoid(_dot(h, wg_ref[:, d * b:d * (b + 1)]) + bg_ref[:, d * b:d * (b + 1)])
        merged = merged + gate * _dot(y, w_ref[...])
    x1 = x + m_ref[2:3, :] * _dot(merged.astype(BF16), wo_ref[...])
    for t in range(TM // TR):
        _route_rows(x1[TR * t:TR * (t + 1)], t, m_ref, g2_ref, wr_ref, br_ref, wsg_ref, wsu_ref, wsd_ref,
                    tri_ref, ut_ref, xa_ref, sl_ref, slt_ref, cnt_ref, xs_ref)


def _merge(x_c, x_l, mods, norm_g, o_f, o_b, z, onorm, bd, yb_c, yb_l, yc_c, yc_l, yd_c, yd_l,
           w_gate, b_gate, wa, wb, wc, wd, wo, norm2_g, w_router, b_router, wsg, wsu, wsd, tri, ut):
    full = lambda a: pl.BlockSpec(a.shape, lambda i: (0, 0))
    tpm = TM // TR
    return pl.pallas_call(
        _merge_kernel,
        grid=(NTOK // TM,),
        in_specs=_two_group_specs(TM, D_MODEL) + [
            pl.BlockSpec((None, 6, D_MODEL), lambda i: (_seg_of_tile(i, TM), 0, 0)),
            full(norm_g),
            pl.BlockSpec((TM, HG_W), lambda i: (i, 0)),
            pl.BlockSpec((TM, HG_W), lambda i: (i, 0)),
            pl.BlockSpec((TM, HG_W), lambda i: (i, COL_AG // HG_W)),
            full(onorm), full(bd),
        ] + _two_group_specs(TM, WQ_B) + _two_group_specs(TM, POOL_W) + _two_group_specs(TM, WD) + [
            full(w_gate), full(b_gate), full(wa), full(wb), full(wc), full(wd), full(wo),
            full(norm2_g), full(w_router), full(b_router), full(wsg), full(wsu), full(wsd), full(tri), full(ut),
        ],
        out_specs=[
            pl.BlockSpec((TM, XW), lambda i: (i, 0)),
            pl.BlockSpec((TM, E_PAD), lambda i: (i, 0)),
            pl.BlockSpec((tpm, 8, TR), lambda i: (i, 0, 0)),
            pl.BlockSpec((tpm, 1, E_PAD), lambda i: (i, 0, 0)),
            pl.BlockSpec((TM, D_MODEL), lambda i: (i, 0)),
        ],
        out_shape=[
            jax.ShapeDtypeStruct((NTOK, XW), BF16),
            jax.ShapeDtypeStruct((NTOK, E_PAD), F32),
            jax.ShapeDtypeStruct((NT_R, 8, TR), F32),
            jax.ShapeDtypeStruct((NT_R, 1, E_PAD), F32),
            jax.ShapeDtypeStruct((NTOK, D_MODEL), F32),
        ],
        compiler_params=_params(("arbitrary",)),
        name="merge",
    )(x_c, x_l, mods, norm_g, o_f, o_b, z, onorm, bd, yb_c, yb_l, yc_c, yc_l, yd_c, yd_l,
      w_gate, b_gate, wa, wb, wc, wd, wo, norm2_g, w_router, b_router, wsg, wsu, wsd, tri, ut)


E_PAD = 128
TR = 256
RT = 512
SLAB = 16
NT_R = NTOK // TR
S_LOC = 1536
S_MAX = NTOK * TOP_K + NT_R * N_EXPERTS * (SLAB - 1) + N_EXPERTS * (RT - SLAB)
N_XT = S_MAX // RT
XW = D_MODEL + 2 * E_PAD
assert S_LOC >= TR * TOP_K + N_EXPERTS * (SLAB - 1) and S_LOC % 128 == 0 and S_MAX % RT == 0


def _route_rows(x, t, m_ref, g_ref, wr_ref, br_ref, wsg_ref, wsu_ref, wsd_ref, tri_ref, ut_ref,
                xa_ref, sl_ref, slt_ref, cnt_ref, xs_ref):
    rows = slice(TR * t, TR * (t + 1))
    lane = lax.broadcasted_iota(jnp.int32, (TR, E_PAD), 1)
    ms = jnp.mean(x * x, axis=-1, keepdims=True)
    y = x * lax.rsqrt(ms + EPS) * g_ref[...]
    h = y * (1.0 + m_ref[4:5, :]) + m_ref[3:4, :]
    h_hi = h.astype(BF16)
    h_lo = (h - h_hi.astype(F32)).astype(BF16)
    w = wr_ref[...]
    w_hi = w.astype(BF16)
    w_lo = (w - w_hi.astype(F32)).astype(BF16)
    logits = _dot(h_hi, w_hi) + _dot(h_hi, w_lo) + _dot(h_lo, w_hi)
    scores = jax.nn.sigmoid(logits)
    sel = jnp.where(lane < N_EXPERTS, scores + br_ref[...], -jnp.inf)
    picked = jnp.zeros((TR, E_PAD), F32)
    hot = jnp.zeros((TR, E_PAD), F32)
    idxs = []
    lane_f = lane.astype(F32)
    for _ in range(TOP_K):
        mx = sel.max(axis=-1, keepdims=True)
        idx = jnp.min(jnp.where(sel == mx, lane_f, float(E_PAD)), axis=-1, keepdims=True)
        hit = lane_f == idx
        picked = jnp.where(hit, scores, picked)
        hot = jnp.where(hit, 1.0, hot)
        sel = jnp.where(hit, -jnp.inf, sel)
        idxs.append(idx)
    wts = ROUTED_SCALE * picked / picked.sum(axis=-1, keepdims=True)

    cnt = hot.sum(axis=0, keepdims=True)
    pad = jnp.floor((cnt + (SLAB - 1.0)) * (1.0 / SLAB)) * SLAB
    loc = _dot(jnp.broadcast_to(pad, (8, E_PAD)).astype(BF16), ut_ref[...])[0:1, :]
    rank = _dot(tri_ref[...], hot.astype(BF16))
    slotmat = loc + rank
    sl = jnp.zeros((TR, E_PAD), F32)
    for k in range(TOP_K):
        s_k = jnp.sum(jnp.where(lane_f == idxs[k], slotmat, 0.0), axis=-1, keepdims=True)
        sl = jnp.where(lane == k, s_k, sl)
    sl_ref[rows, :] = sl
    slt_ref[t] = sl.T[0:8, :]
    cnt_ref[t] = pad

    w16 = wts.astype(BF16)
    xa_ref[rows, 0:D_MODEL] = h_hi
    xa_ref[rows, D_MODEL:D_MODEL + E_PAD] = w16
    xa_ref[rows, D_MODEL + E_PAD:XW] = (wts - w16.astype(F32)).astype(BF16)

    a = _silu(_dot(h_hi, wsg_ref[...])) * _dot(h_hi, wsu_ref[...])
    xs_ref[rows, :] = x + m_ref[5:6, :] * _dot(a.astype(BF16), wsd_ref[...])


def _slab_copy(src, src_row, dst, dst_row, sem):
    hint = lambda r: r if isinstance(r, int) else pl.multiple_of(r, SLAB)
    return pltpu.make_async_copy(src.at[pl.ds(hint(src_row), SLAB), :],
                                 dst.at[pl.ds(hint(dst_row), SLAB), :], sem)


NSL = S_LOC // SLAB


def _for_each_slab(grow_ref, ns_ref, t, fn):
    def per_slab(n, c):
        fn(n * SLAB, grow_ref[t * NSL + n])
        return c

    lax.fori_loop(0, ns_ref[t], per_slab, 0)


def _compact_kernel(grow_ref, ns_ref, nd_ref, toff_ref, tn_ref, nu_ref,
                    xa_ref, slt_ref, xs_hbm, xc_scr, zero_scr, sem):
    t = pl.program_id(0)
    slot = t % 2

    def wait_n(n, s):
        def body(_, c):
            _slab_copy(xc_scr.at[s], 0, xs_hbm, 0, sem.at[s]).wait()
            return c
        lax.fori_loop(0, n, body, 0)

    def unused_tile_copy(j):
        row = pl.multiple_of((nu_ref[0] + j) * RT, RT)
        return pltpu.make_async_copy(zero_scr, xs_hbm.at[pl.ds(row, RT), :], sem.at[2])

    @pl.when(t == 0)
    def _():
        zero_scr[...] = jnp.zeros_like(zero_scr)

        def body(j, c):
            unused_tile_copy(j).start()
            return c
        lax.fori_loop(0, N_XT - nu_ref[0], body, 0)

    @pl.when(t >= 2)
    def _():
        wait_n(nd_ref[t - 2], slot)

    row = lax.broadcasted_iota(jnp.int32, (S_LOC, TR), 0)
    slt = slt_ref[...].astype(jnp.int32)
    hit = row == slt[0:1, :]
    for k in range(1, TOP_K):
        hit = hit | (row == slt[k:k + 1, :])
    onehot = jnp.where(hit, 1.0, 0.0).astype(BF16)
    xc_scr[slot] = _dot(onehot, xa_ref[...]).astype(BF16)

    _for_each_slab(grow_ref, ns_ref, t,
                   lambda lr, gr: _slab_copy(xc_scr.at[slot], lr, xs_hbm, gr, sem.at[slot]).start())

    @pl.when(t < N_EXPERTS)
    def _():
        def body(s, c):
            _slab_copy(zero_scr, 0, xs_hbm, toff_ref[t] + s * SLAB, sem.at[slot]).start()
            return c
        lax.fori_loop(0, tn_ref[t], body, 0)

    @pl.when(t == NT_R - 1)
    def _():
        wait_n(nd_ref[t], slot)
        wait_n(nd_ref[t - 1], 1 - slot)

        def body(j, c):
            unused_tile_copy(j).wait()
            return c
        lax.fori_loop(0, N_XT - nu_ref[0], body, 0)


def _compact(meta, xa, slt):
    grid_spec = pltpu.PrefetchScalarGridSpec(
        num_scalar_prefetch=6,
        grid=(NT_R,),
        in_specs=[
            pl.BlockSpec((TR, XW), lambda i, *_: (i, 0)),
            pl.BlockSpec((None, 8, TR), lambda i, *_: (i, 0, 0)),
        ],
        out_specs=pl.BlockSpec(memory_space=pl.ANY),
        scratch_shapes=[pltpu.VMEM((2, S_LOC, XW), BF16), pltpu.VMEM((RT, XW), BF16),
                        pltpu.SemaphoreType.DMA((3,))],
    )
    return pl.pallas_call(
        _compact_kernel,
        grid_spec=grid_spec,
        out_shape=jax.ShapeDtypeStruct((S_MAX, XW), BF16),
        compiler_params=_params(("arbitrary",)),
        name="compact",
    )(meta['grow'], meta['ns'], meta['nd'], meta['toff'], meta['tn'], meta['nu'], xa, slt)


def _expert_kernel(te_ref, ti_ref, nu_ref, nx_ref, par_ref, xs_ref, wg_hbm, wu_hbm, wd_hbm, y_ref,
                   wg_f, wu_f, wd_f, wgu_s, wd_s, sem, *, layer):
    i = pl.program_id(0)
    e = te_ref[i]
    slot = par_ref[i]
    used = i < nu_ref[0]
    first = used & ((i == 0) | (e != te_ref[jnp.maximum(i - 1, 0)]))

    def weight_copies(expert, s):
        return (pltpu.make_async_copy(wg_hbm.at[layer, expert], wg_f.at[s], sem.at[s]),
                pltpu.make_async_copy(wu_hbm.at[layer, expert], wu_f.at[s], sem.at[s]),
                pltpu.make_async_copy(wd_hbm.at[layer, expert], wd_f.at[s], sem.at[s]))

    @pl.when(i == 0)
    def _():
        for c in weight_copies(e, slot):
            c.start()

    @pl.when(first)
    def _():
        for c in weight_copies(e, slot):
            c.wait()
        wgu_s[:, 0:D_EXPERT] = wg_f[slot].astype(BF16)
        wgu_s[:, D_EXPERT:2 * D_EXPERT] = wu_f[slot].astype(BF16)
        wd_s[...] = wd_f[slot].astype(BF16)

    @pl.when(first & (nx_ref[i] >= 0))
    def _():
        for c in weight_copies(nx_ref[i], 1 - slot):
            c.start()

    @pl.when(used)
    def _():
        x = xs_ref[:, 0:D_MODEL]
        gw = xs_ref[:, D_MODEL:D_MODEL + E_PAD].astype(F32) + xs_ref[:, D_MODEL + E_PAD:XW].astype(F32)
        lane = lax.broadcasted_iota(jnp.int32, (RT, E_PAD), 1)
        ge = jnp.sum(jnp.where(lane == e, gw, 0.0), axis=-1, keepdims=True)
        gu = _dot(x, wgu_s[...])
        a = _silu(gu[:, 0:D_EXPERT]) * gu[:, D_EXPERT:2 * D_EXPERT]
        y_ref[...] = _dot((a * ge).astype(BF16), wd_s[...]).astype(BF16)

    @pl.when(i >= nu_ref[0])
    def _():
        y_ref[...] = jnp.zeros_like(y_ref)


def _experts(meta, layer, xs, w_eg, w_eu, w_ed):
    any_spec = pl.BlockSpec(memory_space=pl.ANY)
    grid_spec = pltpu.PrefetchScalarGridSpec(
        num_scalar_prefetch=5,
        grid=(N_XT,),
        in_specs=[pl.BlockSpec((RT, XW), lambda i, te, ti, *_: (ti[i], 0)), any_spec, any_spec, any_spec],
        out_specs=pl.BlockSpec((RT, D_MODEL), lambda i, *_: (i, 0)),
        scratch_shapes=[pltpu.VMEM((2, D_MODEL, D_EXPERT), F32), pltpu.VMEM((2, D_MODEL, D_EXPERT), F32),
                        pltpu.VMEM((2, D_EXPERT, D_MODEL), F32),
                        pltpu.VMEM((D_MODEL, 2 * D_EXPERT), BF16), pltpu.VMEM((D_EXPERT, D_MODEL), BF16),
                        pltpu.SemaphoreType.DMA((2,))],
    )
    return pl.pallas_call(
        functools.partial(_expert_kernel, layer=layer),
        grid_spec=grid_spec,
        out_shape=jax.ShapeDtypeStruct((S_MAX, D_MODEL), BF16),
        compiler_params=_params(("arbitrary",)),
        name="experts",
    )(meta['te'], meta['ti'], meta['nu'], meta['nx'], meta['par'], xs, w_eg, w_eu, w_ed)


def _combine_kernel(grow_ref, ns_ref, xs_ref, m_ref, sl_ref, y_hbm, oc_ref, ol_ref, yc_scr, sem):
    t = pl.program_id(0)
    slot = t % 2

    def issue(tt, s):
        _for_each_slab(grow_ref, ns_ref, tt,
                       lambda lr, gr: _slab_copy(y_hbm, gr, yc_scr.at[s], lr, sem.at[s]).start())

    @pl.when(t == 0)
    def _():
        yc_scr[...] = jnp.zeros_like(yc_scr)
        issue(0, 0)

    @pl.when(t + 1 < NT_R)
    def _():
        issue(t + 1, 1 - slot)

    def wait_body(_, c):
        _slab_copy(y_hbm, 0, yc_scr.at[slot], 0, sem.at[slot]).wait()
        return c
    lax.fori_loop(0, ns_ref[t], wait_body, 0)

    col = lax.broadcasted_iota(jnp.int32, (TR, S_LOC), 1)
    sl = sl_ref[...].astype(jnp.int32)
    hit = col == sl[:, 0:1]
    for k in range(1, TOP_K):
        hit = hit | (col == sl[:, k:k + 1])
    onehot = jnp.where(hit, 1.0, 0.0).astype(BF16)
    out = xs_ref[...] + m_ref[5:6, :] * _dot(onehot, yc_scr[slot])

    @pl.when(t < N_CTX // TR)
    def _():
        oc_ref[...] = out

    @pl.when(t >= N_CTX // TR)
    def _():
        ol_ref[...] = out


def _combine(meta, xsh, mods, sl, y):
    nct = N_CTX // TR
    out_specs = [pl.BlockSpec((TR, D_MODEL), lambda i, *_: (jnp.minimum(i, nct - 1), 0)),
                 pl.BlockSpec((TR, D_MODEL), lambda i, *_: (jnp.maximum(i - nct, 0), 0))]
    out_shape = [jax.ShapeDtypeStruct((N_CTX, D_MODEL), F32), jax.ShapeDtypeStruct((N_LAT, D_MODEL), F32)]
    grid_spec = pltpu.PrefetchScalarGridSpec(
        num_scalar_prefetch=2,
        grid=(NT_R,),
        in_specs=[
            pl.BlockSpec((TR, D_MODEL), lambda i, *_: (i, 0)),
            pl.BlockSpec((None, 6, D_MODEL), lambda i, *_: (_seg_of_tile(i, TR), 0, 0)),
            pl.BlockSpec((TR, E_PAD), lambda i, *_: (i, 0)),
            pl.BlockSpec(memory_space=pl.ANY),
        ],
        out_specs=out_specs,
        scratch_shapes=[pltpu.VMEM((2, S_LOC, D_MODEL), BF16), pltpu.SemaphoreType.DMA((2,))],
    )
    return pl.pallas_call(
        _combine_kernel,
        grid_spec=grid_spec,
        out_shape=out_shape,
        compiler_params=_params(("arbitrary",)),
        name="combine",
    )(meta['grow'], meta['ns'], xsh, mods, sl, y)


def _route_meta(cnt):
    pc = cnt[:, 0, :N_EXPERTS].astype(jnp.int32)
    tot = pc.sum(axis=0)
    tot_pad = ((tot + RT - 1) // RT) * RT
    ends = jnp.cumsum(tot_pad)
    base = ends - tot_pad
    dst = base[None, :] + jnp.cumsum(pc, axis=0) - pc
    cum = jnp.cumsum(pc, axis=1)
    ns = cum[:, -1] // SLAB
    lrow = jnp.arange(NSL, dtype=jnp.int32)[None, :, None] * SLAB
    owner = jnp.minimum(jnp.sum(cum[:, None, :] <= lrow, axis=2), N_EXPERTS - 1)
    mine = owner[:, :, None] == jnp.arange(N_EXPERTS, dtype=jnp.int32)[None, None, :]
    grow = jnp.sum(jnp.where(mine, (dst - cum + pc)[:, None, :], 0), axis=2) + lrow[:, :, 0]
    tn = (tot_pad - tot) // SLAB
    nd = ns + jnp.pad(tn, (0, NT_R - N_EXPERTS))
    n_used = ends[-1] // RT
    ti = jnp.minimum(jnp.arange(N_XT, dtype=jnp.int32), n_used - 1)
    te = jnp.minimum(jnp.sum(ends[None, :] <= (ti * RT)[:, None], axis=1), N_EXPERTS - 1)
    eidx = jnp.arange(N_EXPERTS, dtype=jnp.int32)
    has = tot_pad > 0
    later = has[None, :] & (eidx[None, :] > eidx[:, None])
    nxt_e = jnp.min(jnp.where(later, eidx[None, :], N_EXPERTS), axis=1)
    nxt_e = jnp.where(nxt_e == N_EXPERTS, -1, nxt_e)
    ordinal = jnp.cumsum(has.astype(jnp.int32)) - 1
    is_e = te[:, None] == eidx[None, :]
    nx = jnp.sum(jnp.where(is_e, nxt_e[None, :], 0), axis=1)
    par = jnp.sum(jnp.where(is_e, ordinal[None, :], 0), axis=1) & 1
    i32 = lambda a: a.astype(jnp.int32)
    return dict(grow=i32(grow.reshape(-1)), ns=i32(ns), nd=i32(nd), toff=i32(base + tot), tn=i32(tn),
                te=i32(te), ti=i32(ti), nu=i32(n_used.reshape(1)), nx=i32(nx), par=i32(par))


def _moe_routed(layer, routed, mods, w_eg, w_eu, w_ed):
    xa, sl, slt, cnt, xsh = routed
    meta = _route_meta(cnt)
    xs = _compact(meta, xa, slt)
    y = _experts(meta, layer, xs, w_eg, w_eu, w_ed)
    return _combine(meta, xsh, mods, sl, y)


def _rope_tables(width):
    t = np.arange(DEC_SEQ)
    quarter = HEAD_DIM // 4
    inv = (ROPE_BASE ** (-np.arange(quarter) / quarter)).astype(np.float32)
    ang_r = (t // GRID_W).astype(np.float32)[:, None] * inv[None]
    ang_c = (t % GRID_W).astype(np.float32)[:, None] * inv[None]
    cos = np.concatenate([np.cos(ang_r), np.cos(ang_r), np.cos(ang_c), np.cos(ang_c)], axis=1)
    sin = np.concatenate([-np.sin(ang_r), np.sin(ang_r), -np.sin(ang_c), np.sin(ang_c)], axis=1)
    reps = width // HEAD_DIM
    return (jnp.asarray(np.tile(cos, (1, reps)), F32), jnp.asarray(np.tile(sin, (1, reps)), F32))


def _permute_w_in(w):
    a = w[:, 0:1280]
    bq, bk, bv = w[:, 1280:1664], w[:, 1664:1792], w[:, 1792:1920]
    cu = w[:, 1920:2176]
    d = w[:, 2176:2944]
    pad = jnp.zeros((w.shape[0], Z_W - 2944), w.dtype)
    return jnp.concatenate([a, cu, d, bq, bk, bv, pad], axis=1)


def _block_diag(blocks):
    g = blocks.shape[0]
    eye = jnp.eye(g, dtype=blocks.dtype)
    return jnp.einsum('gh,gij->gihj', eye, blocks).reshape(g * HEAD_DIM, g * HEAD_DIM)


def kernel(x_prompt, x_sample, cache_win_k, cache_win_v, cache_na_k, cache_na_v, state_hgrn, c, c_ctx, w_mod, b_mod, norm1_g, norm2_g, w_in, w_mgate, b_mgate, hg_lb, hg_onorm, win_qn, win_kn, win_sink, pool_w, pool_scale, na_qn, na_kn, na_rpb, w_branch, w_out, w_router, b_router, w_eg, w_eu, w_ed, w_sg, w_su, w_sd):
    lbp = jax.nn.softmax(hg_lb.astype(F32), axis=0)
    lbs = jnp.cumsum(lbp, axis=0) - lbp[0:1]

    cvec8 = jnp.concatenate([c_ctx[None], c, jnp.zeros((3, D_MODEL), F32)], axis=0)
    mods_all = _modulation(cvec8, w_mod, b_mod).reshape(DEPTH, 8, 6, D_MODEL)

    bd384 = jnp.asarray(_bd_ones(WQ_B), BF16)
    bd256 = bd384[:HG_W, :HG_W]
    bd256_f32 = jnp.asarray(_bd_ones(HG_W), F32)
    rope_q = _rope_tables(WQ_B)
    rope_k = _rope_tables(WK_B)
    tile = lambda g, reps: jnp.tile(g, reps)[None, :]
    tri = jnp.asarray(np.tril(np.ones((TR, TR), np.float32), -1), BF16)
    ut = jnp.asarray(np.triu(np.ones((E_PAD, E_PAD), np.float32), 1), BF16)

    x_c, x_l = x_prompt.reshape(N_CTX, D_MODEL), x_sample.reshape(N_LAT, D_MODEL)
    new_k, new_v, new_kd, new_vd, new_s = [], [], [], [], []
    for l in range(DEPTH):
        mods = mods_all[l]
        z = _projection(x_c, x_l, mods, norm1_g[l][None], _permute_w_in(w_in[l]).astype(BF16))

        o_f, o_b, sfin_f, sfin_b = _hgrn(z, lbs[l], state_hgrn.astype(F32), l, bd256_f32)
        new_s.append(jnp.stack([sfin_f, sfin_b], axis=1))

        w_pool = _block_diag(pool_w[l]).astype(BF16)
        yc_c = _pool(z, 0, BATCH, SEQ, w_pool, pool_scale[l][None])
        yc_l = _pool(z, N_CTX, DEC_BATCH, DEC_SEQ, w_pool, pool_scale[l][None])

        gains = (tile(win_qn[l], WIN_HEADS), tile(win_kn[l], WIN_KV), tile(na_qn[l], NA_HEADS), tile(na_kn[l], NA_HEADS))
        qb_c, kb_c, vb_c, qd_c, kd_c, vd_c, kb32, kd32, vb32, vd32 = _prep(z, 0, N_CTX, gains, bd384, None)
        qb_l, kb_l, vb_l, qd_l, kd_l, vd_l = _prep(z, N_CTX, N_LAT, gains, bd384, rope_q + rope_k)
        sink = win_sink[l][None]
        yb_c, yd_c = _ctx_attn(sink, qb_c, kb_c, vb_c, qd_c, kd_c, vd_c)
        kc = cache_win_k[:, l].reshape(DEC_BATCH, PAST_LEN, WK_B).astype(BF16)
        vc = cache_win_v[:, l].reshape(DEC_BATCH, PAST_LEN, WK_B).astype(BF16)
        yb_l = _win_attn(sink, qb_l, kb_l, vb_l, kc, vc)
        kcd = cache_na_k[:, l].reshape(DEC_BATCH, PAST_LEN, WD).astype(BF16)
        vcd = cache_na_v[:, l].reshape(DEC_BATCH, PAST_LEN, WD).astype(BF16)
        yd_l = _natten(qd_l, kd_l, vd_l, kcd, vcd, _natten_bias(na_rpb[l]))

        new_k.append(kb32.reshape(BATCH, SEQ, WIN_KV, HEAD_DIM))
        new_v.append(vb32.reshape(BATCH, SEQ, WIN_KV, HEAD_DIM))
        new_kd.append(kd32.reshape(BATCH, SEQ, NA_HEADS, HEAD_DIM))
        new_vd.append(vd32.reshape(BATCH, SEQ, NA_HEADS, HEAD_DIM))

        wbr = w_branch[l].astype(BF16)
        wr = jnp.pad(w_router[l], ((0, 0), (0, E_PAD - N_EXPERTS)))
        br = jnp.pad(b_router[l], (0, E_PAD - N_EXPERTS))[None]
        routed = _merge(x_c, x_l, mods, norm1_g[l][None], o_f, o_b, z, tile(hg_onorm[l], HG_HEADS), bd256,
                        yb_c, yb_l, yc_c, yc_l, yd_c, yd_l, w_mgate[l].astype(BF16), b_mgate[l][None],
                        wbr[0:256], wbr[256:640], wbr[640:896], wbr[896:1152], w_out[l].astype(BF16),
                        norm2_g[l][None], wr, br, w_sg[l].astype(BF16), w_su[l].astype(BF16),
                        w_sd[l].astype(BF16), tri, ut)
        x_c, x_l = _moe_routed(l, routed, mods, w_eg, w_eu, w_ed)

    y_p = x_c.reshape(BATCH, SEQ, D_MODEL)
    y_s = x_l.reshape(DEC_BATCH, DEC_SEQ, D_MODEL)
    return (y_p, y_s, jnp.stack(new_k, axis=1), jnp.stack(new_v, axis=1), jnp.stack(new_kd, axis=1),
            jnp.stack(new_vd, axis=1), jnp.stack(new_s, axis=1))
```
